```python
import math
import jax, jax.numpy as jnp
from jax import lax
import numpy as np

D_MODEL = 4096
BATCH = 4
SEQ = 4096
DEPTH = 1
DEC_BATCH = 1
DEC_SEQ = 16384
PAST_LEN = 128

W_A = D_MODEL // 2
DH_A = 64
H_A = W_A // (2 * DH_A)
W_B = D_MODEL // 2
N_B = 64
H_B = W_B // N_B
R_LORA = 128
C_SHIFT = 3 * W_B + 4 * R_LORA
N_IN = 3 * W_A + C_SHIFT + W_A + W_B + 2 * D_MODEL
Q_BLOCK = 128
ROPE_THETA = 10000.0
ATTN_SCALE = DH_A ** -0.5
NORM_EPS = 1e-6
SUBLN_EPS = 1e-5
LNX_EPS = 64e-5

kernel_name = "hybrid_diffattn_rwkv7_gated_encoder"


def rms_norm(x, g, eps=NORM_EPS):
    xf = x.astype(jnp.float32)
    y = xf * lax.rsqrt(jnp.mean(xf * xf, axis=-1, keepdims=True) + eps)
    return y.astype(x.dtype) * g


def lambda_init(layer_idx):
    return 0.8 - 0.6 * math.exp(-0.3 * layer_idx)


def rope(x):
    S = x.shape[1]
    half = DH_A // 2
    inv = 1.0 / (ROPE_THETA ** (jnp.arange(0, DH_A, 2, dtype=jnp.float32) / DH_A))
    ang = jnp.arange(S, dtype=jnp.float32)[:, None] * inv[None, :]
    cos = jnp.cos(ang)[None, :, None, None, :]
    sin = jnp.sin(ang)[None, :, None, None, :]
    xf = x.astype(jnp.float32)
    x1, x2 = xf[..., :half], xf[..., half:]
    return jnp.concatenate([x1 * cos - x2 * sin, x2 * cos + x1 * sin], axis=-1).astype(x.dtype)


def diff_attention(q, k, v, lam, subln_g, lam_init):
    B, S = q.shape[0], q.shape[1]
    nb = S // Q_BLOCK
    qb = q.reshape(B, nb, Q_BLOCK, H_A, 2, DH_A).swapaxes(0, 1)

    def block(qblk):
        s = jnp.einsum('bqhcd,bkhcd->bhcqk', qblk, k,
                       preferred_element_type=jnp.float32) * ATTN_SCALE
        pr = jax.nn.softmax(s, axis=-1)
        att = pr[:, :, 0] - lam * pr[:, :, 1]
        return jnp.einsum('bhqk,bkhe->bqhe', att.astype(v.dtype), v)

    o = lax.map(block, qb).swapaxes(0, 1).reshape(B, S, H_A, 2 * DH_A)
    o = rms_norm(o, subln_g, SUBLN_EPS) * (1.0 - lam_init)
    return o.reshape(B, S, W_A)


def centered_shift(p, mu_prev, mu_next):
    prev = jnp.pad(p[:, :-1], ((0, 0), (1, 0), (0, 0)))
    nxt = jnp.pad(p[:, 1:], ((0, 0), (0, 1), (0, 0)))
    return p + mu_prev * (prev - p) + mu_next * (nxt - p)


def wkv7_scan(r, w, k, v, kk, b, reverse):
    B, S = r.shape[0], r.shape[1]
    xs = tuple(t.astype(jnp.float32).reshape(B, S, H_B, N_B).swapaxes(0, 1)
               for t in (r, w, k, v, kk, b))

    def step(st, inp):
        r_t, w_t, k_t, v_t, kk_t, b_t = inp
        sa = jnp.einsum('bhij,bhj->bhi', st, -kk_t)
        st = st * w_t[:, :, None, :] + sa[..., None] * b_t[:, :, None, :] \
            + v_t[..., None] * k_t[:, :, None, :]
        y = jnp.einsum('bhij,bhj->bhi', st, r_t)
        return st, y

    st0 = jnp.zeros((B, H_B, N_B, N_B), jnp.float32)
    _, ys = lax.scan(step, st0, xs, reverse=reverse)
    return ys.swapaxes(0, 1)


def rwkv7_mix(slab, w0, w_lora, a0, a_lora, k_k, k_a, r_k, lnx_g, lnx_b):
    B, S = slab.shape[0], slab.shape[1]
    r = slab[..., :W_B]
    k = slab[..., W_B:2 * W_B]
    v = slab[..., 2 * W_B:3 * W_B]
    lo = slab[..., 3 * W_B:].reshape(B, S, 4, R_LORA)
    pw, pa = lo[:, :, 0:2], lo[:, :, 2:4]
    wl = (w0 + jnp.einsum('bsdr,drc->bsdc', jnp.tanh(pw), w_lora)).astype(jnp.float32)
    decay = jnp.exp(-jnp.exp(-jax.nn.softplus(-wl) - 0.5))
    a = jax.nn.sigmoid((a0 + jnp.einsum('bsdr,drc->bsdc', pa, a_lora)).astype(jnp.float32))
    kkh = (k * k_k).astype(jnp.float32).reshape(B, S, H_B, N_B)
    kkh = kkh / jnp.maximum(jnp.sqrt(jnp.sum(kkh * kkh, axis=-1, keepdims=True)), 1e-12)
    kk = kkh.reshape(B, S, W_B)
    kdir = k[:, :, None, :].astype(jnp.float32) * (1.0 + (a - 1.0) * k_a)
    bdir = kk[:, :, None, :] * a
    y = wkv7_scan(r, decay[:, :, 0], kdir[:, :, 0], v, kk, bdir[:, :, 0], reverse=False) \
        + wkv7_scan(r, decay[:, :, 1], kdir[:, :, 1], v, kk, bdir[:, :, 1], reverse=True)
    mu = jnp.mean(y, axis=-1, keepdims=True)
    var = jnp.mean(jnp.square(y - mu), axis=-1, keepdims=True)
    y = ((y - mu) * lax.rsqrt(var + LNX_EPS)).reshape(B, S, W_B) * lnx_g + lnx_b
    rh = r.astype(jnp.float32).reshape(B, S, 1, H_B, N_B)
    bonus_w = jnp.sum(rh * kdir.reshape(B, S, 2, H_B, N_B) * r_k, axis=(2, 4))
    bonus = bonus_w[..., None] * v.astype(jnp.float32).reshape(B, S, H_B, N_B)
    return (y + bonus.reshape(B, S, W_B)).astype(slab.dtype)


def mixer_layer(x, lam_init, norm_g, w_in, mu_prev, mu_next, lam_q1, lam_k1, lam_q2, lam_k2,
                subln_g, w0, w_lora, a0, a_lora, k_k, k_a, r_k, lnx_g, lnx_b, w_oA, w_oB, w_out):
    B, S, _ = x.shape
    h = rms_norm(x, norm_g)
    p = jnp.einsum('bsd,dn->bsn', h, w_in)
    q = p[..., 0:W_A].reshape(B, S, H_A, 2, DH_A)
    k = p[..., W_A:2 * W_A].reshape(B, S, H_A, 2, DH_A)
    v = p[..., 2 * W_A:3 * W_A].reshape(B, S, H_A, 2 * DH_A)
    slab = centered_shift(p[..., 3 * W_A:3 * W_A + C_SHIFT], mu_prev, mu_next)
    off = 3 * W_A + C_SHIFT
    z_a = p[..., off:off + W_A]
    z_b = p[..., off + W_A:off + W_A + W_B]
    off2 = off + W_A + W_B
    g_a = p[..., off2:off2 + D_MODEL]
    g_b = p[..., off2 + D_MODEL:off2 + 2 * D_MODEL]

    lam = (jnp.exp(jnp.sum(lam_q1.astype(jnp.float32) * lam_k1.astype(jnp.float32)))
           - jnp.exp(jnp.sum(lam_q2.astype(jnp.float32) * lam_k2.astype(jnp.float32)))
           + lam_init)
    y_a = diff_attention(rope(q), rope(k), v, lam, subln_g, lam_init)
    y_b = rwkv7_mix(slab, w0, w_lora, a0, a_lora, k_k, k_a, r_k, lnx_g, lnx_b)

    o_a = jnp.einsum('bsc,cd->bsd', y_a * jax.nn.silu(z_a), w_oA)
    o_b = jnp.einsum('bsc,cd->bsd', y_b * jax.nn.silu(z_b), w_oB)
    m = jax.nn.sigmoid(g_a) * o_a + jax.nn.sigmoid(g_b) * o_b
    return x + jnp.einsum('bsd,de->bse', m, w_out)


def setup_inputs(seed: int = 0) -> dict:
    key = jax.random.key(seed)
    ks = jax.random.split(key, 24)
    f32 = jnp.float32

    def nrm(k, shape, s):
        return jax.random.normal(k, shape, f32) * s

    return {
        "x_prompt": nrm(ks[0], (BATCH, SEQ, D_MODEL), 1.0),
        "x_sample": nrm(ks[1], (DEC_BATCH, DEC_SEQ, D_MODEL), 1.0),
        "norm_g": 1.0 + nrm(ks[2], (DEPTH, D_MODEL), 0.05),
        "w_in": nrm(ks[3], (DEPTH, D_MODEL, N_IN), D_MODEL ** -0.5),
        "mu_prev": jax.random.uniform(ks[4], (DEPTH, C_SHIFT), f32, 0.0, 0.5),
        "mu_next": jax.random.uniform(ks[5], (DEPTH, C_SHIFT), f32, 0.0, 0.5),
        "lam_q1": nrm(ks[6], (DEPTH, DH_A), 0.1),
        "lam_k1": nrm(ks[7], (DEPTH, DH_A), 0.1),
        "lam_q2": nrm(ks[8], (DEPTH, DH_A), 0.1),
        "lam_k2": nrm(ks[9], (DEPTH, DH_A), 0.1),
        "subln_g": 1.0 + nrm(ks[10], (DEPTH, 2 * DH_A), 0.05),
        "w0": jax.random.uniform(ks[11], (DEPTH, 2, W_B), f32, -4.0, 0.0),
        "w_lora": nrm(ks[12], (DEPTH, 2, R_LORA, W_B), 0.5 * R_LORA ** -0.5),
        "a0": nrm(ks[13], (DEPTH, 2, W_B), 0.5),
        "a_lora": nrm(ks[14], (DEPTH, 2, R_LORA, W_B), 0.5 * R_LORA ** -0.5),
        "k_k": 0.85 + nrm(ks[15], (DEPTH, W_B), 0.05),
        "k_a": 1.0 + nrm(ks[16], (DEPTH, W_B), 0.05),
        "r_k": nrm(ks[17], (DEPTH, H_B, N_B), 0.1),
        "lnx_g": 1.0 + nrm(ks[18], (DEPTH, W_B), 0.05),
        "lnx_b": nrm(ks[19], (DEPTH, W_B), 0.02),
        "w_oA": nrm(ks[20], (DEPTH, W_A, D_MODEL), W_A ** -0.5),
        "w_oB": nrm(ks[21], (DEPTH, W_B, D_MODEL), W_B ** -0.5),
        "w_out": nrm(ks[22], (DEPTH, D_MODEL, D_MODEL), D_MODEL ** -0.5),
        "final_g": 1.0 + nrm(ks[23], (D_MODEL,), 0.05),
    }


def reference(x_prompt, x_sample, norm_g, w_in, mu_prev, mu_next, lam_q1, lam_k1, lam_q2, lam_k2,
              subln_g, w0, w_lora, a0, a_lora, k_k, k_a, r_k, lnx_g, lnx_b, w_oA, w_oB, w_out,
              final_g):
    def trunk(x):
        for l in range(DEPTH):
            x = mixer_layer(x, lambda_init(l), norm_g[l], w_in[l], mu_prev[l], mu_next[l],
                            lam_q1[l], lam_k1[l], lam_q2[l], lam_k2[l], subln_g[l],
                            w0[l], w_lora[l], a0[l], a_lora[l], k_k[l], k_a[l], r_k[l],
                            lnx_g[l], lnx_b[l], w_oA[l], w_oB[l], w_out[l])
        return rms_norm(x, final_g)

    y_prompt = trunk(x_prompt)
    y_sample = trunk(x_sample)
    return (y_prompt, y_sample)
```

```python
import functools
import math

import jax
import jax.numpy as jnp
import numpy as np
from jax import lax
from jax.experimental import pallas as pl
from jax.experimental.pallas import tpu as pltpu

F32 = jnp.float32
BF16 = jnp.bfloat16

LANES = 128
DH_A = 64
N_B = 64
CHUNK = 64
ROPE_THETA = 10000.0
ATTN_SCALE = DH_A ** -0.5
NORM_EPS = 1e-6
SUBLN_EPS = 1e-5
LNX_EPS = 64e-5
MIB = 2 ** 20


def _tile(n, target, align):
    if n <= target:
        return n
    t = (target // align) * align
    while t >= align:
        if n % t == 0:
            return t
        t -= align
    raise ValueError(f"no tile for {n} (target {target}, align {align})")


def _params(semantics, vmem_mib):
    return pltpu.CompilerParams(dimension_semantics=semantics, vmem_limit_bytes=vmem_mib * MIB)


def _sigmoid(x):
    return 1.0 / (1.0 + jnp.exp(-x))


def _dot(a, b):
    return jnp.dot(a, b, preferred_element_type=F32)


def _dot_nt(a, b):
    return lax.dot_general(a, b, (((1,), (1,)), ((), ())), preferred_element_type=F32)


def _dot_tn(a, b):
    return lax.dot_general(a, b, (((0,), (0,)), ((), ())), preferred_element_type=F32)


def _rmsnorm_kernel(x_ref, g_ref, o_ref):
    x = x_ref[...]
    ms = jnp.mean(x * x, axis=-1, keepdims=True)
    o_ref[...] = (x * lax.rsqrt(ms + NORM_EPS) * g_ref[...]).astype(o_ref.dtype)


def _rmsnorm_bf16(x2, g):
    t, d = x2.shape
    tm = _tile(t, 256, 16)
    return pl.pallas_call(
        _rmsnorm_kernel,
        grid=(t // tm,),
        in_specs=[pl.BlockSpec((tm, d), lambda i: (i, 0)), pl.BlockSpec((1, d), lambda i: (0, 0))],
        out_specs=pl.BlockSpec((tm, d), lambda i: (i, 0)),
        out_shape=jax.ShapeDtypeStruct((t, d), BF16),
        compiler_params=_params(("parallel",), 32),
        name="rmsnorm_cast",
    )(x2, g.reshape(1, d))


def _matmul_kernel(a_ref, w_ref, o_ref):
    o_ref[...] = _dot(a_ref[...], w_ref[...]).astype(o_ref.dtype)


def _matmul_bf16(a, w):
    m, k = a.shape
    n = w.shape[1]
    tm = _tile(m, 1024, 16)
    tn = _tile(n, 1024, LANES)
    return pl.pallas_call(
        _matmul_kernel,
        grid=(m // tm, n // tn),
        in_specs=[pl.BlockSpec((tm, k), lambda i, j: (i, 0)), pl.BlockSpec((k, tn), lambda i, j: (0, j))],
        out_specs=pl.BlockSpec((tm, tn), lambda i, j: (i, j)),
        out_shape=jax.ShapeDtypeStruct((m, n), BF16),
        compiler_params=_params(("parallel", "arbitrary"), 56),
        name="in_proj",
    )(a, w)


def _rope_kernel(q_ref, k_ref, v_ref, cos_ref, sin_ref, qo_ref, ko_ref, vo_ref):
    cos = cos_ref[...]
    sin = sin_ref[...]
    lane = lax.broadcasted_iota(jnp.int32, cos.shape, 1)
    first_half = (lane % DH_A) < (DH_A // 2)

    def rope(x):
        partner = jnp.where(first_half,
                            pltpu.roll(x, LANES - DH_A // 2, axis=1),
                            pltpu.roll(x, DH_A // 2, axis=1))
        return x * cos + partner * sin

    qo_ref[0, 0] = (rope(q_ref[0].astype(F32)) * ATTN_SCALE).astype(qo_ref.dtype)
    ko_ref[0, 0] = rope(k_ref[0].astype(F32)).astype(ko_ref.dtype)
    vo_ref[0, 0] = v_ref[0]


def _rope_tables(s):
    half = DH_A // 2
    inv = 1.0 / (ROPE_THETA ** (jnp.arange(0, DH_A, 2, dtype=F32) / DH_A))
    ang = jnp.arange(s, dtype=F32)[:, None] * inv[None, :]
    cos, sin = jnp.cos(ang), jnp.sin(ang)
    reps = LANES // half
    return jnp.tile(cos, (1, reps)), jnp.tile(jnp.concatenate([-sin, sin], axis=-1), (1, reps // 2))


def _attn_prepass(p3, w_a):
    b, s, _ = p3.shape
    h = w_a // LANES
    ts = _tile(s, 512, 16)
    cos, sin = _rope_tables(s)
    head_out = jax.ShapeDtypeStruct((b, h, s, LANES), BF16)
    col = lambda off: pl.BlockSpec((1, ts, LANES), lambda bi, si, hi: (bi, si, off + hi))
    tab = pl.BlockSpec((ts, LANES), lambda bi, si, hi: (si, 0))
    out = pl.BlockSpec((1, 1, ts, LANES), lambda bi, si, hi: (bi, hi, si, 0))
    return pl.pallas_call(
        _rope_kernel,
        grid=(b, s // ts, h),
        in_specs=[col(0), col(h), col(2 * h), tab, tab],
        out_specs=[out, out, out],
        out_shape=[head_out, head_out, head_out],
        compiler_params=_params(("parallel", "parallel", "arbitrary"), 32),
        name="attn_prepass",
    )(p3, p3, p3, cos, sin)


def _attn_kernel(q_ref, k_ref, v_ref, z_ref, lq1_ref, lk1_ref, lq2_ref, lk2_ref, g_ref, o_ref,
                 qq_scr, m_scr, l_scr, acc_scr, *, tk, lam_init):
    tq = q_ref.shape[2]
    nk = k_ref.shape[2] // tk
    q = q_ref[0, 0].astype(F32)
    lane = lax.broadcasted_iota(jnp.int32, q.shape, 1)
    qq_scr[0:tq, :] = jnp.where(lane < DH_A, q, 0.0).astype(BF16)
    qq_scr[tq:2 * tq, :] = jnp.where(lane >= DH_A, q, 0.0).astype(BF16)
    m_scr[...] = jnp.full(m_scr.shape, -jnp.inf, F32)
    l_scr[...] = jnp.zeros(l_scr.shape, F32)
    acc_scr[...] = jnp.zeros(acc_scr.shape, F32)

    def body(i, carry):
        off = pl.multiple_of(i * tk, tk)
        ks = k_ref[0, 0, pl.ds(off, tk), :]
        vs = v_ref[0, 0, pl.ds(off, tk), :]
        s = _dot_nt(qq_scr[...], ks)
        m_old = m_scr[...]
        m_new = jnp.maximum(m_old, jnp.max(s, axis=-1, keepdims=True))
        alpha = jnp.exp(m_old - m_new)
        p = jnp.exp(s - m_new[:, 0:1])
        l_scr[...] = alpha * l_scr[...] + jnp.sum(p, axis=-1, keepdims=True)
        acc_scr[...] = alpha * acc_scr[...] + _dot(p.astype(BF16), vs)
        m_scr[...] = m_new
        return carry

    lax.fori_loop(0, nk, body, 0)

    lam = (jnp.exp(jnp.sum(lq1_ref[...] * lk1_ref[...], keepdims=True))
           - jnp.exp(jnp.sum(lq2_ref[...] * lk2_ref[...], keepdims=True)) + lam_init)
    o1 = acc_scr[0:tq, :] / l_scr[0:tq, :]
    o2 = acc_scr[tq:2 * tq, :] / l_scr[tq:2 * tq, :]
    o = o1 - lam * o2
    y = o * lax.rsqrt(jnp.mean(o * o, axis=-1, keepdims=True) + SUBLN_EPS) * g_ref[...] * (1.0 - lam_init)
    z = z_ref[0].astype(F32)
    o_ref[0] = (y * (z * _sigmoid(z))).astype(o_ref.dtype)


def _diff_attention(qr, kr, vc, p3, z_off, lam_q1, lam_k1, lam_q2, lam_k2, subln_g, lam_init):
    b, h, s, _ = qr.shape
    tq = _tile(s, 512, 16)
    tk = _tile(s, 512, LANES)
    assert z_off % LANES == 0
    zc = z_off // LANES
    vec = lambda n: pl.BlockSpec((1, n), lambda bi, hi, qi: (0, 0))
    kern = functools.partial(_attn_kernel, tk=tk, lam_init=lam_init)
    return pl.pallas_call(
        kern,
        grid=(b, h, s // tq),
        in_specs=[
            pl.BlockSpec((1, 1, tq, LANES), lambda bi, hi, qi: (bi, hi, qi, 0)),
            pl.BlockSpec((1, 1, s, LANES), lambda bi, hi, qi: (bi, hi, 0, 0)),
            pl.BlockSpec((1, 1, s, LANES), lambda bi, hi, qi: (bi, hi, 0, 0)),
            pl.BlockSpec((1, tq, LANES), lambda bi, hi, qi: (bi, qi, zc + hi)),
            vec(DH_A), vec(DH_A), vec(DH_A), vec(DH_A), vec(LANES),
        ],
        out_specs=pl.BlockSpec((1, tq, LANES), lambda bi, hi, qi: (bi, qi, hi)),
        out_shape=jax.ShapeDtypeStruct((b, s, h * LANES), BF16),
        scratch_shapes=[
            pltpu.VMEM((2 * tq, LANES), BF16),
            pltpu.VMEM((2 * tq, LANES), F32),
            pltpu.VMEM((2 * tq, LANES), F32),
            pltpu.VMEM((2 * tq, LANES), F32),
        ],
        compiler_params=_params(("parallel", "parallel", "arbitrary"), 48),
        name="diff_attn",
    )(qr, kr, vc, p3, lam_q1.reshape(1, DH_A), lam_k1.reshape(1, DH_A), lam_q2.reshape(1, DH_A),
      lam_k2.reshape(1, DH_A), subln_g.reshape(1, LANES))


def _group_sum(x, gmat):
    hi = x.astype(BF16)
    lo = (x - hi.astype(F32)).astype(BF16)
    return _dot(hi, gmat) + _dot(lo, gmat)


def _group_matrix(cw):
    r = lax.broadcasted_iota(jnp.int32, (cw, cw), 0) // N_B
    c = lax.broadcasted_iota(jnp.int32, (cw, cw), 1) // N_B
    return jnp.where(r == c, 1.0, 0.0).astype(BF16)


def _rwkv_prep_kernel(r_ref, k_ref, v_ref, lo_ref, rp_ref, kp_ref, vp_ref, lop_ref, rn_ref, kn_ref, vn_ref,
                      lon_ref, mu_ref, mulo_ref, w0_ref, a0_ref, wl_ref, al_ref, kk_ref, ka_ref, rk_ref,
                      ro_ref, vo_ref, ao_ref, lw_ref, kd_ref, bd_ref, bonus_ref):
    ts = r_ref.shape[1]
    si = pl.program_id(1)
    first = si == 0
    last = si == pl.num_programs(1) - 1

    def shift(cur_ref, prev_ref, next_ref, mu_p, mu_n):
        x = cur_ref[0].astype(F32)
        hp = prev_ref[0].astype(F32)
        hn = next_ref[0].astype(F32)
        pr = jnp.where(first, 0.0, hp[hp.shape[0] - 1:hp.shape[0], :])
        nx = jnp.where(last, 0.0, hn[0:1, :])
        row = lax.broadcasted_iota(jnp.int32, x.shape, 0)
        prev = jnp.where(row == 0, pr, pltpu.roll(x, 1, axis=0))
        nxt = jnp.where(row == ts - 1, nx, pltpu.roll(x, ts - 1, axis=0))
        return x + mu_p * (prev - x) + mu_n * (nxt - x)

    r = shift(r_ref, rp_ref, rn_ref, mu_ref[0, 0:1, :], mu_ref[1, 0:1, :])
    k = shift(k_ref, kp_ref, kn_ref, mu_ref[0, 1:2, :], mu_ref[1, 1:2, :])
    v = shift(v_ref, vp_ref, vn_ref, mu_ref[0, 2:3, :], mu_ref[1, 2:3, :])
    lo = shift(lo_ref, lop_ref, lon_ref, mulo_ref[0:1, :], mulo_ref[1:2, :])

    cw = r.shape[1]
    rl = lo.shape[1] // 4
    gmat = _group_matrix(cw)
    kkh = k * kk_ref[...]
    nrm = jnp.sqrt(_group_sum(kkh * kkh, gmat))
    kk = kkh / jnp.maximum(nrm, 1e-12)
    ka = ka_ref[...]
    ro_ref[0] = r.astype(ro_ref.dtype)
    vo_ref[0] = v.astype(vo_ref.dtype)
    ao_ref[0] = (-kk).astype(ao_ref.dtype)
    ksum = jnp.zeros_like(k)
    for d in range(2):
        pw = jnp.tanh(lo[:, d * rl:(d + 1) * rl]).astype(BF16)
        pa = lo[:, (2 + d) * rl:(3 + d) * rl].astype(BF16)
        wl = w0_ref[d:d + 1, :] + _dot(pw, wl_ref[d])
        lw_ref[d, 0] = -math.exp(-0.5) * _sigmoid(wl)
        a = _sigmoid(a0_ref[d:d + 1, :] + _dot(pa, al_ref[d]))
        kdir = k * (1.0 + (a - 1.0) * ka)
        kd_ref[d, 0] = kdir.astype(kd_ref.dtype)
        bd_ref[d, 0] = (kk * a).astype(bd_ref.dtype)
        ksum = ksum + kdir
    bonus_ref[0] = _group_sum(r * ksum * rk_ref[...], gmat) * v


def _rwkv_prep(p3, s_off, w_b, r_lora, mu_prev, mu_next, w0, w_lora, a0, a_lora, k_k, k_a, r_k):
    b, s, _ = p3.shape
    ts = _tile(s, 256, 16)
    cw = _tile(w_b, 256, LANES)
    lw = 4 * r_lora
    halo = 16
    nblk = s // halo
    assert s_off % cw == 0 and w_b % cw == 0 and (s_off + 3 * w_b) % lw == 0 and ts % halo == 0
    rc, kc, vc, lc = s_off // cw, (s_off + w_b) // cw, (s_off + 2 * w_b) // cw, (s_off + 3 * w_b) // lw
    per = ts // halo

    cur = lambda off: pl.BlockSpec((1, ts, cw), lambda bi, si, ji: (bi, si, off + ji))
    prv = lambda off: pl.BlockSpec((1, halo, cw), lambda bi, si, ji: (bi, jnp.maximum(si * per - 1, 0), off + ji))
    nxt = lambda off: pl.BlockSpec((1, halo, cw),
                                   lambda bi, si, ji: (bi, jnp.minimum((si + 1) * per, nblk - 1), off + ji))
    lo_cur = pl.BlockSpec((1, ts, lw), lambda bi, si, ji: (bi, si, lc))
    lo_prv = pl.BlockSpec((1, halo, lw), lambda bi, si, ji: (bi, jnp.maximum(si * per - 1, 0), lc))
    lo_nxt = pl.BlockSpec((1, halo, lw), lambda bi, si, ji: (bi, jnp.minimum((si + 1) * per, nblk - 1), lc))
    colvec = lambda rows: pl.BlockSpec((rows, cw), lambda bi, si, ji: (0, ji))

    mu_rkv = jnp.stack([mu_prev[:3 * w_b].reshape(3, w_b), mu_next[:3 * w_b].reshape(3, w_b)])
    mu_lo = jnp.stack([mu_prev[3 * w_b:], mu_next[3 * w_b:]])
    tok = pl.BlockSpec((1, ts, cw), lambda bi, si, ji: (bi, si, ji))
    tok2 = pl.BlockSpec((2, 1, ts, cw), lambda bi, si, ji: (0, bi, si, ji))
    shp = lambda dt: jax.ShapeDtypeStruct((b, s, w_b), dt)
    shp2 = lambda dt: jax.ShapeDtypeStruct((2, b, s, w_b), dt)
    return pl.pallas_call(
        _rwkv_prep_kernel,
        grid=(b, s // ts, w_b // cw),
        in_specs=[cur(rc), cur(kc), cur(vc), lo_cur, prv(rc), prv(kc), prv(vc), lo_prv,
                  nxt(rc), nxt(kc), nxt(vc), lo_nxt,
                  pl.BlockSpec((2, 3, cw), lambda bi, si, ji: (0, 0, ji)),
                  pl.BlockSpec((2, lw), lambda bi, si, ji: (0, 0)),
                  colvec(2), colvec(2),
                  pl.BlockSpec((2, r_lora, cw), lambda bi, si, ji: (0, 0, ji)),
                  pl.BlockSpec((2, r_lora, cw), lambda bi, si, ji: (0, 0, ji)),
                  colvec(1), colvec(1), colvec(1)],
        out_specs=[tok, tok, tok, tok2, tok2, tok2, tok],
        out_shape=[shp(BF16), shp(BF16), shp(BF16), shp2(F32), shp2(BF16), shp2(BF16), shp(F32)],
        compiler_params=_params(("parallel", "parallel", "arbitrary"), 48),
        name="rwkv_prep",
    )(p3, p3, p3, p3, p3, p3, p3, p3, p3, p3, p3, p3, mu_rkv, mu_lo, w0, a0,
      w_lora.astype(BF16), a_lora.astype(BF16), k_k.reshape(1, w_b), k_a.reshape(1, w_b), r_k.reshape(1, w_b))


def _wkv_kernel(mask_ref, r_ref, a_ref, v_ref, lw_ref, k_ref, b_ref, y_ref, z_scr, *, npairs):
    c = CHUNK
    ci = pl.program_id(3)

    @pl.when(ci == 0)
    def _():
        z_scr[...] = jnp.zeros(z_scr.shape, F32)

    strict = mask_ref[0, 0]
    incl = mask_ref[0, 1]
    cum = incl[0:c, 0:c].astype(BF16)
    lane = lax.broadcasted_iota(jnp.int32, (c, LANES), 1)
    low = lane < N_B
    row2 = lax.broadcasted_iota(jnp.int32, (LANES, LANES), 0)
    col2 = lax.broadcasted_iota(jnp.int32, (LANES, LANES), 1)
    eye = row2 == col2

    def stack(x):
        return jnp.concatenate([jnp.where(low, x, 0.0), jnp.where(low, 0.0, x)], axis=0)

    for j in range(npairs):
        sl = slice(j * LANES, (j + 1) * LANES)
        lw = lw_ref[0, 0, :, sl]
        hi = lw.astype(BF16)
        lo = (lw - hi.astype(F32)).astype(BF16)
        lam = _dot(cum, hi) + _dot(cum, lo)
        tot = jnp.sum(lw, axis=0, keepdims=True)
        e_in = jnp.exp(lam)
        e_ex = jnp.exp(lam - lw)
        e_neg = jnp.exp(-lam)
        e_rem = jnp.exp(tot - lam)
        gdec = jnp.exp(tot)
        r = r_ref[0, :, sl].astype(F32)
        a = a_ref[0, :, sl].astype(F32)
        v = v_ref[0, :, sl].astype(F32)
        k = k_ref[0, 0, :, sl].astype(F32)
        b = b_ref[0, 0, :, sl].astype(F32)
        a_s = stack(a * e_ex)
        r_s = stack(r * e_in)
        b_s = stack(b * e_neg).astype(BF16)
        k_s = stack(k * e_neg).astype(BF16)
        bh_s = stack(b * e_rem).astype(BF16)
        kh_s = stack(k * e_rem).astype(BF16)
        v_s = stack(v).astype(BF16)

        ar = jnp.concatenate([a_s, r_s], axis=0).astype(BF16)
        bk = jnp.concatenate([b_s, k_s], axis=0)
        m4 = _dot_nt(ar, bk)
        lab = m4[0:LANES, 0:LANES] * strict
        lak = m4[0:LANES, LANES:] * strict
        mrb = m4[LANES:, 0:LANES] * incl
        mrk = m4[LANES:, LANES:] * incl

        x = jnp.concatenate([a_s, _dot(lak.astype(BF16), v_s)], axis=1)
        lp = lab.astype(BF16)
        steps = int(math.log2(c))
        for it in range(steps):
            x = x + _dot(lp, x.astype(BF16))
            if it < steps - 1:
                lp = _dot(lp, lp).astype(BF16)
        w_s = x[:, 0:LANES].astype(BF16)
        uv = jnp.concatenate([x[:, LANES:].astype(BF16), v_s], axis=0)

        rw = r_s + _dot(mrb.astype(BF16), w_s)
        yv = _dot(jnp.concatenate([mrb, mrk], axis=1).astype(BF16), uv)
        pt = jnp.where(eye, gdec, 0.0) + _dot_tn(bh_s, w_s)
        qt = _dot_tn(jnp.concatenate([bh_s, kh_s], axis=0), uv)

        z = z_scr[j]
        zb = z.astype(BF16)
        ys = _dot(rw.astype(BF16), zb) + yv
        y_ref[0, 0, :, sl] = (ys[0:c, :] + ys[c:, :]).astype(y_ref.dtype)
        z_scr[j] = _dot(pt.astype(BF16), zb) + qt


def _wkv_masks():
    i = np.arange(LANES)
    same = (i[:, None] // CHUNK) == (i[None, :] // CHUNK)
    t, s = i[:, None] % CHUNK, i[None, :] % CHUNK
    fwd = np.stack([same & (s < t), same & (s <= t)])
    bwd = np.stack([same & (s > t), same & (s >= t)])
    return jnp.asarray(np.stack([fwd, bwd]).astype(np.float32))


def _wkv_scan(r, a, v, logw, kdir, bdir):
    b, s, w_b = r.shape
    c = CHUNK
    nc = s // c
    gw = _tile(w_b, 512, LANES)
    npairs = gw // LANES
    chunk_of = lambda d, ci: ci + d * (nc - 1 - 2 * ci)
    shared = pl.BlockSpec((1, c, gw), lambda d, bi, gi, ci: (bi, chunk_of(d, ci), gi))
    perdir = pl.BlockSpec((1, 1, c, gw), lambda d, bi, gi, ci: (d, bi, chunk_of(d, ci), gi))
    kern = functools.partial(_wkv_kernel, npairs=npairs)
    return pl.pallas_call(
        kern,
        grid=(2, b, w_b // gw, nc),
        in_specs=[pl.BlockSpec((1, 2, LANES, LANES), lambda d, bi, gi, ci: (d, 0, 0, 0)),
                  shared, shared, shared, perdir, perdir, perdir],
        out_specs=perdir,
        out_shape=jax.ShapeDtypeStruct((2, b, s, w_b), BF16),
        scratch_shapes=[pltpu.VMEM((npairs, LANES, LANES), F32)],
        compiler_params=_params(("parallel", "parallel", "parallel", "arbitrary"), 32),
        name="wkv_scan",
    )(_wkv_masks(), r, a, v, logw, kdir, bdir)


def _rwkv_out_kernel(y_ref, bonus_ref, z_ref, g_ref, b_ref, o_ref):
    y = y_ref[0, 0].astype(F32) + y_ref[1, 0].astype(F32)
    gmat = _group_matrix(y.shape[1])
    mu = _group_sum(y, gmat) * (1.0 / N_B)
    yc = y - mu
    var = _group_sum(yc * yc, gmat) * (1.0 / N_B)
    yn = yc * lax.rsqrt(var + LNX_EPS) * g_ref[...] + b_ref[...]
    z = z_ref[0].astype(F32)
    o_ref[0] = ((yn + bonus_ref[0]) * (z * _sigmoid(z))).astype(o_ref.dtype)


def _rwkv_out(y2, bonus, p3, z_off, lnx_g, lnx_b):
    _, b, s, w_b = y2.shape
    ts = _tile(s, 512, 16)
    cw = _tile(w_b, 256, LANES)
    assert z_off % cw == 0
    zc = z_off // cw
    vec = pl.BlockSpec((1, cw), lambda bi, si, ji: (0, ji))
    return pl.pallas_call(
        _rwkv_out_kernel,
        grid=(b, s // ts, w_b // cw),
        in_specs=[pl.BlockSpec((2, 1, ts, cw), lambda bi, si, ji: (0, bi, si, ji)),
                  pl.BlockSpec((1, ts, cw), lambda bi, si, ji: (bi, si, ji)),
                  pl.BlockSpec((1, ts, cw), lambda bi, si, ji: (bi, si, zc + ji)),
                  vec, vec],
        out_specs=pl.BlockSpec((1, ts, cw), lambda bi, si, ji: (bi, si, ji)),
        out_shape=jax.ShapeDtypeStruct((b, s, w_b), BF16),
        compiler_params=_params(("parallel", "parallel", "parallel"), 32),
        name="rwkv_out",
    )(y2, bonus, p3, lnx_g.reshape(1, w_b), lnx_b.reshape(1, w_b))


def _merge_kernel(ua_ref, ub_ref, wa_ref, wb_ref, ga_ref, gb_ref, o_ref):
    oa = _dot(ua_ref[...], wa_ref[...])
    ob = _dot(ub_ref[...], wb_ref[...])
    m = _sigmoid(ga_ref[...].astype(F32)) * oa + _sigmoid(gb_ref[...].astype(F32)) * ob
    o_ref[...] = m.astype(o_ref.dtype)


def _merge(ua, ub, w_oa, w_ob, p2, g_off):
    t, w_a = ua.shape
    w_b = ub.shape[1]
    d = w_oa.shape[1]
    tm = _tile(t, 512, 16)
    tn = _tile(d, 512, LANES)
    assert g_off % tn == 0 and d % tn == 0
    gc = g_off // tn
    nd = d // tn
    return pl.pallas_call(
        _merge_kernel,
        grid=(t // tm, nd),
        in_specs=[pl.BlockSpec((tm, w_a), lambda i, j: (i, 0)),
                  pl.BlockSpec((tm, w_b), lambda i, j: (i, 0)),
                  pl.BlockSpec((w_a, tn), lambda i, j: (0, j)),
                  pl.BlockSpec((w_b, tn), lambda i, j: (0, j)),
                  pl.BlockSpec((tm, tn), lambda i, j: (i, gc + j)),
                  pl.BlockSpec((tm, tn), lambda i, j: (i, gc + nd + j))],
        out_specs=pl.BlockSpec((tm, tn), lambda i, j: (i, j)),
        out_shape=jax.ShapeDtypeStruct((t, d), BF16),
        compiler_params=_params(("parallel", "arbitrary"), 48),
        name="gate_merge",
    )(ua, ub, w_oa, w_ob, p2, p2)


def _out_kernel(m_ref, w_ref, x_ref, g_ref, o_ref, *, final_norm):
    kk = pl.program_id(1)

    @pl.when(kk == 0)
    def _():
        o_ref[...] = x_ref[...]

    o_ref[...] += _dot(m_ref[...], w_ref[...])

    if final_norm:
        @pl.when(kk == pl.num_programs(1) - 1)
        def _():
            o = o_ref[...]
            o_ref[...] = o * lax.rsqrt(jnp.mean(o * o, axis=-1, keepdims=True) + NORM_EPS) * g_ref[...]


def _out_proj(m, w_out, x2, final_g, final_norm):
    t, d = x2.shape
    tm = _tile(t, 512, 16)
    tk = _tile(d, 512, LANES)
    kern = functools.partial(_out_kernel, final_norm=final_norm)
    return pl.pallas_call(
        kern,
        grid=(t // tm, d // tk),
        in_specs=[pl.BlockSpec((tm, tk), lambda i, k: (i, k)),
                  pl.BlockSpec((tk, d), lambda i, k: (k, 0)),
                  pl.BlockSpec((tm, d), lambda i, k: (i, 0)),
                  pl.BlockSpec((1, d), lambda i, k: (0, 0))],
        out_specs=pl.BlockSpec((tm, d), lambda i, k: (i, 0)),
        out_shape=jax.ShapeDtypeStruct((t, d), F32),
        compiler_params=_params(("parallel", "arbitrary"), 56),
        name="out_proj",
    )(m, w_out, x2, final_g.reshape(1, d))


def _lambda_init(layer_idx):
    return 0.8 - 0.6 * math.exp(-0.3 * layer_idx)


def _mixer_layer(x, l, prm, final_g, final_norm):
    b, s, d = x.shape
    w_a = prm["w_oA"][l].shape[0]
    w_b = prm["w_oB"][l].shape[0]
    r_lora = prm["w_lora"].shape[2]
    c_shift = 3 * w_b + 4 * r_lora
    n_in = prm["w_in"].shape[2]
    assert n_in == 3 * w_a + c_shift + w_a + w_b + 2 * d
    assert w_a % LANES == 0 and w_b % LANES == 0 and s % CHUNK == 0
    s_off = 3 * w_a
    za_off = s_off + c_shift
    zb_off = za_off + w_a
    g_off = zb_off + w_b
    lam_init = _lambda_init(l)

    x2 = x.reshape(b * s, d)
    h = _rmsnorm_bf16(x2, prm["norm_g"][l])
    p2 = _matmul_bf16(h, prm["w_in"][l].astype(BF16))
    p3 = p2.reshape(b, s, n_in)

    qr, kr, vc = _attn_prepass(p3, w_a)
    ua = _diff_attention(qr, kr, vc, p3, za_off, prm["lam_q1"][l], prm["lam_k1"][l], prm["lam_q2"][l],
                         prm["lam_k2"][l], prm["subln_g"][l], lam_init)

    r, v, a, logw, kdir, bdir, bonus = _rwkv_prep(
        p3, s_off, w_b, r_lora, prm["mu_prev"][l], prm["mu_next"][l], prm["w0"][l], prm["w_lora"][l],
        prm["a0"][l], prm["a_lora"][l], prm["k_k"][l], prm["k_a"][l], prm["r_k"][l])
    y2 = _wkv_scan(r, a, v, logw, kdir, bdir)
    ub = _rwkv_out(y2, bonus, p3, zb_off, prm["lnx_g"][l], prm["lnx_b"][l])

    m = _merge(ua.reshape(b * s, w_a), ub.reshape(b * s, w_b), prm["w_oA"][l].astype(BF16),
               prm["w_oB"][l].astype(BF16), p2, g_off)
    out = _out_proj(m, prm["w_out"][l].astype(BF16), x2, final_g, final_norm)
    return out.reshape(b, s, d)


def kernel(x_prompt, x_sample, norm_g, w_in, mu_prev, mu_next, lam_q1, lam_k1, lam_q2, lam_k2, subln_g, w0,
           w_lora, a0, a_lora, k_k, k_a, r_k, lnx_g, lnx_b, w_oA, w_oB, w_out, final_g):
    prm = dict(norm_g=norm_g, w_in=w_in, mu_prev=mu_prev, mu_next=mu_next, lam_q1=lam_q1, lam_k1=lam_k1,
               lam_q2=lam_q2, lam_k2=lam_k2, subln_g=subln_g, w0=w0, w_lora=w_lora, a0=a0, a_lora=a_lora,
               k_k=k_k, k_a=k_a, r_k=r_k, lnx_g=lnx_g, lnx_b=lnx_b, w_oA=w_oA, w_oB=w_oB, w_out=w_out)
    depth = norm_g.shape[0]

    def trunk(x):
        for l in range(depth):
            x = _mixer_layer(x, l, prm, final_g, final_norm=(l == depth - 1))
        return x

    return (trunk(x_prompt), trunk(x_sample))
```

```python
import functools
import math

import jax
import jax.numpy as jnp
import numpy as np
from jax import lax
from jax.experimental import pallas as pl
from jax.experimental.pallas import tpu as pltpu

F32 = jnp.float32
BF16 = jnp.bfloat16

LANES = 128
DH_A = 64
N_B = 64
CHUNK = 64
ROPE_THETA = 10000.0
ATTN_SCALE = DH_A ** -0.5
LOG2_E = math.log2(math.e)
ATTN_KEY_CHUNK = 512
WKV_GROUP_LANES = 1024
NORM_EPS = 1e-6
SUBLN_EPS = 1e-5
LNX_EPS = 64e-5
MIB = 2 ** 20


def _tile(n, target, align):
    if n <= target:
        return n
    t = (target // align) * align
    while t >= align:
        if n % t == 0:
            return t
        t -= align
    raise ValueError(f"no tile for {n} (target {target}, align {align})")


def _params(semantics, vmem_mib):
    return pltpu.CompilerParams(dimension_semantics=semantics, vmem_limit_bytes=vmem_mib * MIB)


def _sigmoid(x):
    return 1.0 / (1.0 + jnp.exp(-x))


def _dot(a, b):
    return jnp.dot(a, b, preferred_element_type=F32)


def _dot_nt(a, b):
    return lax.dot_general(a, b, (((1,), (1,)), ((), ())), preferred_element_type=F32)


def _dot_tn(a, b):
    return lax.dot_general(a, b, (((0,), (0,)), ((), ())), preferred_element_type=F32)


def _rmsnorm_kernel(x_ref, g_ref, o_ref):
    x = x_ref[...]
    ms = jnp.mean(x * x, axis=-1, keepdims=True)
    o_ref[...] = (x * lax.rsqrt(ms + NORM_EPS) * g_ref[...]).astype(o_ref.dtype)


def _rmsnorm_bf16(x2, g):
    t, d = x2.shape
    tm = _tile(t, 256, 16)
    return pl.pallas_call(
        _rmsnorm_kernel,
        grid=(t // tm,),
        in_specs=[pl.BlockSpec((tm, d), lambda i: (i, 0)), pl.BlockSpec((1, d), lambda i: (0, 0))],
        out_specs=pl.BlockSpec((tm, d), lambda i: (i, 0)),
        out_shape=jax.ShapeDtypeStruct((t, d), BF16),
        compiler_params=_params(("parallel",), 32),
        name="rmsnorm_cast",
    )(x2, g.reshape(1, d))


def _matmul_kernel(a_ref, w_ref, o_ref):
    o_ref[...] = _dot(a_ref[...], w_ref[...]).astype(o_ref.dtype)


def _matmul_bf16(a, w):
    m, k = a.shape
    n = w.shape[1]
    tm = _tile(m, 1024, 16)
    tn = _tile(n, 1024, LANES)
    return pl.pallas_call(
        _matmul_kernel,
        grid=(m // tm, n // tn),
        in_specs=[pl.BlockSpec((tm, k), lambda i, j: (i, 0)), pl.BlockSpec((k, tn), lambda i, j: (0, j))],
        out_specs=pl.BlockSpec((tm, tn), lambda i, j: (i, j)),
        out_shape=jax.ShapeDtypeStruct((m, n), BF16),
        compiler_params=_params(("parallel", "arbitrary"), 56),
        name="in_proj",
    )(a, w)


def _rope_kernel(q_ref, k_ref, v_ref, cos_ref, sin_ref, qo_ref, ko_ref, vo_ref):
    cos = cos_ref[...]
    sin = sin_ref[...]
    lane = lax.broadcasted_iota(jnp.int32, cos.shape, 1)
    first_half = (lane % DH_A) < (DH_A // 2)

    def rope(x):
        partner = jnp.where(first_half,
                            pltpu.roll(x, LANES - DH_A // 2, axis=1),
                            pltpu.roll(x, DH_A // 2, axis=1))
        return x * cos + partner * sin

    qo_ref[0, 0] = (rope(q_ref[0].astype(F32)) * (ATTN_SCALE * LOG2_E)).astype(qo_ref.dtype)
    ko_ref[0, 0] = rope(k_ref[0].astype(F32)).astype(ko_ref.dtype)
    vo_ref[0, 0, 0] = v_ref[0].astype(F32).T.astype(vo_ref.dtype)


def _rope_tables(s):
    half = DH_A // 2
    inv = 1.0 / (ROPE_THETA ** (jnp.arange(0, DH_A, 2, dtype=F32) / DH_A))
    ang = jnp.arange(s, dtype=F32)[:, None] * inv[None, :]
    cos, sin = jnp.cos(ang), jnp.sin(ang)
    reps = LANES // half
    return jnp.tile(cos, (1, reps)), jnp.tile(jnp.concatenate([-sin, sin], axis=-1), (1, reps // 2))


def _attn_prepass(p3, w_a):
    b, s, _ = p3.shape
    h = w_a // LANES
    ts = _tile(s, ATTN_KEY_CHUNK, LANES)
    cos, sin = _rope_tables(s)
    head_out = jax.ShapeDtypeStruct((b, h, s, LANES), BF16)
    vt_out = jax.ShapeDtypeStruct((b, h, s // ts, LANES, ts), BF16)
    col = lambda off: pl.BlockSpec((1, ts, LANES), lambda bi, si, hi: (bi, si, off + hi))
    tab = pl.BlockSpec((ts, LANES), lambda bi, si, hi: (si, 0))
    out = pl.BlockSpec((1, 1, ts, LANES), lambda bi, si, hi: (bi, hi, si, 0))
    out_t = pl.BlockSpec((1, 1, 1, LANES, ts), lambda bi, si, hi: (bi, hi, si, 0, 0))
    return pl.pallas_call(
        _rope_kernel,
        grid=(b, s // ts, h),
        in_specs=[col(0), col(h), col(2 * h), tab, tab],
        out_specs=[out, out, out_t],
        out_shape=[head_out, head_out, vt_out],
        compiler_params=_params(("parallel", "parallel", "arbitrary"), 32),
        name="attn_prepass",
    )(p3, p3, p3, cos, sin)


def _attn_kernel(q_ref, k_ref, vt_ref, z_ref, lq1_ref, lk1_ref, lq2_ref, lk2_ref, g_ref, o_ref,
                 qq_scr, acc_scr, *, qt, lam_init):
    tq = q_ref.shape[2]
    nk = vt_ref.shape[2]
    tk = vt_ref.shape[4]
    tiles = [slice(c * qt, (c + 1) * qt) for c in range(2 * tq // qt)]
    q = q_ref[0, 0].astype(F32)
    lane = lax.broadcasted_iota(jnp.int32, q.shape, 1)
    qq_scr[0:tq, :] = jnp.where(lane < DH_A, q, 0.0).astype(BF16)
    qq_scr[tq:2 * tq, :] = jnp.where(lane >= DH_A, q, 0.0).astype(BF16)
    acc_scr[...] = jnp.zeros(acc_scr.shape, F32)

    def body(i, carry):
        m_old, l_old = carry
        off = pl.multiple_of(i * tk, tk)
        ks = k_ref[0, 0, pl.ds(off, tk), :]
        vt = vt_ref[0, 0, i]
        qs = [qq_scr[t, :] for t in tiles]
        acc = [acc_scr[:, t] for t in tiles]
        s = [_dot_nt(ks, x) for x in qs]
        m_new = [jnp.maximum(m_old[:, t], jnp.max(x, axis=0, keepdims=True)) for t, x in zip(tiles, s)]
        alpha = [jnp.exp2(m_old[:, t] - mn) for t, mn in zip(tiles, m_new)]
        p = [jnp.exp2(x - mn) for x, mn in zip(s, m_new)]
        l_new = [al * l_old[:, t] + jnp.sum(x, axis=0, keepdims=True) for t, al, x in zip(tiles, alpha, p)]
        pv = [_dot(vt, x.astype(BF16)) for x in p]
        for t, al, a, x in zip(tiles, alpha, acc, pv):
            acc_scr[:, t] = al * a + x
        return jnp.concatenate(m_new, axis=1), jnp.concatenate(l_new, axis=1)

    m0 = jnp.full((1, 2 * tq), -jnp.inf, F32)
    l0 = jnp.zeros((1, 2 * tq), F32)
    _, l_fin = lax.fori_loop(0, nk, body, (m0, l0), unroll=2 if nk % 2 == 0 else 1)

    lam = (jnp.exp(jnp.sum(lq1_ref[...] * lk1_ref[...], keepdims=True))
           - jnp.exp(jnp.sum(lq2_ref[...] * lk2_ref[...], keepdims=True)) + lam_init)
    ot = acc_scr[...] / l_fin
    o = (ot[:, 0:tq] - lam * ot[:, tq:2 * tq]).T
    y = o * lax.rsqrt(jnp.mean(o * o, axis=-1, keepdims=True) + SUBLN_EPS) * g_ref[...] * (1.0 - lam_init)
    z = z_ref[0].astype(F32)
    o_ref[0] = (y * (z * _sigmoid(z))).astype(o_ref.dtype)


def _diff_attention(qr, kr, vt, p3, z_off, lam_q1, lam_k1, lam_q2, lam_k2, subln_g, lam_init):
    b, h, s, _ = qr.shape
    nk, tk = vt.shape[2], vt.shape[4]
    tq = _tile(s, 512, LANES)
    qt = _tile(2 * tq, 256, LANES)
    assert z_off % LANES == 0
    zc = z_off // LANES
    vec = lambda n: pl.BlockSpec((1, n), lambda bi, hi, qi: (0, 0))
    kern = functools.partial(_attn_kernel, qt=qt, lam_init=lam_init)
    return pl.pallas_call(
        kern,
        grid=(b, h, s // tq),
        in_specs=[
            pl.BlockSpec((1, 1, tq, LANES), lambda bi, hi, qi: (bi, hi, qi, 0)),
            pl.BlockSpec((1, 1, s, LANES), lambda bi, hi, qi: (bi, hi, 0, 0)),
            pl.BlockSpec((1, 1, nk, LANES, tk), lambda bi, hi, qi: (bi, hi, 0, 0, 0)),
            pl.BlockSpec((1, tq, LANES), lambda bi, hi, qi: (bi, qi, zc + hi)),
            vec(DH_A), vec(DH_A), vec(DH_A), vec(DH_A), vec(LANES),
        ],
        out_specs=pl.BlockSpec((1, tq, LANES), lambda bi, hi, qi: (bi, qi, hi)),
        out_shape=jax.ShapeDtypeStruct((b, s, h * LANES), BF16),
        scratch_shapes=[
            pltpu.VMEM((2 * tq, LANES), BF16),
            pltpu.VMEM((LANES, 2 * tq), F32),
        ],
        compiler_params=_params(("parallel", "parallel", "arbitrary"), 48),
        name="diff_attn",
    )(qr, kr, vt, p3, lam_q1.reshape(1, DH_A), lam_k1.reshape(1, DH_A), lam_q2.reshape(1, DH_A),
      lam_k2.reshape(1, DH_A), subln_g.reshape(1, LANES))


def _group_sum(x, gmat):
    hi = x.astype(BF16)
    lo = (x - hi.astype(F32)).astype(BF16)
    return _dot(hi, gmat) + _dot(lo, gmat)


def _group_matrix(cw):
    r = lax.broadcasted_iota(jnp.int32, (cw, cw), 0) // N_B
    c = lax.broadcasted_iota(jnp.int32, (cw, cw), 1) // N_B
    return jnp.where(r == c, 1.0, 0.0).astype(BF16)


def _rwkv_prep_kernel(r_ref, k_ref, v_ref, lo_ref, rp_ref, kp_ref, vp_ref, lop_ref, rn_ref, kn_ref, vn_ref,
                      lon_ref, mu_ref, mulo_ref, w0_ref, a0_ref, wl_ref, al_ref, kk_ref, ka_ref, rk_ref,
                      ro_ref, vo_ref, ao_ref, lw_ref, kd_ref, bd_ref, bonus_ref):
    ts = r_ref.shape[1]
    si = pl.program_id(1)
    first = si == 0
    last = si == pl.num_programs(1) - 1

    def shift(cur_ref, prev_ref, next_ref, mu_p, mu_n):
        x = cur_ref[0].astype(F32)
        hp = prev_ref[0].astype(F32)
        hn = next_ref[0].astype(F32)
        pr = jnp.where(first, 0.0, hp[hp.shape[0] - 1:hp.shape[0], :])
        nx = jnp.where(last, 0.0, hn[0:1, :])
        row = lax.broadcasted_iota(jnp.int32, x.shape, 0)
        prev = jnp.where(row == 0, pr, pltpu.roll(x, 1, axis=0))
        nxt = jnp.where(row == ts - 1, nx, pltpu.roll(x, ts - 1, axis=0))
        return x + mu_p * (prev - x) + mu_n * (nxt - x)

    r = shift(r_ref, rp_ref, rn_ref, mu_ref[0, 0:1, :], mu_ref[1, 0:1, :])
    k = shift(k_ref, kp_ref, kn_ref, mu_ref[0, 1:2, :], mu_ref[1, 1:2, :])
    v = shift(v_ref, vp_ref, vn_ref, mu_ref[0, 2:3, :], mu_ref[1, 2:3, :])
    lo = shift(lo_ref, lop_ref, lon_ref, mulo_ref[0:1, :], mulo_ref[1:2, :])

    cw = r.shape[1]
    rl = lo.shape[1] // 4
    gmat = _group_matrix(cw)
    kkh = k * kk_ref[...]
    nrm = jnp.sqrt(_group_sum(kkh * kkh, gmat))
    kk = kkh / jnp.maximum(nrm, 1e-12)
    ka = ka_ref[...]
    ro_ref[0] = r.astype(ro_ref.dtype)
    vo_ref[0] = v.astype(vo_ref.dtype)
    ao_ref[0] = (-kk).astype(ao_ref.dtype)
    ksum = jnp.zeros_like(k)
    for d in range(2):
        pw = jnp.tanh(lo[:, d * rl:(d + 1) * rl]).astype(BF16)
        pa = lo[:, (2 + d) * rl:(3 + d) * rl].astype(BF16)
        wl = w0_ref[d:d + 1, :] + _dot(pw, wl_ref[d])
        lw_ref[d, 0] = -math.exp(-0.5) * _sigmoid(wl)
        a = _sigmoid(a0_ref[d:d + 1, :] + _dot(pa, al_ref[d]))
        kdir = k * (1.0 + (a - 1.0) * ka)
        kd_ref[d, 0] = kdir.astype(kd_ref.dtype)
        bd_ref[d, 0] = (kk * a).astype(bd_ref.dtype)
        ksum = ksum + kdir
    bonus_ref[0] = _group_sum(r * ksum * rk_ref[...], gmat) * v


def _rwkv_prep(p3, s_off, w_b, r_lora, mu_prev, mu_next, w0, w_lora, a0, a_lora, k_k, k_a, r_k):
    b, s, _ = p3.shape
    ts = _tile(s, 256, 16)
    cw = _tile(w_b, 256, LANES)
    lw = 4 * r_lora
    halo = 16
    nblk = s // halo
    assert s_off % cw == 0 and w_b % cw == 0 and (s_off + 3 * w_b) % lw == 0 and ts % halo == 0
    rc, kc, vc, lc = s_off // cw, (s_off + w_b) // cw, (s_off + 2 * w_b) // cw, (s_off + 3 * w_b) // lw
    per = ts // halo

    cur = lambda off: pl.BlockSpec((1, ts, cw), lambda bi, si, ji: (bi, si, off + ji))
    prv = lambda off: pl.BlockSpec((1, halo, cw), lambda bi, si, ji: (bi, jnp.maximum(si * per - 1, 0), off + ji))
    nxt = lambda off: pl.BlockSpec((1, halo, cw),
                                   lambda bi, si, ji: (bi, jnp.minimum((si + 1) * per, nblk - 1), off + ji))
    lo_cur = pl.BlockSpec((1, ts, lw), lambda bi, si, ji: (bi, si, lc))
    lo_prv = pl.BlockSpec((1, halo, lw), lambda bi, si, ji: (bi, jnp.maximum(si * per - 1, 0), lc))
    lo_nxt = pl.BlockSpec((1, halo, lw), lambda bi, si, ji: (bi, jnp.minimum((si + 1) * per, nblk - 1), lc))
    colvec = lambda rows: pl.BlockSpec((rows, cw), lambda bi, si, ji: (0, ji))

    mu_rkv = jnp.stack([mu_prev[:3 * w_b].reshape(3, w_b), mu_next[:3 * w_b].reshape(3, w_b)])
    mu_lo = jnp.stack([mu_prev[3 * w_b:], mu_next[3 * w_b:]])
    tok = pl.BlockSpec((1, ts, cw), lambda bi, si, ji: (bi, si, ji))
    tok2 = pl.BlockSpec((2, 1, ts, cw), lambda bi, si, ji: (0, bi, si, ji))
    shp = lambda dt: jax.ShapeDtypeStruct((b, s, w_b), dt)
    shp2 = lambda dt: jax.ShapeDtypeStruct((2, b, s, w_b), dt)
    return pl.pallas_call(
        _rwkv_prep_kernel,
        grid=(b, s // ts, w_b // cw),
        in_specs=[cur(rc), cur(kc), cur(vc), lo_cur, prv(rc), prv(kc), prv(vc), lo_prv,
                  nxt(rc), nxt(kc), nxt(vc), lo_nxt,
                  pl.BlockSpec((2, 3, cw), lambda bi, si, ji: (0, 0, ji)),
                  pl.BlockSpec((2, lw), lambda bi, si, ji: (0, 0)),
                  colvec(2), colvec(2),
                  pl.BlockSpec((2, r_lora, cw), lambda bi, si, ji: (0, 0, ji)),
                  pl.BlockSpec((2, r_lora, cw), lambda bi, si, ji: (0, 0, ji)),
                  colvec(1), colvec(1), colvec(1)],
        out_specs=[tok, tok, tok, tok2, tok2, tok2, tok],
        out_shape=[shp(BF16), shp(BF16), shp(BF16), shp2(F32), shp2(BF16), shp2(BF16), shp(F32)],
        compiler_params=_params(("parallel", "parallel", "arbitrary"), 48),
        name="rwkv_prep",
    )(p3, p3, p3, p3, p3, p3, p3, p3, p3, p3, p3, p3, mu_rkv, mu_lo, w0, a0,
      w_lora.astype(BF16), a_lora.astype(BF16), k_k.reshape(1, w_b), k_a.reshape(1, w_b), r_k.reshape(1, w_b))


def _wkv_kernel(mask_ref, r_ref, a_ref, v_ref, lw_ref, k_ref, b_ref, y_ref, z_scr, *, npairs):
    c = CHUNK
    ci = pl.program_id(3)

    @pl.when(ci == 0)
    def _():
        z_scr[...] = jnp.zeros(z_scr.shape, F32)

    strict = mask_ref[0, 0]
    incl = mask_ref[0, 1]
    cum = incl[0:c, 0:c].astype(BF16)
    lane = lax.broadcasted_iota(jnp.int32, (c, LANES), 1)
    low = lane < N_B
    row2 = lax.broadcasted_iota(jnp.int32, (LANES, LANES), 0)
    col2 = lax.broadcasted_iota(jnp.int32, (LANES, LANES), 1)
    eye = row2 == col2

    def stack(x):
        return jnp.concatenate([jnp.where(low, x, 0.0), jnp.where(low, 0.0, x)], axis=0)

    def each(f, *cols):
        return [f(*xs) for xs in zip(*cols)]

    sls = [slice(j * LANES, (j + 1) * LANES) for j in range(npairs)]
    lw = [lw_ref[0, 0, :, sl] for sl in sls]
    r = [r_ref[0, :, sl].astype(F32) for sl in sls]
    a = [a_ref[0, :, sl].astype(F32) for sl in sls]
    v = [v_ref[0, :, sl].astype(F32) for sl in sls]
    k = [k_ref[0, 0, :, sl].astype(F32) for sl in sls]
    b = [b_ref[0, 0, :, sl].astype(F32) for sl in sls]
    zb = [z_scr[j].astype(BF16) for j in range(npairs)]

    hi = each(lambda x: x.astype(BF16), lw)
    lo = each(lambda x, h: (x - h.astype(F32)).astype(BF16), lw, hi)
    lam = each(lambda h, l: _dot(cum, h) + _dot(cum, l), hi, lo)
    tot = each(lambda x: jnp.sum(x, axis=0, keepdims=True), lw)
    gdec = each(jnp.exp, tot)
    a_s = each(lambda x, lm, w: stack(x * jnp.exp(lm - w)), a, lam, lw)
    r_s = each(lambda x, lm: stack(x * jnp.exp(lm)), r, lam)
    e_neg = each(lambda lm: jnp.exp(-lm), lam)
    e_rem = each(lambda t, lm: jnp.exp(t - lm), tot, lam)
    b_s = each(lambda x, e: stack(x * e).astype(BF16), b, e_neg)
    k_s = each(lambda x, e: stack(x * e).astype(BF16), k, e_neg)
    bh_s = each(lambda x, e: stack(x * e).astype(BF16), b, e_rem)
    kh_s = each(lambda x, e: stack(x * e).astype(BF16), k, e_rem)
    v_s = each(lambda x: stack(x).astype(BF16), v)

    m4 = each(lambda x, y, p, q: _dot_nt(jnp.concatenate([x, y], axis=0).astype(BF16),
                                         jnp.concatenate([p, q], axis=0)), a_s, r_s, b_s, k_s)
    lp = each(lambda m: (m[0:LANES, 0:LANES] * strict).astype(BF16), m4)
    lak = each(lambda m: (m[0:LANES, LANES:] * strict).astype(BF16), m4)
    mrbk = each(lambda m: jnp.concatenate([m[LANES:, 0:LANES] * incl, m[LANES:, LANES:] * incl],
                                          axis=1).astype(BF16), m4)

    x = each(lambda p, q, w: jnp.concatenate([p, _dot(q, w)], axis=1), a_s, lak, v_s)
    steps = int(math.log2(c))
    for it in range(steps):
        x = each(lambda p, q: q + _dot(p, q.astype(BF16)), lp, x)
        if it < steps - 1:
            lp = each(lambda p: _dot(p, p).astype(BF16), lp)
    w_s = each(lambda q: q[:, 0:LANES].astype(BF16), x)
    uv = each(lambda q, w: jnp.concatenate([q[:, LANES:].astype(BF16), w], axis=0), x, v_s)

    rw = each(lambda p, m, w: (p + _dot(m[:, 0:LANES], w)).astype(BF16), r_s, mrbk, w_s)
    yv = each(_dot, mrbk, uv)
    pt = each(lambda g, p, w: (jnp.where(eye, g, 0.0) + _dot_tn(p, w)).astype(BF16), gdec, bh_s, w_s)
    qt = each(lambda p, q, u: _dot_tn(jnp.concatenate([p, q], axis=0), u), bh_s, kh_s, uv)

    ys = each(lambda p, z, q: _dot(p, z) + q, rw, zb, yv)
    z_new = each(lambda p, z, q: _dot(p, z) + q, pt, zb, qt)
    for j, sl in enumerate(sls):
        y_ref[0, 0, :, sl] = (ys[j][0:c, :] + ys[j][c:, :]).astype(y_ref.dtype)
        z_scr[j] = z_new[j]


def _wkv_masks():
    i = np.arange(LANES)
    same = (i[:, None] // CHUNK) == (i[None, :] // CHUNK)
    t, s = i[:, None] % CHUNK, i[None, :] % CHUNK
    fwd = np.stack([same & (s < t), same & (s <= t)])
    bwd = np.stack([same & (s > t), same & (s >= t)])
    return jnp.asarray(np.stack([fwd, bwd]).astype(np.float32))


def _wkv_scan(r, a, v, logw, kdir, bdir):
    b, s, w_b = r.shape
    c = CHUNK
    nc = s // c
    gw = _tile(w_b, WKV_GROUP_LANES, LANES)
    npairs = gw // LANES
    chunk_of = lambda d, ci: ci + d * (nc - 1 - 2 * ci)
    shared = pl.BlockSpec((1, c, gw), lambda d, bi, gi, ci: (bi, chunk_of(d, ci), gi))
    perdir = pl.BlockSpec((1, 1, c, gw), lambda d, bi, gi, ci: (d, bi, chunk_of(d, ci), gi))
    kern = functools.partial(_wkv_kernel, npairs=npairs)
    return pl.pallas_call(
        kern,
        grid=(2, b, w_b // gw, nc),
        in_specs=[pl.BlockSpec((1, 2, LANES, LANES), lambda d, bi, gi, ci: (d, 0, 0, 0)),
                  shared, shared, shared, perdir, perdir, perdir],
        out_specs=perdir,
        out_shape=jax.ShapeDtypeStruct((2, b, s, w_b), BF16),
        scratch_shapes=[pltpu.VMEM((npairs, LANES, LANES), F32)],
        compiler_params=_params(("parallel", "parallel", "parallel", "arbitrary"), 32),
        name="wkv_scan",
    )(_wkv_masks(), r, a, v, logw, kdir, bdir)


def _rwkv_out_kernel(y_ref, bonus_ref, z_ref, g_ref, b_ref, o_ref):
    y = y_ref[0, 0].astype(F32) + y_ref[1, 0].astype(F32)
    gmat = _group_matrix(y.shape[1])
    mu = _group_sum(y, gmat) * (1.0 / N_B)
    yc = y - mu
    var = _group_sum(yc * yc, gmat) * (1.0 / N_B)
    yn = yc * lax.rsqrt(var + LNX_EPS) * g_ref[...] + b_ref[...]
    z = z_ref[0].astype(F32)
    o_ref[0] = ((yn + bonus_ref[0]) * (z * _sigmoid(z))).astype(o_ref.dtype)


def _rwkv_out(y2, bonus, p3, z_off, lnx_g, lnx_b):
    _, b, s, w_b = y2.shape
    ts = _tile(s, 512, 16)
    cw = _tile(w_b, 256, LANES)
    assert z_off % cw == 0
    zc = z_off // cw
    vec = pl.BlockSpec((1, cw), lambda bi, si, ji: (0, ji))
    return pl.pallas_call(
        _rwkv_out_kernel,
        grid=(b, s // ts, w_b // cw),
        in_specs=[pl.BlockSpec((2, 1, ts, cw), lambda bi, si, ji: (0, bi, si, ji)),
                  pl.BlockSpec((1, ts, cw), lambda bi, si, ji: (bi, si, ji)),
                  pl.BlockSpec((1, ts, cw), lambda bi, si, ji: (bi, si, zc + ji)),
                  vec, vec],
        out_specs=pl.BlockSpec((1, ts, cw), lambda bi, si, ji: (bi, si, ji)),
        out_shape=jax.ShapeDtypeStruct((b, s, w_b), BF16),
        compiler_params=_params(("parallel", "parallel", "parallel"), 32),
        name="rwkv_out",
    )(y2, bonus, p3, lnx_g.reshape(1, w_b), lnx_b.reshape(1, w_b))


def _merge_kernel(ua_ref, ub_ref, wa_ref, wb_ref, ga_ref, gb_ref, o_ref):
    oa = _dot(ua_ref[...], wa_ref[...])
    ob = _dot(ub_ref[...], wb_ref[...])
    m = _sigmoid(ga_ref[...].astype(F32)) * oa + _sigmoid(gb_ref[...].astype(F32)) * ob
    o_ref[...] = m.astype(o_ref.dtype)


def _merge(ua, ub, w_oa, w_ob, p2, g_off):
    t, w_a = ua.shape
    w_b = ub.shape[1]
    d = w_oa.shape[1]
    tm = _tile(t, 512, 16)
    tn = _tile(d, 512, LANES)
    assert g_off % tn == 0 and d % tn == 0
    gc = g_off // tn
    nd = d // tn
    return pl.pallas_call(
        _merge_kernel,
        grid=(t // tm, nd),
        in_specs=[pl.BlockSpec((tm, w_a), lambda i, j: (i, 0)),
                  pl.BlockSpec((tm, w_b), lambda i, j: (i, 0)),
                  pl.BlockSpec((w_a, tn), lambda i, j: (0, j)),
                  pl.BlockSpec((w_b, tn), lambda i, j: (0, j)),
                  pl.BlockSpec((tm, tn), lambda i, j: (i, gc + j)),
                  pl.BlockSpec((tm, tn), lambda i, j: (i, gc + nd + j))],
        out_specs=pl.BlockSpec((tm, tn), lambda i, j: (i, j)),
        out_shape=jax.ShapeDtypeStruct((t, d), BF16),
        compiler_params=_params(("parallel", "arbitrary"), 48),
        name="gate_merge",
    )(ua, ub, w_oa, w_ob, p2, p2)


def _out_kernel(m_ref, w_ref, x_ref, g_ref, o_ref, *, final_norm):
    kk = pl.program_id(1)

    @pl.when(kk == 0)
    def _():
        o_ref[...] = x_ref[...]

    o_ref[...] += _dot(m_ref[...], w_ref[...])

    if final_norm:
        @pl.when(kk == pl.num_programs(1) - 1)
        def _():
            o = o_ref[...]
            o_ref[...] = o * lax.rsqrt(jnp.mean(o * o, axis=-1, keepdims=True) + NORM_EPS) * g_ref[...]


def _out_proj(m, w_out, x2, final_g, final_norm):
    t, d = x2.shape
    tm = _tile(t, 512, 16)
    tk = _tile(d, 512, LANES)
    kern = functools.partial(_out_kernel, final_norm=final_norm)
    return pl.pallas_call(
        kern,
        grid=(t // tm, d // tk),
        in_specs=[pl.BlockSpec((tm, tk), lambda i, k: (i, k)),
                  pl.BlockSpec((tk, d), lambda i, k: (k, 0)),
                  pl.BlockSpec((tm, d), lambda i, k: (i, 0)),
                  pl.BlockSpec((1, d), lambda i, k: (0, 0))],
        out_specs=pl.BlockSpec((tm, d), lambda i, k: (i, 0)),
        out_shape=jax.ShapeDtypeStruct((t, d), F32),
        compiler_params=_params(("parallel", "arbitrary"), 56),
        name="out_proj",
    )(m, w_out, x2, final_g.reshape(1, d))


def _lambda_init(layer_idx):
    return 0.8 - 0.6 * math.exp(-0.3 * layer_idx)


def _mixer_layer(x, l, prm, final_g, final_norm):
    b, s, d = x.shape
    w_a = prm["w_oA"][l].shape[0]
    w_b = prm["w_oB"][l].shape[0]
    r_lora = prm["w_lora"].shape[2]
    c_shift = 3 * w_b + 4 * r_lora
    n_in = prm["w_in"].shape[2]
    assert n_in == 3 * w_a + c_shift + w_a + w_b + 2 * d
    assert w_a % LANES == 0 and w_b % LANES == 0 and s % CHUNK == 0
    s_off = 3 * w_a
    za_off = s_off + c_shift
    zb_off = za_off + w_a
    g_off = zb_off + w_b
    lam_init = _lambda_init(l)

    x2 = x.reshape(b * s, d)
    h = _rmsnorm_bf16(x2, prm["norm_g"][l])
    p2 = _matmul_bf16(h, prm["w_in"][l].astype(BF16))
    p3 = p2.reshape(b, s, n_in)

    qr, kr, vt = _attn_prepass(p3, w_a)
    ua = _diff_attention(qr, kr, vt, p3, za_off, prm["lam_q1"][l], prm["lam_k1"][l], prm["lam_q2"][l],
                         prm["lam_k2"][l], prm["subln_g"][l], lam_init)

    r, v, a, logw, kdir, bdir, bonus = _rwkv_prep(
        p3, s_off, w_b, r_lora, prm["mu_prev"][l], prm["mu_next"][l], prm["w0"][l], prm["w_lora"][l],
        prm["a0"][l], prm["a_lora"][l], prm["k_k"][l], prm["k_a"][l], prm["r_k"][l])
    y2 = _wkv_scan(r, a, v, logw, kdir, bdir)
    ub = _rwkv_out(y2, bonus, p3, zb_off, prm["lnx_g"][l], prm["lnx_b"][l])

    m = _merge(ua.reshape(b * s, w_a), ub.reshape(b * s, w_b), prm["w_oA"][l].astype(BF16),
               prm["w_oB"][l].astype(BF16), p2, g_off)
    out = _out_proj(m, prm["w_out"][l].astype(BF16), x2, final_g, final_norm)
    return out.reshape(b, s, d)


def kernel(x_prompt, x_sample, norm_g, w_in, mu_prev, mu_next, lam_q1, lam_k1, lam_q2, lam_k2, subln_g, w0,
           w_lora, a0, a_lora, k_k, k_a, r_k, lnx_g, lnx_b, w_oA, w_oB, w_out, final_g):
    prm = dict(norm_g=norm_g, w_in=w_in, mu_prev=mu_prev, mu_next=mu_next, lam_q1=lam_q1, lam_k1=lam_k1,
               lam_q2=lam_q2, lam_k2=lam_k2, subln_g=subln_g, w0=w0, w_lora=w_lora, a0=a0, a_lora=a_lora,
               k_k=k_k, k_a=k_a, r_k=r_k, lnx_g=lnx_g, lnx_b=lnx_b, w_oA=w_oA, w_oB=w_oB, w_out=w_out)
    depth = norm_g.shape[0]

    def trunk(x):
        for l in range(depth):
            x = _mixer_layer(x, l, prm, final_g, final_norm=(l == depth - 1))
        return x

    return (trunk(x_prompt), trunk(x_sample))
```

```python
import functools
import math

import jax
import jax.numpy as jnp
import numpy as np
from jax import lax
from jax.experimental import pallas as pl
from jax.experimental.pallas import tpu as pltpu

F32 = jnp.float32
BF16 = jnp.bfloat16

LANES = 128
DH_A = 64
N_B = 64
CHUNK = 64
ROPE_THETA = 10000.0
ATTN_SCALE = DH_A ** -0.5
LOG2_E = math.log2(math.e)
ATTN_KEY_CHUNK = 512
WKV_GROUP_LANES = 1024
NORM_EPS = 1e-6
SUBLN_EPS = 1e-5
LNX_EPS = 64e-5
MIB = 2 ** 20


def _tile(n, target, align):
    if n <= target:
        return n
    t = (target // align) * align
    while t >= align:
        if n % t == 0:
            return t
        t -= align
    raise ValueError(f"no tile for {n} (target {target}, align {align})")


def _params(semantics, vmem_mib):
    return pltpu.CompilerParams(dimension_semantics=semantics, vmem_limit_bytes=vmem_mib * MIB)


def _sigmoid(x):
    return 1.0 / (1.0 + jnp.exp(-x))


def _dot(a, b):
    return jnp.dot(a, b, preferred_element_type=F32)


def _dot_nt(a, b):
    return lax.dot_general(a, b, (((1,), (1,)), ((), ())), preferred_element_type=F32)


def _dot_tn(a, b):
    return lax.dot_general(a, b, (((0,), (0,)), ((), ())), preferred_element_type=F32)


def _rmsnorm_kernel(x_ref, g_ref, o_ref):
    x = x_ref[...]
    ms = jnp.mean(x * x, axis=-1, keepdims=True)
    o_ref[...] = (x * lax.rsqrt(ms + NORM_EPS) * g_ref[...]).astype(o_ref.dtype)


def _rmsnorm_bf16(x2, g):
    t, d = x2.shape
    tm = _tile(t, 256, 16)
    return pl.pallas_call(
        _rmsnorm_kernel,
        grid=(t // tm,),
        in_specs=[pl.BlockSpec((tm, d), lambda i: (i, 0)), pl.BlockSpec((1, d), lambda i: (0, 0))],
        out_specs=pl.BlockSpec((tm, d), lambda i: (i, 0)),
        out_shape=jax.ShapeDtypeStruct((t, d), BF16),
        compiler_params=_params(("parallel",), 32),
        name="rmsnorm_cast",
    )(x2, g.reshape(1, d))


def _matmul_kernel(a_ref, w_ref, o_ref):
    o_ref[...] = _dot(a_ref[...], w_ref[...]).astype(o_ref.dtype)


def _matmul_bf16(a, w):
    m, k = a.shape
    n = w.shape[1]
    tm = _tile(m, 1024, 16)
    tn = _tile(n, 1024, LANES)
    return pl.pallas_call(
        _matmul_kernel,
        grid=(m // tm, n // tn),
        in_specs=[pl.BlockSpec((tm, k), lambda i, j: (i, 0)), pl.BlockSpec((k, tn), lambda i, j: (0, j))],
        out_specs=pl.BlockSpec((tm, tn), lambda i, j: (i, j)),
        out_shape=jax.ShapeDtypeStruct((m, n), BF16),
        compiler_params=_params(("parallel", "arbitrary"), 56),
        name="in_proj",
    )(a, w)


def _rope_kernel(q_ref, k_ref, v_ref, cos_ref, sin_ref, qo_ref, ko_ref, vo_ref):
    cos = cos_ref[...]
    sin = sin_ref[...]
    lane = lax.broadcasted_iota(jnp.int32, cos.shape, 1)
    first_half = (lane % DH_A) < (DH_A // 2)

    def rope(x):
        partner = jnp.where(first_half,
                            pltpu.roll(x, LANES - DH_A // 2, axis=1),
                            pltpu.roll(x, DH_A // 2, axis=1))
        return x * cos + partner * sin

    qo_ref[0, 0] = (rope(q_ref[0].astype(F32)) * (ATTN_SCALE * LOG2_E)).astype(qo_ref.dtype)
    ko_ref[0, 0] = rope(k_ref[0].astype(F32)).astype(ko_ref.dtype)
    vo_ref[0, 0, 0] = v_ref[0].astype(F32).T.astype(vo_ref.dtype)


def _rope_tables(s):
    half = DH_A // 2
    inv = 1.0 / (ROPE_THETA ** (jnp.arange(0, DH_A, 2, dtype=F32) / DH_A))
    ang = jnp.arange(s, dtype=F32)[:, None] * inv[None, :]
    cos, sin = jnp.cos(ang), jnp.sin(ang)
    reps = LANES // half
    return jnp.tile(cos, (1, reps)), jnp.tile(jnp.concatenate([-sin, sin], axis=-1), (1, reps // 2))


def _attn_prepass(p3, w_a):
    b, s, _ = p3.shape
    h = w_a // LANES
    ts = _tile(s // 2, ATTN_KEY_CHUNK, LANES)
    cos, sin = _rope_tables(s)
    head_out = jax.ShapeDtypeStruct((b, h, s, LANES), BF16)
    vt_out = jax.ShapeDtypeStruct((b, h, s // ts, LANES, ts), BF16)
    col = lambda off: pl.BlockSpec((1, ts, LANES), lambda bi, si, hi: (bi, si, off + hi))
    tab = pl.BlockSpec((ts, LANES), lambda bi, si, hi: (si, 0))
    out = pl.BlockSpec((1, 1, ts, LANES), lambda bi, si, hi: (bi, hi, si, 0))
    out_t = pl.BlockSpec((1, 1, 1, LANES, ts), lambda bi, si, hi: (bi, hi, si, 0, 0))
    return pl.pallas_call(
        _rope_kernel,
        grid=(b, s // ts, h),
        in_specs=[col(0), col(h), col(2 * h), tab, tab],
        out_specs=[out, out, out_t],
        out_shape=[head_out, head_out, vt_out],
        compiler_params=_params(("parallel", "parallel", "arbitrary"), 32),
        name="attn_prepass",
    )(p3, p3, p3, cos, sin)


def _attn_kernel(q_ref, k_ref, vt_ref, z_ref, lq1_ref, lk1_ref, lq2_ref, lk2_ref, g_ref, o_ref,
                 qq_scr, acc_scr, sa_scr, sb_scr, *, qt, lam_init):
    tq = q_ref.shape[2]
    nk = vt_ref.shape[2]
    tk = vt_ref.shape[4]
    tiles = [slice(c * qt, (c + 1) * qt) for c in range(2 * tq // qt)]
    q = q_ref[0, 0].astype(F32)
    lane = lax.broadcasted_iota(jnp.int32, q.shape, 1)
    qq_scr[0:tq, :] = jnp.where(lane < DH_A, q, 0.0).astype(BF16)
    qq_scr[tq:2 * tq, :] = jnp.where(lane >= DH_A, q, 0.0).astype(BF16)
    acc_scr[...] = jnp.zeros(acc_scr.shape, F32)

    def produce(i, s_scr):
        off = pl.multiple_of(i * tk, tk)
        ks = k_ref[0, 0, pl.ds(off, tk), :]
        s = [_dot_nt(ks, qq_scr[t, :]) for t in tiles]
        for t, x in zip(tiles, s):
            s_scr[:, t] = x
        return jnp.concatenate([jnp.max(x, axis=0, keepdims=True) for x in s], axis=1)

    def consume(i, s_scr, cmax, m_old, l_old):
        vt = vt_ref[0, 0, i]
        m_new = jnp.maximum(m_old, cmax)
        alpha = jnp.exp2(m_old - m_new)
        p = [jnp.exp2(s_scr[:, t] - m_new[:, t]) for t in tiles]
        psum = jnp.concatenate([jnp.sum(x, axis=0, keepdims=True) for x in p], axis=1)
        pv = [_dot(vt, x.astype(BF16)) for x in p]
        for t, x in zip(tiles, pv):
            acc_scr[:, t] = alpha[:, t] * acc_scr[:, t] + x
        return m_new, alpha * l_old + psum

    def pair(j, carry):
        m, l, cmax_a = carry
        cmax_b = produce(2 * j + 1, sb_scr)
        m, l = consume(2 * j, sa_scr, cmax_a, m, l)
        cmax_a = produce(2 * j + 2, sa_scr)
        m, l = consume(2 * j + 1, sb_scr, cmax_b, m, l)
        return m, l, cmax_a

    m0 = jnp.full((1, 2 * tq), -jnp.inf, F32)
    l0 = jnp.zeros((1, 2 * tq), F32)
    m, l, cmax_a = lax.fori_loop(0, nk // 2 - 1, pair, (m0, l0, produce(0, sa_scr)))
    cmax_b = produce(nk - 1, sb_scr)
    m, l = consume(nk - 2, sa_scr, cmax_a, m, l)
    _, l_fin = consume(nk - 1, sb_scr, cmax_b, m, l)

    lam = (jnp.exp(jnp.sum(lq1_ref[...] * lk1_ref[...], keepdims=True))
           - jnp.exp(jnp.sum(lq2_ref[...] * lk2_ref[...], keepdims=True)) + lam_init)
    ot = acc_scr[...] / l_fin
    o = (ot[:, 0:tq] - lam * ot[:, tq:2 * tq]).T
    y = o * lax.rsqrt(jnp.mean(o * o, axis=-1, keepdims=True) + SUBLN_EPS) * g_ref[...] * (1.0 - lam_init)
    z = z_ref[0].astype(F32)
    o_ref[0] = (y * (z * _sigmoid(z))).astype(o_ref.dtype)


def _diff_attention(qr, kr, vt, p3, z_off, lam_q1, lam_k1, lam_q2, lam_k2, subln_g, lam_init):
    b, h, s, _ = qr.shape
    nk, tk = vt.shape[2], vt.shape[4]
    tq = _tile(s, 512, LANES)
    qt = _tile(2 * tq, 256, LANES)
    assert z_off % LANES == 0 and nk % 2 == 0
    zc = z_off // LANES
    vec = lambda n: pl.BlockSpec((1, n), lambda bi, hi, qi: (0, 0))
    kern = functools.partial(_attn_kernel, qt=qt, lam_init=lam_init)
    return pl.pallas_call(
        kern,
        grid=(b, h, s // tq),
        in_specs=[
            pl.BlockSpec((1, 1, tq, LANES), lambda bi, hi, qi: (bi, hi, qi, 0)),
            pl.BlockSpec((1, 1, s, LANES), lambda bi, hi, qi: (bi, hi, 0, 0)),
            pl.BlockSpec((1, 1, nk, LANES, tk), lambda bi, hi, qi: (bi, hi, 0, 0, 0)),
            pl.BlockSpec((1, tq, LANES), lambda bi, hi, qi: (bi, qi, zc + hi)),
            vec(DH_A), vec(DH_A), vec(DH_A), vec(DH_A), vec(LANES),
        ],
        out_specs=pl.BlockSpec((1, tq, LANES), lambda bi, hi, qi: (bi, qi, hi)),
        out_shape=jax.ShapeDtypeStruct((b, s, h * LANES), BF16),
        scratch_shapes=[
            pltpu.VMEM((2 * tq, LANES), BF16),
            pltpu.VMEM((LANES, 2 * tq), F32),
            pltpu.VMEM((tk, 2 * tq), F32),
            pltpu.VMEM((tk, 2 * tq), F32),
        ],
        compiler_params=_params(("parallel", "parallel", "arbitrary"), 48),
        name="diff_attn",
    )(qr, kr, vt, p3, lam_q1.reshape(1, DH_A), lam_k1.reshape(1, DH_A), lam_q2.reshape(1, DH_A),
      lam_k2.reshape(1, DH_A), subln_g.reshape(1, LANES))


def _group_sum(x, gmat):
    hi = x.astype(BF16)
    lo = (x - hi.astype(F32)).astype(BF16)
    return _dot(hi, gmat) + _dot(lo, gmat)


def _group_matrix(cw):
    r = lax.broadcasted_iota(jnp.int32, (cw, cw), 0) // N_B
    c = lax.broadcasted_iota(jnp.int32, (cw, cw), 1) // N_B
    return jnp.where(r == c, 1.0, 0.0).astype(BF16)


def _rwkv_prep_kernel(r_ref, k_ref, v_ref, lo_ref, rp_ref, kp_ref, vp_ref, lop_ref, rn_ref, kn_ref, vn_ref,
                      lon_ref, mu_ref, mulo_ref, w0_ref, a0_ref, wl_ref, al_ref, kk_ref, ka_ref, rk_ref,
                      ro_ref, vo_ref, ao_ref, lw_ref, kd_ref, bd_ref, bonus_ref):
    ts = r_ref.shape[1]
    si = pl.program_id(1)
    first = si == 0
    last = si == pl.num_programs(1) - 1

    def shift(cur_ref, prev_ref, next_ref, mu_p, mu_n):
        x = cur_ref[0].astype(F32)
        hp = prev_ref[0].astype(F32)
        hn = next_ref[0].astype(F32)
        pr = jnp.where(first, 0.0, hp[hp.shape[0] - 1:hp.shape[0], :])
        nx = jnp.where(last, 0.0, hn[0:1, :])
        row = lax.broadcasted_iota(jnp.int32, x.shape, 0)
        prev = jnp.where(row == 0, pr, pltpu.roll(x, 1, axis=0))
        nxt = jnp.where(row == ts - 1, nx, pltpu.roll(x, ts - 1, axis=0))
        return x + mu_p * (prev - x) + mu_n * (nxt - x)

    r = shift(r_ref, rp_ref, rn_ref, mu_ref[0, 0:1, :], mu_ref[1, 0:1, :])
    k = shift(k_ref, kp_ref, kn_ref, mu_ref[0, 1:2, :], mu_ref[1, 1:2, :])
    v = shift(v_ref, vp_ref, vn_ref, mu_ref[0, 2:3, :], mu_ref[1, 2:3, :])
    lo = shift(lo_ref, lop_ref, lon_ref, mulo_ref[0:1, :], mulo_ref[1:2, :])

    cw = r.shape[1]
    rl = lo.shape[1] // 4
    gmat = _group_matrix(cw)
    kkh = k * kk_ref[...]
    nrm = jnp.sqrt(_group_sum(kkh * kkh, gmat))
    kk = kkh / jnp.maximum(nrm, 1e-12)
    ka = ka_ref[...]
    ro_ref[0] = r.astype(ro_ref.dtype)
    vo_ref[0] = v.astype(vo_ref.dtype)
    ao_ref[0] = (-kk).astype(ao_ref.dtype)
    ksum = jnp.zeros_like(k)
    for d in range(2):
        pw = jnp.tanh(lo[:, d * rl:(d + 1) * rl]).astype(BF16)
        pa = lo[:, (2 + d) * rl:(3 + d) * rl].astype(BF16)
        wl = w0_ref[d:d + 1, :] + _dot(pw, wl_ref[d])
        lw_ref[d, 0] = -math.exp(-0.5) * _sigmoid(wl)
        a = _sigmoid(a0_ref[d:d + 1, :] + _dot(pa, al_ref[d]))
        kdir = k * (1.0 + (a - 1.0) * ka)
        kd_ref[d, 0] = kdir.astype(kd_ref.dtype)
        bd_ref[d, 0] = (kk * a).astype(bd_ref.dtype)
        ksum = ksum + kdir
    bonus_ref[0] = _group_sum(r * ksum * rk_ref[...], gmat) * v


def _rwkv_prep(p3, s_off, w_b, r_lora, mu_prev, mu_next, w0, w_lora, a0, a_lora, k_k, k_a, r_k):
    b, s, _ = p3.shape
    ts = _tile(s, 256, 16)
    cw = _tile(w_b, 256, LANES)
    lw = 4 * r_lora
    halo = 16
    nblk = s // halo
    assert s_off % cw == 0 and w_b % cw == 0 and (s_off + 3 * w_b) % lw == 0 and ts % halo == 0
    rc, kc, vc, lc = s_off // cw, (s_off + w_b) // cw, (s_off + 2 * w_b) // cw, (s_off + 3 * w_b) // lw
    per = ts // halo

    cur = lambda off: pl.BlockSpec((1, ts, cw), lambda bi, si, ji: (bi, si, off + ji))
    prv = lambda off: pl.BlockSpec((1, halo, cw), lambda bi, si, ji: (bi, jnp.maximum(si * per - 1, 0), off + ji))
    nxt = lambda off: pl.BlockSpec((1, halo, cw),
                                   lambda bi, si, ji: (bi, jnp.minimum((si + 1) * per, nblk - 1), off + ji))
    lo_cur = pl.BlockSpec((1, ts, lw), lambda bi, si, ji: (bi, si, lc))
    lo_prv = pl.BlockSpec((1, halo, lw), lambda bi, si, ji: (bi, jnp.maximum(si * per - 1, 0), lc))
    lo_nxt = pl.BlockSpec((1, halo, lw), lambda bi, si, ji: (bi, jnp.minimum((si + 1) * per, nblk - 1), lc))
    colvec = lambda rows: pl.BlockSpec((rows, cw), lambda bi, si, ji: (0, ji))

    mu_rkv = jnp.stack([mu_prev[:3 * w_b].reshape(3, w_b), mu_next[:3 * w_b].reshape(3, w_b)])
    mu_lo = jnp.stack([mu_prev[3 * w_b:], mu_next[3 * w_b:]])
    tok = pl.BlockSpec((1, ts, cw), lambda bi, si, ji: (bi, si, ji))
    tok2 = pl.BlockSpec((2, 1, ts, cw), lambda bi, si, ji: (0, bi, si, ji))
    shp = lambda dt: jax.ShapeDtypeStruct((b, s, w_b), dt)
    shp2 = lambda dt: jax.ShapeDtypeStruct((2, b, s, w_b), dt)
    return pl.pallas_call(
        _rwkv_prep_kernel,
        grid=(b, s // ts, w_b // cw),
        in_specs=[cur(rc), cur(kc), cur(vc), lo_cur, prv(rc), prv(kc), prv(vc), lo_prv,
                  nxt(rc), nxt(kc), nxt(vc), lo_nxt,
                  pl.BlockSpec((2, 3, cw), lambda bi, si, ji: (0, 0, ji)),
                  pl.BlockSpec((2, lw), lambda bi, si, ji: (0, 0)),
                  colvec(2), colvec(2),
                  pl.BlockSpec((2, r_lora, cw), lambda bi, si, ji: (0, 0, ji)),
                  pl.BlockSpec((2, r_lora, cw), lambda bi, si, ji: (0, 0, ji)),
                  colvec(1), colvec(1), colvec(1)],
        out_specs=[tok, tok, tok, tok2, tok2, tok2, tok],
        out_shape=[shp(BF16), shp(BF16), shp(BF16), shp2(F32), shp2(BF16), shp2(BF16), shp(F32)],
        compiler_params=_params(("parallel", "parallel", "arbitrary"), 48),
        name="rwkv_prep",
    )(p3, p3, p3, p3, p3, p3, p3, p3, p3, p3, p3, p3, mu_rkv, mu_lo, w0, a0,
      w_lora.astype(BF16), a_lora.astype(BF16), k_k.reshape(1, w_b), k_a.reshape(1, w_b), r_k.reshape(1, w_b))


def _wkv_kernel(mask_ref, r_ref, a_ref, v_ref, lw_ref, k_ref, b_ref, y_ref, z_scr, *, npairs):
    c = CHUNK
    ci = pl.program_id(3)

    @pl.when(ci == 0)
    def _():
        z_scr[...] = jnp.zeros(z_scr.shape, F32)

    strict = mask_ref[0, 0]
    incl = mask_ref[0, 1]
    cum = incl[0:c, 0:c].astype(BF16)
    lane = lax.broadcasted_iota(jnp.int32, (c, LANES), 1)
    low = lane < N_B
    row2 = lax.broadcasted_iota(jnp.int32, (LANES, LANES), 0)
    col2 = lax.broadcasted_iota(jnp.int32, (LANES, LANES), 1)
    eye = row2 == col2

    def stack(x):
        return jnp.concatenate([jnp.where(low, x, 0.0), jnp.where(low, 0.0, x)], axis=0)

    def each(f, *cols):
        return [f(*xs) for xs in zip(*cols)]

    sls = [slice(j * LANES, (j + 1) * LANES) for j in range(npairs)]
    lw = [lw_ref[0, 0, :, sl] for sl in sls]
    r = [r_ref[0, :, sl].astype(F32) for sl in sls]
    a = [a_ref[0, :, sl].astype(F32) for sl in sls]
    v = [v_ref[0, :, sl].astype(F32) for sl in sls]
    k = [k_ref[0, 0, :, sl].astype(F32) for sl in sls]
    b = [b_ref[0, 0, :, sl].astype(F32) for sl in sls]
    zb = [z_scr[j].astype(BF16) for j in range(npairs)]

    hi = each(lambda x: x.astype(BF16), lw)
    lo = each(lambda x, h: (x - h.astype(F32)).astype(BF16), lw, hi)
    lam = each(lambda h, l: _dot(cum, h) + _dot(cum, l), hi, lo)
    tot = each(lambda x: jnp.sum(x, axis=0, keepdims=True), lw)
    gdec = each(jnp.exp, tot)
    a_s = each(lambda x, lm, w: stack(x * jnp.exp(lm - w)), a, lam, lw)
    r_s = each(lambda x, lm: stack(x * jnp.exp(lm)), r, lam)
    e_neg = each(lambda lm: jnp.exp(-lm), lam)
    e_rem = each(lambda t, lm: jnp.exp(t - lm), tot, lam)
    b_s = each(lambda x, e: stack(x * e).astype(BF16), b, e_neg)
    k_s = each(lambda x, e: stack(x * e).astype(BF16), k, e_neg)
    bh_s = each(lambda x, e: stack(x * e).astype(BF16), b, e_rem)
    kh_s = each(lambda x, e: stack(x * e).astype(BF16), k, e_rem)
    v_s = each(lambda x: stack(x).astype(BF16), v)

    m4 = each(lambda x, y, p, q: _dot_nt(jnp.concatenate([x, y], axis=0).astype(BF16),
                                         jnp.concatenate([p, q], axis=0)), a_s, r_s, b_s, k_s)
    lp = each(lambda m: (m[0:LANES, 0:LANES] * strict).astype(BF16), m4)
    lak = each(lambda m: (m[0:LANES, LANES:] * strict).astype(BF16), m4)
    mrbk = each(lambda m: jnp.concatenate([m[LANES:, 0:LANES] * incl, m[LANES:, LANES:] * incl],
                                          axis=1).astype(BF16), m4)

    x = each(lambda p, q, w: jnp.concatenate([p, _dot(q, w)], axis=1), a_s, lak, v_s)
    steps = int(math.log2(c))
    for it in range(steps):
        x = each(lambda p, q: q + _dot(p, q.astype(BF16)), lp, x)
        if it < steps - 1:
            lp = each(lambda p: _dot(p, p).astype(BF16), lp)
    w_s = each(lambda q: q[:, 0:LANES].astype(BF16), x)
    uv = each(lambda q, w: jnp.concatenate([q[:, LANES:].astype(BF16), w], axis=0), x, v_s)

    rw = each(lambda p, m, w: (p + _dot(m[:, 0:LANES], w)).astype(BF16), r_s, mrbk, w_s)
    yv = each(_dot, mrbk, uv)
    pt = each(lambda g, p, w: (jnp.where(eye, g, 0.0) + _dot_tn(p, w)).astype(BF16), gdec, bh_s, w_s)
    qt = each(lambda p, q, u: _dot_tn(jnp.concatenate([p, q], axis=0), u), bh_s, kh_s, uv)

    ys = each(lambda p, z, q: _dot(p, z) + q, rw, zb, yv)
    z_new = each(lambda p, z, q: _dot(p, z) + q, pt, zb, qt)
    for j, sl in enumerate(sls):
        y_ref[0, 0, :, sl] = (ys[j][0:c, :] + ys[j][c:, :]).astype(y_ref.dtype)
        z_scr[j] = z_new[j]


def _wkv_masks():
    i = np.arange(LANES)
    same = (i[:, None] // CHUNK) == (i[None, :] // CHUNK)
    t, s = i[:, None] % CHUNK, i[None, :] % CHUNK
    fwd = np.stack([same & (s < t), same & (s <= t)])
    bwd = np.stack([same & (s > t), same & (s >= t)])
    return jnp.asarray(np.stack([fwd, bwd]).astype(np.float32))


def _wkv_scan(r, a, v, logw, kdir, bdir):
    b, s, w_b = r.shape
    c = CHUNK
    nc = s // c
    gw = _tile(w_b, WKV_GROUP_LANES, LANES)
    npairs = gw // LANES
    chunk_of = lambda d, ci: ci + d * (nc - 1 - 2 * ci)
    shared = pl.BlockSpec((1, c, gw), lambda d, bi, gi, ci: (bi, chunk_of(d, ci), gi))
    perdir = pl.BlockSpec((1, 1, c, gw), lambda d, bi, gi, ci: (d, bi, chunk_of(d, ci), gi))
    kern = functools.partial(_wkv_kernel, npairs=npairs)
    return pl.pallas_call(
        kern,
        grid=(2, b, w_b // gw, nc),
        in_specs=[pl.BlockSpec((1, 2, LANES, LANES), lambda d, bi, gi, ci: (d, 0, 0, 0)),
                  shared, shared, shared, perdir, perdir, perdir],
        out_specs=perdir,
        out_shape=jax.ShapeDtypeStruct((2, b, s, w_b), BF16),
        scratch_shapes=[pltpu.VMEM((npairs, LANES, LANES), F32)],
        compiler_params=_params(("parallel", "parallel", "parallel", "arbitrary"), 32),
        name="wkv_scan",
    )(_wkv_masks(), r, a, v, logw, kdir, bdir)


def _rwkv_out_kernel(y_ref, bonus_ref, z_ref, g_ref, b_ref, o_ref):
    y = y_ref[0, 0].astype(F32) + y_ref[1, 0].astype(F32)
    gmat = _group_matrix(y.shape[1])
    mu = _group_sum(y, gmat) * (1.0 / N_B)
    yc = y - mu
    var = _group_sum(yc * yc, gmat) * (1.0 / N_B)
    yn = yc * lax.rsqrt(var + LNX_EPS) * g_ref[...] + b_ref[...]
    z = z_ref[0].astype(F32)
    o_ref[0] = ((yn + bonus_ref[0]) * (z * _sigmoid(z))).astype(o_ref.dtype)


def _rwkv_out(y2, bonus, p3, z_off, lnx_g, lnx_b):
    _, b, s, w_b = y2.shape
    ts = _tile(s, 512, 16)
    cw = _tile(w_b, 256, LANES)
    assert z_off % cw == 0
    zc = z_off // cw
    vec = pl.BlockSpec((1, cw), lambda bi, si, ji: (0, ji))
    return pl.pallas_call(
        _rwkv_out_kernel,
        grid=(b, s // ts, w_b // cw),
        in_specs=[pl.BlockSpec((2, 1, ts, cw), lambda bi, si, ji: (0, bi, si, ji)),
                  pl.BlockSpec((1, ts, cw), lambda bi, si, ji: (bi, si, ji)),
                  pl.BlockSpec((1, ts, cw), lambda bi, si, ji: (bi, si, zc + ji)),
                  vec, vec],
        out_specs=pl.BlockSpec((1, ts, cw), lambda bi, si, ji: (bi, si, ji)),
        out_shape=jax.ShapeDtypeStruct((b, s, w_b), BF16),
        compiler_params=_params(("parallel", "parallel", "parallel"), 32),
        name="rwkv_out",
    )(y2, bonus, p3, lnx_g.reshape(1, w_b), lnx_b.reshape(1, w_b))


def _merge_kernel(ua_ref, ub_ref, wa_ref, wb_ref, ga_ref, gb_ref, o_ref):
    oa = _dot(ua_ref[...], wa_ref[...])
    ob = _dot(ub_ref[...], wb_ref[...])
    m = _sigmoid(ga_ref[...].astype(F32)) * oa + _sigmoid(gb_ref[...].astype(F32)) * ob
    o_ref[...] = m.astype(o_ref.dtype)


def _merge(ua, ub, w_oa, w_ob, p2, g_off):
    t, w_a = ua.shape
    w_b = ub.shape[1]
    d = w_oa.shape[1]
    tm = _tile(t, 512, 16)
    tn = _tile(d, 512, LANES)
    assert g_off % tn == 0 and d % tn == 0
    gc = g_off // tn
    nd = d // tn
    return pl.pallas_call(
        _merge_kernel,
        grid=(t // tm, nd),
        in_specs=[pl.BlockSpec((tm, w_a), lambda i, j: (i, 0)),
                  pl.BlockSpec((tm, w_b), lambda i, j: (i, 0)),
                  pl.BlockSpec((w_a, tn), lambda i, j: (0, j)),
                  pl.BlockSpec((w_b, tn), lambda i, j: (0, j)),
                  pl.BlockSpec((tm, tn), lambda i, j: (i, gc + j)),
                  pl.BlockSpec((tm, tn), lambda i, j: (i, gc + nd + j))],
        out_specs=pl.BlockSpec((tm, tn), lambda i, j: (i, j)),
        out_shape=jax.ShapeDtypeStruct((t, d), BF16),
        compiler_params=_params(("parallel", "arbitrary"), 48),
        name="gate_merge",
    )(ua, ub, w_oa, w_ob, p2, p2)


def _out_kernel(m_ref, w_ref, x_ref, g_ref, o_ref, *, final_norm):
    kk = pl.program_id(1)

    @pl.when(kk == 0)
    def _():
        o_ref[...] = x_ref[...]

    o_ref[...] += _dot(m_ref[...], w_ref[...])

    if final_norm:
        @pl.when(kk == pl.num_programs(1) - 1)
        def _():
            o = o_ref[...]
            o_ref[...] = o * lax.rsqrt(jnp.mean(o * o, axis=-1, keepdims=True) + NORM_EPS) * g_ref[...]


def _out_proj(m, w_out, x2, final_g, final_norm):
    t, d = x2.shape
    tm = _tile(t, 512, 16)
    tk = _tile(d, 512, LANES)
    kern = functools.partial(_out_kernel, final_norm=final_norm)
    return pl.pallas_call(
        kern,
        grid=(t // tm, d // tk),
        in_specs=[pl.BlockSpec((tm, tk), lambda i, k: (i, k)),
                  pl.BlockSpec((tk, d), lambda i, k: (k, 0)),
                  pl.BlockSpec((tm, d), lambda i, k: (i, 0)),
                  pl.BlockSpec((1, d), lambda i, k: (0, 0))],
        out_specs=pl.BlockSpec((tm, d), lambda i, k: (i, 0)),
        out_shape=jax.ShapeDtypeStruct((t, d), F32),
        compiler_params=_params(("parallel", "arbitrary"), 56),
        name="out_proj",
    )(m, w_out, x2, final_g.reshape(1, d))


def _lambda_init(layer_idx):
    return 0.8 - 0.6 * math.exp(-0.3 * layer_idx)


def _mixer_layer(x, l, prm, final_g, final_norm):
    b, s, d = x.shape
    w_a = prm["w_oA"][l].shape[0]
    w_b = prm["w_oB"][l].shape[0]
    r_lora = prm["w_lora"].shape[2]
    c_shift = 3 * w_b + 4 * r_lora
    n_in = prm["w_in"].shape[2]
    assert n_in == 3 * w_a + c_shift + w_a + w_b + 2 * d
    assert w_a % LANES == 0 and w_b % LANES == 0 and s % CHUNK == 0
    s_off = 3 * w_a
    za_off = s_off + c_shift
    zb_off = za_off + w_a
    g_off = zb_off + w_b
    lam_init = _lambda_init(l)

    x2 = x.reshape(b * s, d)
    h = _rmsnorm_bf16(x2, prm["norm_g"][l])
    p2 = _matmul_bf16(h, prm["w_in"][l].astype(BF16))
    p3 = p2.reshape(b, s, n_in)

    qr, kr, vt = _attn_prepass(p3, w_a)
    ua = _diff_attention(qr, kr, vt, p3, za_off, prm["lam_q1"][l], prm["lam_k1"][l], prm["lam_q2"][l],
                         prm["lam_k2"][l], prm["subln_g"][l], lam_init)

    r, v, a, logw, kdir, bdir, bonus = _rwkv_prep(
        p3, s_off, w_b, r_lora, prm["mu_prev"][l], prm["mu_next"][l], prm["w0"][l], prm["w_lora"][l],
        prm["a0"][l], prm["a_lora"][l], prm["k_k"][l], prm["k_a"][l], prm["r_k"][l])
    y2 = _wkv_scan(r, a, v, logw, kdir, bdir)
    ub = _rwkv_out(y2, bonus, p3, zb_off, prm["lnx_g"][l], prm["lnx_b"][l])

    m = _merge(ua.reshape(b * s, w_a), ub.reshape(b * s, w_b), prm["w_oA"][l].astype(BF16),
               prm["w_oB"][l].astype(BF16), p2, g_off)
    out = _out_proj(m, prm["w_out"][l].astype(BF16), x2, final_g, final_norm)
    return out.reshape(b, s, d)


def kernel(x_prompt, x_sample, norm_g, w_in, mu_prev, mu_next, lam_q1, lam_k1, lam_q2, lam_k2, subln_g, w0,
           w_lora, a0, a_lora, k_k, k_a, r_k, lnx_g, lnx_b, w_oA, w_oB, w_out, final_g):
    prm = dict(norm_g=norm_g, w_in=w_in, mu_prev=mu_prev, mu_next=mu_next, lam_q1=lam_q1, lam_k1=lam_k1,
               lam_q2=lam_q2, lam_k2=lam_k2, subln_g=subln_g, w0=w0, w_lora=w_lora, a0=a0, a_lora=a_lora,
               k_k=k_k, k_a=k_a, r_k=r_k, lnx_g=lnx_g, lnx_b=lnx_b, w_oA=w_oA, w_oB=w_oB, w_out=w_out)
    depth = norm_g.shape[0]

    def trunk(x):
        for l in range(depth):
            x = _mixer_layer(x, l, prm, final_g, final_norm=(l == depth - 1))
        return x

    return (trunk(x_prompt), trunk(x_sample))
```

```python
import functools
import math

import jax
import jax.numpy as jnp
import numpy as np
from jax import lax
from jax.experimental import pallas as pl
from jax.experimental.pallas import tpu as pltpu

F32 = jnp.float32
BF16 = jnp.bfloat16

LANES = 128
DH_A = 64
N_B = 64
CHUNK = 64
ROPE_THETA = 10000.0
ATTN_SCALE = DH_A ** -0.5
LOG2_E = math.log2(math.e)
ATTN_KEY_CHUNK = 512
WKV_GROUP_LANES = 1024
VT_ROWS = LANES + 16
NORM_EPS = 1e-6
SUBLN_EPS = 1e-5
LNX_EPS = 64e-5
MIB = 2 ** 20


def _tile(n, target, align):
    if n <= target:
        return n
    t = (target // align) * align
    while t >= align:
        if n % t == 0:
            return t
        t -= align
    raise ValueError(f"no tile for {n} (target {target}, align {align})")


def _params(semantics, vmem_mib):
    return pltpu.CompilerParams(dimension_semantics=semantics, vmem_limit_bytes=vmem_mib * MIB)


def _sigmoid(x):
    return 1.0 / (1.0 + jnp.exp(-x))


def _dot(a, b):
    return jnp.dot(a, b, preferred_element_type=F32)


def _dot_nt(a, b):
    return lax.dot_general(a, b, (((1,), (1,)), ((), ())), preferred_element_type=F32)


def _dot_tn(a, b):
    return lax.dot_general(a, b, (((0,), (0,)), ((), ())), preferred_element_type=F32)


def _rmsnorm_kernel(x_ref, g_ref, o_ref):
    x = x_ref[...]
    ms = jnp.mean(x * x, axis=-1, keepdims=True)
    o_ref[...] = (x * lax.rsqrt(ms + NORM_EPS) * g_ref[...]).astype(o_ref.dtype)


def _rmsnorm_bf16(x2, g):
    t, d = x2.shape
    tm = _tile(t, 256, 16)
    return pl.pallas_call(
        _rmsnorm_kernel,
        grid=(t // tm,),
        in_specs=[pl.BlockSpec((tm, d), lambda i: (i, 0)), pl.BlockSpec((1, d), lambda i: (0, 0))],
        out_specs=pl.BlockSpec((tm, d), lambda i: (i, 0)),
        out_shape=jax.ShapeDtypeStruct((t, d), BF16),
        compiler_params=_params(("parallel",), 32),
        name="rmsnorm_cast",
    )(x2, g.reshape(1, d))


def _matmul_kernel(a_ref, w_ref, o_ref):
    o_ref[...] = _dot(a_ref[...], w_ref[...]).astype(o_ref.dtype)


def _matmul_bf16(a, w):
    m, k = a.shape
    n = w.shape[1]
    tm = _tile(m, 1024, 16)
    tn = _tile(n, 1024, 2 * LANES)
    return pl.pallas_call(
        _matmul_kernel,
        grid=(m // tm, n // tn),
        in_specs=[pl.BlockSpec((tm, k), lambda i, j: (i, 0)), pl.BlockSpec((k, tn), lambda i, j: (0, j))],
        out_specs=pl.BlockSpec((tm, tn), lambda i, j: (i, j)),
        out_shape=jax.ShapeDtypeStruct((m, n), BF16),
        compiler_params=_params(("parallel", "arbitrary"), 56),
        name="in_proj",
    )(a, w)


def _rope_kernel(q_ref, k_ref, v_ref, cos_ref, sin_ref, qo_ref, ko_ref, vo_ref):
    cos = cos_ref[...]
    sin = sin_ref[...]
    lane = lax.broadcasted_iota(jnp.int32, cos.shape, 1)
    first_half = (lane % DH_A) < (DH_A // 2)

    def rope(x):
        partner = jnp.where(first_half,
                            pltpu.roll(x, LANES - DH_A // 2, axis=1),
                            pltpu.roll(x, DH_A // 2, axis=1))
        return x * cos + partner * sin

    for hh in range(qo_ref.shape[1]):
        cols = slice(hh * LANES, (hh + 1) * LANES)
        qo_ref[0, hh] = (rope(q_ref[0, :, cols].astype(F32)) * (ATTN_SCALE * LOG2_E)).astype(qo_ref.dtype)
        ko_ref[0, hh] = rope(k_ref[0, :, cols].astype(F32)).astype(ko_ref.dtype)
        vo_ref[0, hh, 0, 0:LANES, :] = v_ref[0, :, cols].astype(F32).T.astype(vo_ref.dtype)
        vo_ref[0, hh, 0, LANES:, :] = jnp.ones((vo_ref.shape[3] - LANES, vo_ref.shape[4]), vo_ref.dtype)


def _rope_tables(s):
    half = DH_A // 2
    inv = 1.0 / (ROPE_THETA ** (jnp.arange(0, DH_A, 2, dtype=F32) / DH_A))
    ang = jnp.arange(s, dtype=F32)[:, None] * inv[None, :]
    cos, sin = jnp.cos(ang), jnp.sin(ang)
    reps = LANES // half
    return jnp.tile(cos, (1, reps)), jnp.tile(jnp.concatenate([-sin, sin], axis=-1), (1, reps // 2))


def _attn_prepass(p3, w_a):
    b, s, _ = p3.shape
    h = w_a // LANES
    ts = _tile(s // 2, ATTN_KEY_CHUNK, LANES)
    cos, sin = _rope_tables(s)
    head_out = jax.ShapeDtypeStruct((b, h, s, LANES), BF16)
    vt_out = jax.ShapeDtypeStruct((b, h, s // ts, VT_ROWS, ts), BF16)
    hp = _tile(h, 4, 1)
    hg = h // hp
    col = lambda off: pl.BlockSpec((1, ts, hp * LANES), lambda bi, si, hi: (bi, si, off + hi))
    tab = pl.BlockSpec((ts, LANES), lambda bi, si, hi: (si, 0))
    out = pl.BlockSpec((1, hp, ts, LANES), lambda bi, si, hi: (bi, hi, si, 0))
    out_t = pl.BlockSpec((1, hp, 1, VT_ROWS, ts), lambda bi, si, hi: (bi, hi, si, 0, 0))
    return pl.pallas_call(
        _rope_kernel,
        grid=(b, s // ts, hg),
        in_specs=[col(0), col(hg), col(2 * hg), tab, tab],
        out_specs=[out, out, out_t],
        out_shape=[head_out, head_out, vt_out],
        compiler_params=_params(("parallel", "parallel", "arbitrary"), 32),
        name="attn_prepass",
    )(p3, p3, p3, cos, sin)


def _attn_kernel(q_ref, k_ref, vt_ref, z_ref, lq1_ref, lk1_ref, lq2_ref, lk2_ref, g_ref, o_ref,
                 qq_scr, acc_scr, sa_scr, sb_scr, *, qt, lam_init):
    tq = q_ref.shape[2]
    nk = vt_ref.shape[2]
    tk = vt_ref.shape[4]
    tiles = [slice(c * qt, (c + 1) * qt) for c in range(2 * tq // qt)]
    q = q_ref[0, 0].astype(F32)
    lane = lax.broadcasted_iota(jnp.int32, q.shape, 1)
    qq_scr[0:tq, :] = jnp.where(lane < DH_A, q, 0.0).astype(BF16)
    qq_scr[tq:2 * tq, :] = jnp.where(lane >= DH_A, q, 0.0).astype(BF16)
    acc_scr[...] = jnp.zeros(acc_scr.shape, F32)

    def produce(i, s_scr):
        off = pl.multiple_of(i * tk, tk)
        ks = k_ref[0, 0, pl.ds(off, tk), :]
        s = [_dot_nt(ks, qq_scr[t, :]) for t in tiles]
        for t, x in zip(tiles, s):
            s_scr[:, t] = x
        return jnp.concatenate([jnp.max(x, axis=0, keepdims=True) for x in s], axis=1)

    def consume(i, s_scr, cmax, m_old):
        vt = vt_ref[0, 0, i]
        m_new = jnp.maximum(m_old, cmax)
        alpha = jnp.exp2(m_old - m_new)
        p = [jnp.exp2(s_scr[:, t] - m_new[:, t]).astype(BF16) for t in tiles]
        pv = [_dot(vt, x) for x in p]
        for t, x in zip(tiles, pv):
            acc_scr[:, t] = alpha[:, t] * acc_scr[:, t] + x
        return m_new

    def pair(j, carry):
        m, cmax_a = carry
        cmax_b = produce(2 * j + 1, sb_scr)
        m = consume(2 * j, sa_scr, cmax_a, m)
        cmax_a = produce(2 * j + 2, sa_scr)
        m = consume(2 * j + 1, sb_scr, cmax_b, m)
        return m, cmax_a

    m0 = jnp.full((1, 2 * tq), -jnp.inf, F32)
    trips = nk // 2 - 1
    unroll = next(u for u in (3, 2, 1) if trips % u == 0)
    m, cmax_a = lax.fori_loop(0, trips, pair, (m0, produce(0, sa_scr)), unroll=unroll)
    cmax_b = produce(nk - 1, sb_scr)
    m = consume(nk - 2, sa_scr, cmax_a, m)
    consume(nk - 1, sb_scr, cmax_b, m)

    lam = (jnp.exp(jnp.sum(lq1_ref[...] * lk1_ref[...], keepdims=True))
           - jnp.exp(jnp.sum(lq2_ref[...] * lk2_ref[...], keepdims=True)) + lam_init)
    ot = acc_scr[0:LANES, :] / acc_scr[LANES:LANES + 1, :]
    o = (ot[:, 0:tq] - lam * ot[:, tq:2 * tq]).T
    y = o * lax.rsqrt(jnp.mean(o * o, axis=-1, keepdims=True) + SUBLN_EPS) * g_ref[...] * (1.0 - lam_init)
    z = z_ref[0].astype(F32)
    o_ref[0] = (y * (z * _sigmoid(z))).astype(o_ref.dtype)


def _diff_attention(qr, kr, vt, p3, z_off, lam_q1, lam_k1, lam_q2, lam_k2, subln_g, lam_init):
    b, h, s, _ = qr.shape
    nk, tk = vt.shape[2], vt.shape[4]
    tq = _tile(s, 512, LANES)
    qt = _tile(2 * tq, 256, LANES)
    assert z_off % LANES == 0 and nk % 2 == 0
    zc = z_off // LANES
    vec = lambda n: pl.BlockSpec((1, n), lambda bi, hi, qi: (0, 0))
    kern = functools.partial(_attn_kernel, qt=qt, lam_init=lam_init)
    return pl.pallas_call(
        kern,
        grid=(b, h, s // tq),
        in_specs=[
            pl.BlockSpec((1, 1, tq, LANES), lambda bi, hi, qi: (bi, hi, qi, 0)),
            pl.BlockSpec((1, 1, s, LANES), lambda bi, hi, qi: (bi, hi, 0, 0)),
            pl.BlockSpec((1, 1, nk, VT_ROWS, tk), lambda bi, hi, qi: (bi, hi, 0, 0, 0)),
            pl.BlockSpec((1, tq, LANES), lambda bi, hi, qi: (bi, qi, zc + hi)),
            vec(DH_A), vec(DH_A), vec(DH_A), vec(DH_A), vec(LANES),
        ],
        out_specs=pl.BlockSpec((1, tq, LANES), lambda bi, hi, qi: (bi, qi, hi)),
        out_shape=jax.ShapeDtypeStruct((b, s, h * LANES), BF16),
        scratch_shapes=[
            pltpu.VMEM((2 * tq, LANES), BF16),
            pltpu.VMEM((VT_ROWS, 2 * tq), F32),
            pltpu.VMEM((tk, 2 * tq), F32),
            pltpu.VMEM((tk, 2 * tq), F32),
        ],
        compiler_params=_params(("parallel", "parallel", "arbitrary"), 48),
        name="diff_attn",
    )(qr, kr, vt, p3, lam_q1.reshape(1, DH_A), lam_k1.reshape(1, DH_A), lam_q2.reshape(1, DH_A),
      lam_k2.reshape(1, DH_A), subln_g.reshape(1, LANES))


def _group_sum(x, gmat):
    hi = x.astype(BF16)
    lo = (x - hi.astype(F32)).astype(BF16)
    return _dot(hi, gmat) + _dot(lo, gmat)


def _group_matrix(cw):
    r = lax.broadcasted_iota(jnp.int32, (cw, cw), 0) // N_B
    c = lax.broadcasted_iota(jnp.int32, (cw, cw), 1) // N_B
    return jnp.where(r == c, 1.0, 0.0).astype(BF16)


def _rwkv_prep_kernel(r_ref, k_ref, v_ref, lo_ref, rp_ref, kp_ref, vp_ref, lop_ref, rn_ref, kn_ref, vn_ref,
                      lon_ref, mu_ref, mulo_ref, w0_ref, a0_ref, wl_ref, al_ref, kk_ref, ka_ref, rk_ref,
                      ro_ref, vo_ref, ao_ref, lw_ref, kd_ref, bd_ref, bonus_ref, pw_scr, pa_scr):
    ts = r_ref.shape[1]
    si = pl.program_id(1)
    first = si == 0
    last = si == pl.num_programs(1) - 1

    def shift(cur_ref, prev_ref, next_ref, mu_p, mu_n):
        x = cur_ref[0].astype(F32)
        hp = prev_ref[0].astype(F32)
        hn = next_ref[0].astype(F32)
        pr = jnp.where(first, 0.0, hp[hp.shape[0] - 1:hp.shape[0], :])
        nx = jnp.where(last, 0.0, hn[0:1, :])
        row = lax.broadcasted_iota(jnp.int32, x.shape, 0)
        prev = jnp.where(row == 0, pr, pltpu.roll(x, 1, axis=0))
        nxt = jnp.where(row == ts - 1, nx, pltpu.roll(x, ts - 1, axis=0))
        return x + mu_p * (prev - x) + mu_n * (nxt - x)

    r = shift(r_ref, rp_ref, rn_ref, mu_ref[0, 0:1, :], mu_ref[1, 0:1, :])
    k = shift(k_ref, kp_ref, kn_ref, mu_ref[0, 1:2, :], mu_ref[1, 1:2, :])
    v = shift(v_ref, vp_ref, vn_ref, mu_ref[0, 2:3, :], mu_ref[1, 2:3, :])

    @pl.when(pl.program_id(2) == 0)
    def _():
        lo = shift(lo_ref, lop_ref, lon_ref, mulo_ref[0:1, :], mulo_ref[1:2, :])
        rl = lo.shape[1] // 4
        for d in range(2):
            pw_scr[d] = jnp.tanh(lo[:, d * rl:(d + 1) * rl]).astype(BF16)
            pa_scr[d] = lo[:, (2 + d) * rl:(3 + d) * rl].astype(BF16)

    cw = r.shape[1]
    gmat = _group_matrix(cw)
    kkh = k * kk_ref[...]
    nrm = jnp.sqrt(_group_sum(kkh * kkh, gmat))
    kk = kkh / jnp.maximum(nrm, 1e-12)
    ka = ka_ref[...]
    ro_ref[0] = r.astype(ro_ref.dtype)
    vo_ref[0] = v.astype(vo_ref.dtype)
    ao_ref[0] = (-kk).astype(ao_ref.dtype)
    ksum = jnp.zeros_like(k)
    for d in range(2):
        pw = pw_scr[d]
        pa = pa_scr[d]
        wl = w0_ref[d:d + 1, :] + _dot(pw, wl_ref[d])
        lw_ref[d, 0] = -math.exp(-0.5) * _sigmoid(wl)
        a = _sigmoid(a0_ref[d:d + 1, :] + _dot(pa, al_ref[d]))
        kdir = k * (1.0 + (a - 1.0) * ka)
        kd_ref[d, 0] = kdir.astype(kd_ref.dtype)
        bd_ref[d, 0] = (kk * a).astype(bd_ref.dtype)
        ksum = ksum + kdir
    bonus_ref[0] = _group_sum(r * ksum * rk_ref[...], gmat) * v


def _rwkv_prep(p3, s_off, w_b, r_lora, mu_prev, mu_next, w0, w_lora, a0, a_lora, k_k, k_a, r_k):
    b, s, _ = p3.shape
    ts = _tile(s, 256, 16)
    cw = _tile(w_b, 256, LANES)
    lw = 4 * r_lora
    halo = 16
    nblk = s // halo
    assert s_off % cw == 0 and w_b % cw == 0 and (s_off + 3 * w_b) % lw == 0 and ts % halo == 0
    rc, kc, vc, lc = s_off // cw, (s_off + w_b) // cw, (s_off + 2 * w_b) // cw, (s_off + 3 * w_b) // lw
    per = ts // halo

    cur = lambda off: pl.BlockSpec((1, ts, cw), lambda bi, si, ji: (bi, si, off + ji))
    prv = lambda off: pl.BlockSpec((1, halo, cw), lambda bi, si, ji: (bi, jnp.maximum(si * per - 1, 0), off + ji))
    nxt = lambda off: pl.BlockSpec((1, halo, cw),
                                   lambda bi, si, ji: (bi, jnp.minimum((si + 1) * per, nblk - 1), off + ji))
    lo_cur = pl.BlockSpec((1, ts, lw), lambda bi, si, ji: (bi, si, lc))
    lo_prv = pl.BlockSpec((1, halo, lw), lambda bi, si, ji: (bi, jnp.maximum(si * per - 1, 0), lc))
    lo_nxt = pl.BlockSpec((1, halo, lw), lambda bi, si, ji: (bi, jnp.minimum((si + 1) * per, nblk - 1), lc))
    colvec = lambda rows: pl.BlockSpec((rows, cw), lambda bi, si, ji: (0, ji))

    mu_rkv = jnp.stack([mu_prev[:3 * w_b].reshape(3, w_b), mu_next[:3 * w_b].reshape(3, w_b)])
    mu_lo = jnp.stack([mu_prev[3 * w_b:], mu_next[3 * w_b:]])
    tok = pl.BlockSpec((1, ts, cw), lambda bi, si, ji: (bi, si, ji))
    tok2 = pl.BlockSpec((2, 1, ts, cw), lambda bi, si, ji: (0, bi, si, ji))
    shp = lambda dt: jax.ShapeDtypeStruct((b, s, w_b), dt)
    shp2 = lambda dt: jax.ShapeDtypeStruct((2, b, s, w_b), dt)
    return pl.pallas_call(
        _rwkv_prep_kernel,
        grid=(b, s // ts, w_b // cw),
        in_specs=[cur(rc), cur(kc), cur(vc), lo_cur, prv(rc), prv(kc), prv(vc), lo_prv,
                  nxt(rc), nxt(kc), nxt(vc), lo_nxt,
                  pl.BlockSpec((2, 3, cw), lambda bi, si, ji: (0, 0, ji)),
                  pl.BlockSpec((2, lw), lambda bi, si, ji: (0, 0)),
                  colvec(2), colvec(2),
                  pl.BlockSpec((2, r_lora, cw), lambda bi, si, ji: (0, 0, ji)),
                  pl.BlockSpec((2, r_lora, cw), lambda bi, si, ji: (0, 0, ji)),
                  colvec(1), colvec(1), colvec(1)],
        out_specs=[tok, tok, tok, tok2, tok2, tok2, tok],
        out_shape=[shp(BF16), shp(BF16), shp(BF16), shp2(F32), shp2(BF16), shp2(BF16), shp(F32)],
        scratch_shapes=[pltpu.VMEM((2, ts, r_lora), BF16), pltpu.VMEM((2, ts, r_lora), BF16)],
        compiler_params=_params(("parallel", "parallel", "arbitrary"), 48),
        name="rwkv_prep",
    )(p3, p3, p3, p3, p3, p3, p3, p3, p3, p3, p3, p3, mu_rkv, mu_lo, w0, a0,
      w_lora.astype(BF16), a_lora.astype(BF16), k_k.reshape(1, w_b), k_a.reshape(1, w_b), r_k.reshape(1, w_b))


def _wkv_kernel(mask_ref, r_ref, a_ref, v_ref, lw_ref, k_ref, b_ref, y_ref, z_scr, *, npairs):
    c = CHUNK
    ci = pl.program_id(3)

    @pl.when(ci == 0)
    def _():
        z_scr[...] = jnp.zeros(z_scr.shape, F32)

    strict = mask_ref[0, 0]
    incl = mask_ref[0, 1]
    cum = incl[0:c, 0:c].astype(BF16)
    lane = lax.broadcasted_iota(jnp.int32, (c, LANES), 1)
    low = lane < N_B
    row2 = lax.broadcasted_iota(jnp.int32, (LANES, LANES), 0)
    col2 = lax.broadcasted_iota(jnp.int32, (LANES, LANES), 1)
    eye = row2 == col2

    def stack(x):
        return jnp.concatenate([jnp.where(low, x, 0.0), jnp.where(low, 0.0, x)], axis=0)

    def each(f, *cols):
        return [f(*xs) for xs in zip(*cols)]

    sls = [slice(j * LANES, (j + 1) * LANES) for j in range(npairs)]
    lw = [lw_ref[0, 0, :, sl] for sl in sls]
    r = [r_ref[0, :, sl].astype(F32) for sl in sls]
    a = [a_ref[0, :, sl].astype(F32) for sl in sls]
    v = [v_ref[0, :, sl].astype(F32) for sl in sls]
    k = [k_ref[0, 0, :, sl].astype(F32) for sl in sls]
    b = [b_ref[0, 0, :, sl].astype(F32) for sl in sls]
    zb = [z_scr[j].astype(BF16) for j in range(npairs)]

    hi = each(lambda x: x.astype(BF16), lw)
    lo = each(lambda x, h: (x - h.astype(F32)).astype(BF16), lw, hi)
    lam = each(lambda h, l: _dot(cum, h) + _dot(cum, l), hi, lo)
    tot = each(lambda x: jnp.sum(x, axis=0, keepdims=True), lw)
    gdec = each(jnp.exp, tot)
    a_s = each(lambda x, lm, w: stack(x * jnp.exp(lm - w)), a, lam, lw)
    r_s = each(lambda x, lm: stack(x * jnp.exp(lm)), r, lam)
    e_neg = each(lambda lm: jnp.exp(-lm), lam)
    e_rem = each(lambda t, lm: jnp.exp(t - lm), tot, lam)
    b_s = each(lambda x, e: stack(x * e).astype(BF16), b, e_neg)
    k_s = each(lambda x, e: stack(x * e).astype(BF16), k, e_neg)
    bh_s = each(lambda x, e: stack(x * e).astype(BF16), b, e_rem)
    kh_s = each(lambda x, e: stack(x * e).astype(BF16), k, e_rem)
    v_s = each(lambda x: stack(x).astype(BF16), v)

    m4 = each(lambda x, y, p, q: _dot_nt(jnp.concatenate([x, y], axis=0).astype(BF16),
                                         jnp.concatenate([p, q], axis=0)), a_s, r_s, b_s, k_s)
    lp = each(lambda m: (m[0:LANES, 0:LANES] * strict).astype(BF16), m4)
    lak = each(lambda m: (m[0:LANES, LANES:] * strict).astype(BF16), m4)
    mrbk = each(lambda m: jnp.concatenate([m[LANES:, 0:LANES] * incl, m[LANES:, LANES:] * incl],
                                          axis=1).astype(BF16), m4)

    x = each(lambda p, q, w: jnp.concatenate([p, _dot(q, w)], axis=1), a_s, lak, v_s)
    steps = int(math.log2(c))
    for it in range(steps):
        x = each(lambda p, q: q + _dot(p, q.astype(BF16)), lp, x)
        if it < steps - 1:
            lp = each(lambda p: _dot(p, p).astype(BF16), lp)
    w_s = each(lambda q: q[:, 0:LANES].astype(BF16), x)
    uv = each(lambda q, w: jnp.concatenate([q[:, LANES:].astype(BF16), w], axis=0), x, v_s)

    rw = each(lambda p, m, w: (p + _dot(m[:, 0:LANES], w)).astype(BF16), r_s, mrbk, w_s)
    yv = each(_dot, mrbk, uv)
    pt = each(lambda g, p, w: (jnp.where(eye, g, 0.0) + _dot_tn(p, w)).astype(BF16), gdec, bh_s, w_s)
    qt = each(lambda p, q, u: _dot_tn(jnp.concatenate([p, q], axis=0), u), bh_s, kh_s, uv)

    ys = each(lambda p, z, q: _dot(p, z) + q, rw, zb, yv)
    z_new = each(lambda p, z, q: _dot(p, z) + q, pt, zb, qt)
    for j, sl in enumerate(sls):
        y_ref[0, 0, :, sl] = (ys[j][0:c, :] + ys[j][c:, :]).astype(y_ref.dtype)
        z_scr[j] = z_new[j]


def _wkv_masks():
    i = np.arange(LANES)
    same = (i[:, None] // CHUNK) == (i[None, :] // CHUNK)
    t, s = i[:, None] % CHUNK, i[None, :] % CHUNK
    fwd = np.stack([same & (s < t), same & (s <= t)])
    bwd = np.stack([same & (s > t), same & (s >= t)])
    return jnp.asarray(np.stack([fwd, bwd]).astype(np.float32))


def _wkv_scan(r, a, v, logw, kdir, bdir):
    b, s, w_b = r.shape
    c = CHUNK
    nc = s // c
    gw = _tile(w_b, WKV_GROUP_LANES, LANES)
    npairs = gw // LANES
    chunk_of = lambda d, ci: ci + d * (nc - 1 - 2 * ci)
    shared = pl.BlockSpec((1, c, gw), lambda d, bi, gi, ci: (bi, chunk_of(d, ci), gi))
    perdir = pl.BlockSpec((1, 1, c, gw), lambda d, bi, gi, ci: (d, bi, chunk_of(d, ci), gi))
    kern = functools.partial(_wkv_kernel, npairs=npairs)
    return pl.pallas_call(
        kern,
        grid=(2, b, w_b // gw, nc),
        in_specs=[pl.BlockSpec((1, 2, LANES, LANES), lambda d, bi, gi, ci: (d, 0, 0, 0)),
                  shared, shared, shared, perdir, perdir, perdir],
        out_specs=perdir,
        out_shape=jax.ShapeDtypeStruct((2, b, s, w_b), BF16),
        scratch_shapes=[pltpu.VMEM((npairs, LANES, LANES), F32)],
        compiler_params=_params(("parallel", "parallel", "parallel", "arbitrary"), 32),
        name="wkv_scan",
    )(_wkv_masks(), r, a, v, logw, kdir, bdir)


def _rwkv_out_kernel(y_ref, bonus_ref, z_ref, g_ref, b_ref, o_ref):
    y = y_ref[0, 0].astype(F32) + y_ref[1, 0].astype(F32)
    gmat = _group_matrix(y.shape[1])
    mu = _group_sum(y, gmat) * (1.0 / N_B)
    yc = y - mu
    var = _group_sum(yc * yc, gmat) * (1.0 / N_B)
    yn = yc * lax.rsqrt(var + LNX_EPS) * g_ref[...] + b_ref[...]
    z = z_ref[0].astype(F32)
    o_ref[0] = ((yn + bonus_ref[0]) * (z * _sigmoid(z))).astype(o_ref.dtype)


def _rwkv_out(y2, bonus, p3, z_off, lnx_g, lnx_b):
    _, b, s, w_b = y2.shape
    ts = _tile(s, 512, 16)
    cw = _tile(w_b, 256, LANES)
    assert z_off % cw == 0
    zc = z_off // cw
    vec = pl.BlockSpec((1, cw), lambda bi, si, ji: (0, ji))
    return pl.pallas_call(
        _rwkv_out_kernel,
        grid=(b, s // ts, w_b // cw),
        in_specs=[pl.BlockSpec((2, 1, ts, cw), lambda bi, si, ji: (0, bi, si, ji)),
                  pl.BlockSpec((1, ts, cw), lambda bi, si, ji: (bi, si, ji)),
                  pl.BlockSpec((1, ts, cw), lambda bi, si, ji: (bi, si, zc + ji)),
                  vec, vec],
        out_specs=pl.BlockSpec((1, ts, cw), lambda bi, si, ji: (bi, si, ji)),
        out_shape=jax.ShapeDtypeStruct((b, s, w_b), BF16),
        compiler_params=_params(("parallel", "parallel", "parallel"), 32),
        name="rwkv_out",
    )(y2, bonus, p3, lnx_g.reshape(1, w_b), lnx_b.reshape(1, w_b))


def _merge_kernel(ua_ref, ub_ref, wa_ref, wb_ref, ga_ref, gb_ref, o_ref):
    oa = _dot(ua_ref[...], wa_ref[...])
    ob = _dot(ub_ref[...], wb_ref[...])
    m = _sigmoid(ga_ref[...].astype(F32)) * oa + _sigmoid(gb_ref[...].astype(F32)) * ob
    o_ref[...] = m.astype(o_ref.dtype)


def _merge(ua, ub, w_oa, w_ob, p2, g_off):
    t, w_a = ua.shape
    w_b = ub.shape[1]
    d = w_oa.shape[1]
    tm = _tile(t, 512, 16)
    tn = _tile(d, 512, LANES)
    assert g_off % tn == 0 and d % tn == 0
    gc = g_off // tn
    nd = d // tn
    return pl.pallas_call(
        _merge_kernel,
        grid=(t // tm, nd),
        in_specs=[pl.BlockSpec((tm, w_a), lambda i, j: (i, 0)),
                  pl.BlockSpec((tm, w_b), lambda i, j: (i, 0)),
                  pl.BlockSpec((w_a, tn), lambda i, j: (0, j)),
                  pl.BlockSpec((w_b, tn), lambda i, j: (0, j)),
                  pl.BlockSpec((tm, tn), lambda i, j: (i, gc + j)),
                  pl.BlockSpec((tm, tn), lambda i, j: (i, gc + nd + j))],
        out_specs=pl.BlockSpec((tm, tn), lambda i, j: (i, j)),
        out_shape=jax.ShapeDtypeStruct((t, d), BF16),
        compiler_params=_params(("parallel", "arbitrary"), 48),
        name="gate_merge",
    )(ua, ub, w_oa, w_ob, p2, p2)


def _out_kernel(m_ref, w_ref, x_ref, o_ref):
    o_ref[...] = x_ref[...] + _dot(m_ref[...], w_ref[...])


def _out_norm_kernel(m_ref, w_ref, x_ref, g_ref, o_ref, res_scr, ss_scr):
    j = pl.program_id(1)
    nj, _, tn = res_scr.shape
    res = x_ref[...] + _dot(m_ref[...], w_ref[...])
    res_scr[j] = res

    @pl.when(j == 0)
    def _():
        ss_scr[...] = jnp.zeros(ss_scr.shape, F32)

    ss_scr[...] += jnp.sum(res * res, axis=-1, keepdims=True)

    @pl.when(j == nj - 1)
    def _():
        scale = lax.rsqrt(ss_scr[...] * (1.0 / (nj * tn)) + NORM_EPS)
        for jj in range(nj):
            cols = slice(jj * tn, (jj + 1) * tn)
            o_ref[:, cols] = res_scr[jj] * scale * g_ref[:, cols]


def _out_proj(m, w_out, x2, final_g, final_norm):
    t, d = x2.shape
    tm = _tile(t, 512, 16)
    tn = _tile(d, 512, 2 * LANES)
    in_specs = [pl.BlockSpec((tm, d), lambda i, j: (i, 0)),
                pl.BlockSpec((d, tn), lambda i, j: (0, j)),
                pl.BlockSpec((tm, tn), lambda i, j: (i, j))]
    if not final_norm:
        return pl.pallas_call(
            _out_kernel,
            grid=(t // tm, d // tn),
            in_specs=in_specs,
            out_specs=pl.BlockSpec((tm, tn), lambda i, j: (i, j)),
            out_shape=jax.ShapeDtypeStruct((t, d), F32),
            compiler_params=_params(("parallel", "arbitrary"), 56),
            name="out_proj",
        )(m, w_out, x2)
    return pl.pallas_call(
        _out_norm_kernel,
        grid=(t // tm, d // tn),
        in_specs=in_specs + [pl.BlockSpec((1, d), lambda i, j: (0, 0))],
        out_specs=pl.BlockSpec((tm, d), lambda i, j: (i, 0)),
        out_shape=jax.ShapeDtypeStruct((t, d), F32),
        scratch_shapes=[pltpu.VMEM((d // tn, tm, tn), F32), pltpu.VMEM((tm, 1), F32)],
        compiler_params=_params(("parallel", "arbitrary"), 56),
        name="out_proj_norm",
    )(m, w_out, x2, final_g.reshape(1, d))


def _lambda_init(layer_idx):
    return 0.8 - 0.6 * math.exp(-0.3 * layer_idx)


def _mixer_layer(x, l, prm, final_g, final_norm):
    b, s, d = x.shape
    w_a = prm["w_oA"][l].shape[0]
    w_b = prm["w_oB"][l].shape[0]
    r_lora = prm["w_lora"].shape[2]
    c_shift = 3 * w_b + 4 * r_lora
    n_in = prm["w_in"].shape[2]
    assert n_in == 3 * w_a + c_shift + w_a + w_b + 2 * d
    assert w_a % LANES == 0 and w_b % LANES == 0 and s % CHUNK == 0
    s_off = 3 * w_a
    za_off = s_off + c_shift
    zb_off = za_off + w_a
    g_off = zb_off + w_b
    lam_init = _lambda_init(l)

    x2 = x.reshape(b * s, d)
    h = _rmsnorm_bf16(x2, prm["norm_g"][l])
    p2 = _matmul_bf16(h, prm["w_in"][l].astype(BF16))
    p3 = p2.reshape(b, s, n_in)

    qr, kr, vt = _attn_prepass(p3, w_a)
    ua = _diff_attention(qr, kr, vt, p3, za_off, prm["lam_q1"][l], prm["lam_k1"][l], prm["lam_q2"][l],
                         prm["lam_k2"][l], prm["subln_g"][l], lam_init)

    r, v, a, logw, kdir, bdir, bonus = _rwkv_prep(
        p3, s_off, w_b, r_lora, prm["mu_prev"][l], prm["mu_next"][l], prm["w0"][l], prm["w_lora"][l],
        prm["a0"][l], prm["a_lora"][l], prm["k_k"][l], prm["k_a"][l], prm["r_k"][l])
    y2 = _wkv_scan(r, a, v, logw, kdir, bdir)
    ub = _rwkv_out(y2, bonus, p3, zb_off, prm["lnx_g"][l], prm["lnx_b"][l])

    m = _merge(ua.reshape(b * s, w_a), ub.reshape(b * s, w_b), prm["w_oA"][l].astype(BF16),
               prm["w_oB"][l].astype(BF16), p2, g_off)
    out = _out_proj(m, prm["w_out"][l].astype(BF16), x2, final_g, final_norm)
    return out.reshape(b, s, d)


def kernel(x_prompt, x_sample, norm_g, w_in, mu_prev, mu_next, lam_q1, lam_k1, lam_q2, lam_k2, subln_g, w0,
           w_lora, a0, a_lora, k_k, k_a, r_k, lnx_g, lnx_b, w_oA, w_oB, w_out, final_g):
    prm = dict(norm_g=norm_g, w_in=w_in, mu_prev=mu_prev, mu_next=mu_next, lam_q1=lam_q1, lam_k1=lam_k1,
               lam_q2=lam_q2, lam_k2=lam_k2, subln_g=subln_g, w0=w0, w_lora=w_lora, a0=a0, a_lora=a_lora,
               k_k=k_k, k_a=k_a, r_k=r_k, lnx_g=lnx_g, lnx_b=lnx_b, w_oA=w_oA, w_oB=w_oB, w_out=w_out)
    depth = norm_g.shape[0]

    def trunk(x):
        for l in range(depth):
            x = _mixer_layer(x, l, prm, final_g, final_norm=(l == depth - 1))
        return x

    return (trunk(x_prompt), trunk(x_sample))
```

```python
import functools
import math

import jax
import jax.numpy as jnp
import numpy as np
from jax import lax
from jax.experimental import pallas as pl
from jax.experimental.pallas import tpu as pltpu

F32 = jnp.float32
BF16 = jnp.bfloat16

LANES = 128
DH_A = 64
N_B = 64
CHUNK = 64
ROPE_THETA = 10000.0
ATTN_SCALE = DH_A ** -0.5
LOG2_E = math.log2(math.e)
ATTN_KEY_CHUNK = 512
WKV_GROUP_LANES = 2048
VT_ROWS = LANES + 16
NORM_EPS = 1e-6
SUBLN_EPS = 1e-5
LNX_EPS = 64e-5
MIB = 2 ** 20


def _tile(n, target, align):
    if n <= target:
        return n
    t = (target // align) * align
    while t >= align:
        if n % t == 0:
            return t
        t -= align
    raise ValueError(f"no tile for {n} (target {target}, align {align})")


def _params(semantics, vmem_mib):
    return pltpu.CompilerParams(dimension_semantics=semantics, vmem_limit_bytes=vmem_mib * MIB)


def _sigmoid(x):
    return 1.0 / (1.0 + jnp.exp(-x))


def _dot(a, b):
    return jnp.dot(a, b, preferred_element_type=F32)


def _dot_nt(a, b):
    return lax.dot_general(a, b, (((1,), (1,)), ((), ())), preferred_element_type=F32)


def _dot_tn(a, b):
    return lax.dot_general(a, b, (((0,), (0,)), ((), ())), preferred_element_type=F32)


def _rmsnorm_kernel(x_ref, g_ref, o_ref):
    x = x_ref[...]
    ms = jnp.mean(x * x, axis=-1, keepdims=True)
    o_ref[...] = (x * lax.rsqrt(ms + NORM_EPS) * g_ref[...]).astype(o_ref.dtype)


def _rmsnorm_bf16(x2, g):
    t, d = x2.shape
    tm = _tile(t, 256, 16)
    return pl.pallas_call(
        _rmsnorm_kernel,
        grid=(t // tm,),
        in_specs=[pl.BlockSpec((tm, d), lambda i: (i, 0)), pl.BlockSpec((1, d), lambda i: (0, 0))],
        out_specs=pl.BlockSpec((tm, d), lambda i: (i, 0)),
        out_shape=jax.ShapeDtypeStruct((t, d), BF16),
        compiler_params=_params(("parallel",), 32),
        name="rmsnorm_cast",
    )(x2, g.reshape(1, d))


def _matmul_kernel(a_ref, w_ref, o_ref):
    o_ref[...] = _dot(a_ref[...], w_ref[...]).astype(o_ref.dtype)


def _matmul_bf16(a, w):
    m, k = a.shape
    n = w.shape[1]
    tm = _tile(m, 1024, 16)
    tn = _tile(n, 1024, 2 * LANES)
    return pl.pallas_call(
        _matmul_kernel,
        grid=(m // tm, n // tn),
        in_specs=[pl.BlockSpec((tm, k), lambda i, j: (i, 0)), pl.BlockSpec((k, tn), lambda i, j: (0, j))],
        out_specs=pl.BlockSpec((tm, tn), lambda i, j: (i, j)),
        out_shape=jax.ShapeDtypeStruct((m, n), BF16),
        compiler_params=_params(("parallel", "arbitrary"), 56),
        name="in_proj",
    )(a, w)


def _rope_kernel(q_ref, k_ref, v_ref, cos_ref, sin_ref, qo_ref, ko_ref, vo_ref):
    cos = cos_ref[...]
    sin = sin_ref[...]
    lane = lax.broadcasted_iota(jnp.int32, cos.shape, 1)
    first_half = (lane % DH_A) < (DH_A // 2)

    def rope(x):
        partner = jnp.where(first_half,
                            pltpu.roll(x, LANES - DH_A // 2, axis=1),
                            pltpu.roll(x, DH_A // 2, axis=1))
        return x * cos + partner * sin

    for hh in range(qo_ref.shape[1]):
        cols = slice(hh * LANES, (hh + 1) * LANES)
        qo_ref[0, hh] = (rope(q_ref[0, :, cols].astype(F32)) * (ATTN_SCALE * LOG2_E)).astype(qo_ref.dtype)
        ko_ref[0, hh] = rope(k_ref[0, :, cols].astype(F32)).astype(ko_ref.dtype)
        vo_ref[0, hh, 0, 0:LANES, :] = v_ref[0, :, cols].astype(F32).T.astype(vo_ref.dtype)
        vo_ref[0, hh, 0, LANES:, :] = jnp.ones((vo_ref.shape[3] - LANES, vo_ref.shape[4]), vo_ref.dtype)


def _rope_tables(s):
    half = DH_A // 2
    inv = 1.0 / (ROPE_THETA ** (jnp.arange(0, DH_A, 2, dtype=F32) / DH_A))
    ang = jnp.arange(s, dtype=F32)[:, None] * inv[None, :]
    cos, sin = jnp.cos(ang), jnp.sin(ang)
    reps = LANES // half
    return jnp.tile(cos, (1, reps)), jnp.tile(jnp.concatenate([-sin, sin], axis=-1), (1, reps // 2))


def _attn_prepass(p3, w_a):
    b, s, _ = p3.shape
    h = w_a // LANES
    ts = _tile(s // 2, ATTN_KEY_CHUNK, LANES)
    cos, sin = _rope_tables(s)
    head_out = jax.ShapeDtypeStruct((b, h, s, LANES), BF16)
    vt_out = jax.ShapeDtypeStruct((b, h, s // ts, VT_ROWS, ts), BF16)
    hp = _tile(h, 4, 1)
    hg = h // hp
    col = lambda off: pl.BlockSpec((1, ts, hp * LANES), lambda bi, si, hi: (bi, si, off + hi))
    tab = pl.BlockSpec((ts, LANES), lambda bi, si, hi: (si, 0))
    out = pl.BlockSpec((1, hp, ts, LANES), lambda bi, si, hi: (bi, hi, si, 0))
    out_t = pl.BlockSpec((1, hp, 1, VT_ROWS, ts), lambda bi, si, hi: (bi, hi, si, 0, 0))
    return pl.pallas_call(
        _rope_kernel,
        grid=(b, s // ts, hg),
        in_specs=[col(0), col(hg), col(2 * hg), tab, tab],
        out_specs=[out, out, out_t],
        out_shape=[head_out, head_out, vt_out],
        compiler_params=_params(("parallel", "parallel", "arbitrary"), 32),
        name="attn_prepass",
    )(p3, p3, p3, cos, sin)


def _attn_kernel(q_ref, k_ref, vt_ref, z_ref, lq1_ref, lk1_ref, lq2_ref, lk2_ref, g_ref, o_ref,
                 qq_scr, acc_scr, sa_scr, sb_scr, *, qt, lam_init):
    tq = q_ref.shape[2]
    nk = vt_ref.shape[2]
    tk = vt_ref.shape[4]
    tiles = [slice(c * qt, (c + 1) * qt) for c in range(2 * tq // qt)]
    q = q_ref[0, 0].astype(F32)
    lane = lax.broadcasted_iota(jnp.int32, q.shape, 1)
    qq_scr[0:tq, :] = jnp.where(lane < DH_A, q, 0.0).astype(BF16)
    qq_scr[tq:2 * tq, :] = jnp.where(lane >= DH_A, q, 0.0).astype(BF16)
    acc_scr[...] = jnp.zeros(acc_scr.shape, F32)

    def produce(i, s_scr):
        off = pl.multiple_of(i * tk, tk)
        ks = k_ref[0, 0, pl.ds(off, tk), :]
        s = [_dot_nt(ks, qq_scr[t, :]) for t in tiles]
        for t, x in zip(tiles, s):
            s_scr[:, t] = x
        return jnp.concatenate([jnp.max(x, axis=0, keepdims=True) for x in s], axis=1)

    def consume(i, s_scr, cmax, m_old):
        vt = vt_ref[0, 0, i]
        m_new = jnp.maximum(m_old, cmax)
        alpha = jnp.exp2(m_old - m_new)
        p = [jnp.exp2(s_scr[:, t] - m_new[:, t]).astype(BF16) for t in tiles]
        pv = [_dot(vt, x) for x in p]
        for t, x in zip(tiles, pv):
            acc_scr[:, t] = alpha[:, t] * acc_scr[:, t] + x
        return m_new

    def pair(j, carry):
        m, cmax_a = carry
        cmax_b = produce(2 * j + 1, sb_scr)
        m = consume(2 * j, sa_scr, cmax_a, m)
        cmax_a = produce(2 * j + 2, sa_scr)
        m = consume(2 * j + 1, sb_scr, cmax_b, m)
        return m, cmax_a

    m0 = jnp.full((1, 2 * tq), -jnp.inf, F32)
    trips = nk // 2 - 1
    unroll = next(u for u in (3, 2, 1) if trips % u == 0 and (trips // u >= 2 or u == 1))
    m, cmax_a = lax.fori_loop(0, trips, pair, (m0, produce(0, sa_scr)), unroll=unroll)
    cmax_b = produce(nk - 1, sb_scr)
    m = consume(nk - 2, sa_scr, cmax_a, m)
    consume(nk - 1, sb_scr, cmax_b, m)

    lam = (jnp.exp(jnp.sum(lq1_ref[...] * lk1_ref[...], keepdims=True))
           - jnp.exp(jnp.sum(lq2_ref[...] * lk2_ref[...], keepdims=True)) + lam_init)
    ot = acc_scr[0:LANES, :] / acc_scr[LANES:LANES + 1, :]
    o = (ot[:, 0:tq] - lam * ot[:, tq:2 * tq]).T
    y = o * lax.rsqrt(jnp.mean(o * o, axis=-1, keepdims=True) + SUBLN_EPS) * g_ref[...] * (1.0 - lam_init)
    z = z_ref[0].astype(F32)
    o_ref[0] = (y * (z * _sigmoid(z))).astype(o_ref.dtype)


def _diff_attention(qr, kr, vt, p3, z_off, lam_q1, lam_k1, lam_q2, lam_k2, subln_g, lam_init):
    b, h, s, _ = qr.shape
    nk, tk = vt.shape[2], vt.shape[4]
    tq = _tile(s, 512, LANES)
    qt = _tile(2 * tq, 256, LANES)
    assert z_off % LANES == 0 and nk % 2 == 0
    zc = z_off // LANES
    vec = lambda n: pl.BlockSpec((1, n), lambda bi, hi, qi: (0, 0))
    kern = functools.partial(_attn_kernel, qt=qt, lam_init=lam_init)
    return pl.pallas_call(
        kern,
        grid=(b, h, s // tq),
        in_specs=[
            pl.BlockSpec((1, 1, tq, LANES), lambda bi, hi, qi: (bi, hi, qi, 0)),
            pl.BlockSpec((1, 1, s, LANES), lambda bi, hi, qi: (bi, hi, 0, 0)),
            pl.BlockSpec((1, 1, nk, VT_ROWS, tk), lambda bi, hi, qi: (bi, hi, 0, 0, 0)),
            pl.BlockSpec((1, tq, LANES), lambda bi, hi, qi: (bi, qi, zc + hi)),
            vec(DH_A), vec(DH_A), vec(DH_A), vec(DH_A), vec(LANES),
        ],
        out_specs=pl.BlockSpec((1, tq, LANES), lambda bi, hi, qi: (bi, qi, hi)),
        out_shape=jax.ShapeDtypeStruct((b, s, h * LANES), BF16),
        scratch_shapes=[
            pltpu.VMEM((2 * tq, LANES), BF16),
            pltpu.VMEM((VT_ROWS, 2 * tq), F32),
            pltpu.VMEM((tk, 2 * tq), F32),
            pltpu.VMEM((tk, 2 * tq), F32),
        ],
        compiler_params=_params(("parallel", "parallel", "arbitrary"), 48),
        name="diff_attn",
    )(qr, kr, vt, p3, lam_q1.reshape(1, DH_A), lam_k1.reshape(1, DH_A), lam_q2.reshape(1, DH_A),
      lam_k2.reshape(1, DH_A), subln_g.reshape(1, LANES))


def _group_sum(x, gmat):
    hi = x.astype(BF16)
    lo = (x - hi.astype(F32)).astype(BF16)
    return _dot(hi, gmat) + _dot(lo, gmat)


def _group_matrix(cw):
    r = lax.broadcasted_iota(jnp.int32, (cw, cw), 0) // N_B
    c = lax.broadcasted_iota(jnp.int32, (cw, cw), 1) // N_B
    return jnp.where(r == c, 1.0, 0.0).astype(BF16)


def _rwkv_prep_kernel(r_ref, k_ref, v_ref, lo_ref, rp_ref, kp_ref, vp_ref, lop_ref, rn_ref, kn_ref, vn_ref,
                      lon_ref, mu_ref, mulo_ref, w0_ref, a0_ref, wl_ref, al_ref, kk_ref, ka_ref, rk_ref,
                      ro_ref, vo_ref, ao_ref, lw_ref, kd_ref, bd_ref, bonus_ref, pw_scr, pa_scr):
    ts = r_ref.shape[1]
    si = pl.program_id(1)
    first = si == 0
    last = si == pl.num_programs(1) - 1

    def shift(cur_ref, prev_ref, next_ref, mu_p, mu_n):
        x = cur_ref[0].astype(F32)
        hp = prev_ref[0].astype(F32)
        hn = next_ref[0].astype(F32)
        pr = jnp.where(first, 0.0, hp[hp.shape[0] - 1:hp.shape[0], :])
        nx = jnp.where(last, 0.0, hn[0:1, :])
        row = lax.broadcasted_iota(jnp.int32, x.shape, 0)
        prev = jnp.where(row == 0, pr, pltpu.roll(x, 1, axis=0))
        nxt = jnp.where(row == ts - 1, nx, pltpu.roll(x, ts - 1, axis=0))
        return x + mu_p * (prev - x) + mu_n * (nxt - x)

    r = shift(r_ref, rp_ref, rn_ref, mu_ref[0, 0:1, :], mu_ref[1, 0:1, :])
    k = shift(k_ref, kp_ref, kn_ref, mu_ref[0, 1:2, :], mu_ref[1, 1:2, :])
    v = shift(v_ref, vp_ref, vn_ref, mu_ref[0, 2:3, :], mu_ref[1, 2:3, :])

    @pl.when(pl.program_id(2) == 0)
    def _():
        lo = shift(lo_ref, lop_ref, lon_ref, mulo_ref[0:1, :], mulo_ref[1:2, :])
        rl = lo.shape[1] // 4
        for d in range(2):
            pw_scr[d] = jnp.tanh(lo[:, d * rl:(d + 1) * rl]).astype(BF16)
            pa_scr[d] = lo[:, (2 + d) * rl:(3 + d) * rl].astype(BF16)

    cw = r.shape[1]
    gmat = _group_matrix(cw)
    kkh = k * kk_ref[...]
    nrm = jnp.sqrt(_group_sum(kkh * kkh, gmat))
    kk = kkh / jnp.maximum(nrm, 1e-12)
    ka = ka_ref[...]
    ro_ref[0] = r.astype(ro_ref.dtype)
    vo_ref[0] = v.astype(vo_ref.dtype)
    ao_ref[0] = (-kk).astype(ao_ref.dtype)
    ksum = jnp.zeros_like(k)
    for d in range(2):
        pw = pw_scr[d]
        pa = pa_scr[d]
        wl = w0_ref[d:d + 1, :] + _dot(pw, wl_ref[d])
        lw_ref[d, 0] = -math.exp(-0.5) * _sigmoid(wl)
        a = _sigmoid(a0_ref[d:d + 1, :] + _dot(pa, al_ref[d]))
        kdir = k * (1.0 + (a - 1.0) * ka)
        kd_ref[d, 0] = kdir.astype(kd_ref.dtype)
        bd_ref[d, 0] = (kk * a).astype(bd_ref.dtype)
        ksum = ksum + kdir
    bonus_ref[0] = _group_sum(r * ksum * rk_ref[...], gmat) * v


def _rwkv_prep(p3, s_off, w_b, r_lora, mu_prev, mu_next, w0, w_lora, a0, a_lora, k_k, k_a, r_k):
    b, s, _ = p3.shape
    ts = _tile(s, 512, 16)
    cw = _tile(w_b, 256, LANES)
    lw = 4 * r_lora
    halo = 16
    nblk = s // halo
    assert s_off % cw == 0 and w_b % cw == 0 and (s_off + 3 * w_b) % lw == 0 and ts % halo == 0
    rc, kc, vc, lc = s_off // cw, (s_off + w_b) // cw, (s_off + 2 * w_b) // cw, (s_off + 3 * w_b) // lw
    per = ts // halo

    cur = lambda off: pl.BlockSpec((1, ts, cw), lambda bi, si, ji: (bi, si, off + ji))
    prv = lambda off: pl.BlockSpec((1, halo, cw), lambda bi, si, ji: (bi, jnp.maximum(si * per - 1, 0), off + ji))
    nxt = lambda off: pl.BlockSpec((1, halo, cw),
                                   lambda bi, si, ji: (bi, jnp.minimum((si + 1) * per, nblk - 1), off + ji))
    lo_cur = pl.BlockSpec((1, ts, lw), lambda bi, si, ji: (bi, si, lc))
    lo_prv = pl.BlockSpec((1, halo, lw), lambda bi, si, ji: (bi, jnp.maximum(si * per - 1, 0), lc))
    lo_nxt = pl.BlockSpec((1, halo, lw), lambda bi, si, ji: (bi, jnp.minimum((si + 1) * per, nblk - 1), lc))
    colvec = lambda rows: pl.BlockSpec((rows, cw), lambda bi, si, ji: (0, ji))

    mu_rkv = jnp.stack([mu_prev[:3 * w_b].reshape(3, w_b), mu_next[:3 * w_b].reshape(3, w_b)])
    mu_lo = jnp.stack([mu_prev[3 * w_b:], mu_next[3 * w_b:]])
    tok = pl.BlockSpec((1, ts, cw), lambda bi, si, ji: (bi, si, ji))
    tok2 = pl.BlockSpec((2, 1, ts, cw), lambda bi, si, ji: (0, bi, si, ji))
    shp = lambda dt: jax.ShapeDtypeStruct((b, s, w_b), dt)
    shp2 = lambda dt: jax.ShapeDtypeStruct((2, b, s, w_b), dt)
    return pl.pallas_call(
        _rwkv_prep_kernel,
        grid=(b, s // ts, w_b // cw),
        in_specs=[cur(rc), cur(kc), cur(vc), lo_cur, prv(rc), prv(kc), prv(vc), lo_prv,
                  nxt(rc), nxt(kc), nxt(vc), lo_nxt,
                  pl.BlockSpec((2, 3, cw), lambda bi, si, ji: (0, 0, ji)),
                  pl.BlockSpec((2, lw), lambda bi, si, ji: (0, 0)),
                  colvec(2), colvec(2),
                  pl.BlockSpec((2, r_lora, cw), lambda bi, si, ji: (0, 0, ji)),
                  pl.BlockSpec((2, r_lora, cw), lambda bi, si, ji: (0, 0, ji)),
                  colvec(1), colvec(1), colvec(1)],
        out_specs=[tok, tok, tok, tok2, tok2, tok2, tok],
        out_shape=[shp(BF16), shp(BF16), shp(BF16), shp2(F32), shp2(BF16), shp2(BF16), shp(F32)],
        scratch_shapes=[pltpu.VMEM((2, ts, r_lora), BF16), pltpu.VMEM((2, ts, r_lora), BF16)],
        compiler_params=_params(("parallel", "parallel", "arbitrary"), 48),
        name="rwkv_prep",
    )(p3, p3, p3, p3, p3, p3, p3, p3, p3, p3, p3, p3, mu_rkv, mu_lo, w0, a0,
      w_lora.astype(BF16), a_lora.astype(BF16), k_k.reshape(1, w_b), k_a.reshape(1, w_b), r_k.reshape(1, w_b))


def _wkv_kernel(mask_ref, r_ref, a_ref, v_ref, lw_ref, k_ref, b_ref, y_ref, z_scr, *, npairs):
    c = CHUNK
    ci = pl.program_id(3)

    @pl.when(ci == 0)
    def _():
        z_scr[...] = jnp.zeros(z_scr.shape, F32)

    strict = mask_ref[0, 0]
    incl = mask_ref[0, 1]
    cum = incl[0:c, 0:c].astype(BF16)
    lane = lax.broadcasted_iota(jnp.int32, (c, LANES), 1)
    low = lane < N_B
    row2 = lax.broadcasted_iota(jnp.int32, (LANES, LANES), 0)
    col2 = lax.broadcasted_iota(jnp.int32, (LANES, LANES), 1)
    eye = row2 == col2
    own = (row2 < c) == (col2 < N_B)

    def stack(x):
        return jnp.concatenate([jnp.where(low, x, 0.0), jnp.where(low, 0.0, x)], axis=0)

    def each(f, *cols):
        return [f(*xs) for xs in zip(*cols)]

    sls = [slice(j * LANES, (j + 1) * LANES) for j in range(npairs)]
    lw = [lw_ref[0, 0, :, sl] for sl in sls]
    r = [r_ref[0, :, sl].astype(F32) for sl in sls]
    a = [a_ref[0, :, sl].astype(F32) for sl in sls]
    v = [v_ref[0, :, sl].astype(F32) for sl in sls]
    k = [k_ref[0, 0, :, sl].astype(F32) for sl in sls]
    b = [b_ref[0, 0, :, sl].astype(F32) for sl in sls]
    zb = [z_scr[j].astype(BF16) for j in range(npairs)]

    hi = each(lambda x: x.astype(BF16), lw)
    lo = each(lambda x, h: (x - h.astype(F32)).astype(BF16), lw, hi)
    cum2 = jnp.concatenate([cum, cum], axis=1)
    lam = each(lambda h, l: _dot(cum2, jnp.concatenate([h, l], axis=0)), hi, lo)
    tot = each(lambda x: jnp.sum(x, axis=0, keepdims=True), lw)
    gdec = each(jnp.exp, tot)
    a_s = each(lambda x, lm, w: stack(x * jnp.exp(lm - w)), a, lam, lw)
    r_s = each(lambda x, lm: stack(x * jnp.exp(lm)), r, lam)
    e_neg = each(lambda lm: jnp.exp(-lm), lam)
    e_rem = each(lambda t, lm: jnp.exp(t - lm), tot, lam)
    b_s = each(lambda x, e: stack(x * e).astype(BF16), b, e_neg)
    k_s = each(lambda x, e: stack(x * e).astype(BF16), k, e_neg)
    bh_s = each(lambda x, e: stack(x * e).astype(BF16), b, e_rem)
    kh_s = each(lambda x, e: stack(x * e).astype(BF16), k, e_rem)
    v_s = each(lambda x: stack(x).astype(BF16), v)

    m4 = each(lambda x, y, p, q: _dot_nt(jnp.concatenate([x, y], axis=0).astype(BF16),
                                         jnp.concatenate([p, q], axis=0)), a_s, r_s, b_s, k_s)
    lp = each(lambda m: (m[0:LANES, 0:LANES] * strict).astype(BF16), m4)
    lak = each(lambda m: (m[0:LANES, LANES:] * strict).astype(BF16), m4)
    mrbk = each(lambda m: jnp.concatenate([m[LANES:, 0:LANES] * incl, m[LANES:, LANES:] * incl],
                                          axis=1).astype(BF16), m4)

    half = LANES // 2
    x = each(lambda p, q, w: p + pltpu.roll(_dot(q, w), half, axis=1), a_s, lak, v_s)
    steps = int(math.log2(c))
    for it in range(steps - 1):
        res = each(lambda p, q: _dot(p, jnp.concatenate([q.astype(BF16), p], axis=1)), lp, x)
        x = each(lambda q, s: q + s[:, 0:LANES], x, res)
        lp = each(lambda s: s[:, LANES:].astype(BF16), res)
    x = each(lambda p, q: q + _dot(p, q.astype(BF16)), lp, x)
    w_s = each(lambda q: jnp.where(own, q, 0.0).astype(BF16), x)
    u_s = each(lambda q: jnp.where(own, pltpu.roll(q, half, axis=1), 0.0).astype(BF16), x)

    zero = jnp.zeros((LANES, LANES), BF16)
    wuv = each(lambda w, u, vv: jnp.concatenate([jnp.concatenate([w, u], axis=1),
                                                 jnp.concatenate([zero, vv], axis=1)], axis=0), w_s, u_s, v_s)
    ry = each(_dot, mrbk, wuv)
    pq = each(lambda p, q, m: _dot_tn(jnp.concatenate([p, q], axis=0), m), bh_s, kh_s, wuv)
    rw = each(lambda p, s: (p + s[:, 0:LANES]).astype(BF16), r_s, ry)
    pt = each(lambda g, s: (jnp.where(eye, g, 0.0) + s[:, 0:LANES]).astype(BF16), gdec, pq)

    ys = each(lambda p, z, s: _dot(p, z) + s[:, LANES:], rw, zb, ry)
    z_new = each(lambda p, z, s: _dot(p, z) + s[:, LANES:], pt, zb, pq)
    for j, sl in enumerate(sls):
        y_ref[0, 0, :, sl] = (ys[j][0:c, :] + ys[j][c:, :]).astype(y_ref.dtype)
        z_scr[j] = z_new[j]


def _wkv_masks():
    i = np.arange(LANES)
    same = (i[:, None] // CHUNK) == (i[None, :] // CHUNK)
    t, s = i[:, None] % CHUNK, i[None, :] % CHUNK
    fwd = np.stack([same & (s < t), same & (s <= t)])
    bwd = np.stack([same & (s > t), same & (s >= t)])
    return jnp.asarray(np.stack([fwd, bwd]).astype(np.float32))


def _wkv_scan(r, a, v, logw, kdir, bdir):
    b, s, w_b = r.shape
    c = CHUNK
    nc = s // c
    gw = _tile(w_b, WKV_GROUP_LANES, LANES)
    npairs = gw // LANES
    chunk_of = lambda d, ci: ci + d * (nc - 1 - 2 * ci)
    shared = pl.BlockSpec((1, c, gw), lambda d, bi, gi, ci: (bi, chunk_of(d, ci), gi))
    perdir = pl.BlockSpec((1, 1, c, gw), lambda d, bi, gi, ci: (d, bi, chunk_of(d, ci), gi))
    kern = functools.partial(_wkv_kernel, npairs=npairs)
    return pl.pallas_call(
        kern,
        grid=(2, b, w_b // gw, nc),
        in_specs=[pl.BlockSpec((1, 2, LANES, LANES), lambda d, bi, gi, ci: (d, 0, 0, 0)),
                  shared, shared, shared, perdir, perdir, perdir],
        out_specs=perdir,
        out_shape=jax.ShapeDtypeStruct((2, b, s, w_b), BF16),
        scratch_shapes=[pltpu.VMEM((npairs, LANES, LANES), F32)],
        compiler_params=_params(("parallel", "parallel", "parallel", "arbitrary"), 32),
        name="wkv_scan",
    )(_wkv_masks(), r, a, v, logw, kdir, bdir)


def _rwkv_out_kernel(y_ref, bonus_ref, z_ref, g_ref, b_ref, o_ref):
    y = y_ref[0, 0].astype(F32) + y_ref[1, 0].astype(F32)
    gmat = _group_matrix(y.shape[1])
    mu = _group_sum(y, gmat) * (1.0 / N_B)
    yc = y - mu
    var = _group_sum(yc * yc, gmat) * (1.0 / N_B)
    yn = yc * lax.rsqrt(var + LNX_EPS) * g_ref[...] + b_ref[...]
    z = z_ref[0].astype(F32)
    o_ref[0] = ((yn + bonus_ref[0]) * (z * _sigmoid(z))).astype(o_ref.dtype)


def _rwkv_out(y2, bonus, p3, z_off, lnx_g, lnx_b):
    _, b, s, w_b = y2.shape
    ts = _tile(s, 512, 16)
    cw = _tile(w_b, 256, LANES)
    assert z_off % cw == 0
    zc = z_off // cw
    vec = pl.BlockSpec((1, cw), lambda bi, si, ji: (0, ji))
    return pl.pallas_call(
        _rwkv_out_kernel,
        grid=(b, s // ts, w_b // cw),
        in_specs=[pl.BlockSpec((2, 1, ts, cw), lambda bi, si, ji: (0, bi, si, ji)),
                  pl.BlockSpec((1, ts, cw), lambda bi, si, ji: (bi, si, ji)),
                  pl.BlockSpec((1, ts, cw), lambda bi, si, ji: (bi, si, zc + ji)),
                  vec, vec],
        out_specs=pl.BlockSpec((1, ts, cw), lambda bi, si, ji: (bi, si, ji)),
        out_shape=jax.ShapeDtypeStruct((b, s, w_b), BF16),
        compiler_params=_params(("parallel", "parallel", "parallel"), 32),
        name="rwkv_out",
    )(y2, bonus, p3, lnx_g.reshape(1, w_b), lnx_b.reshape(1, w_b))


def _merge_kernel(ua_ref, ub_ref, wa_ref, wb_ref, ga_ref, gb_ref, o_ref):
    oa = _dot(ua_ref[...], wa_ref[...])
    ob = _dot(ub_ref[...], wb_ref[...])
    m = _sigmoid(ga_ref[...].astype(F32)) * oa + _sigmoid(gb_ref[...].astype(F32)) * ob
    o_ref[...] = m.astype(o_ref.dtype)


def _merge(ua, ub, w_oa, w_ob, p2, g_off):
    t, w_a = ua.shape
    w_b = ub.shape[1]
    d = w_oa.shape[1]
    tm = _tile(t, 512, 16)
    tn = _tile(d, 512, LANES)
    assert g_off % tn == 0 and d % tn == 0
    gc = g_off // tn
    nd = d // tn
    return pl.pallas_call(
        _merge_kernel,
        grid=(t // tm, nd),
        in_specs=[pl.BlockSpec((tm, w_a), lambda i, j: (i, 0)),
                  pl.BlockSpec((tm, w_b), lambda i, j: (i, 0)),
                  pl.BlockSpec((w_a, tn), lambda i, j: (0, j)),
                  pl.BlockSpec((w_b, tn), lambda i, j: (0, j)),
                  pl.BlockSpec((tm, tn), lambda i, j: (i, gc + j)),
                  pl.BlockSpec((tm, tn), lambda i, j: (i, gc + nd + j))],
        out_specs=pl.BlockSpec((tm, tn), lambda i, j: (i, j)),
        out_shape=jax.ShapeDtypeStruct((t, d), BF16),
        compiler_params=_params(("parallel", "arbitrary"), 48),
        name="gate_merge",
    )(ua, ub, w_oa, w_ob, p2, p2)


def _out_kernel(m_ref, w_ref, x_ref, o_ref):
    o_ref[...] = x_ref[...] + _dot(m_ref[...], w_ref[...])


def _out_norm_kernel(m_ref, w_ref, x_ref, g_ref, o_ref, res_scr, ss_scr):
    j = pl.program_id(1)
    nj, _, tn = res_scr.shape
    res = x_ref[...] + _dot(m_ref[...], w_ref[...])
    res_scr[j] = res

    @pl.when(j == 0)
    def _():
        ss_scr[...] = jnp.zeros(ss_scr.shape, F32)

    ss_scr[...] += jnp.sum(res * res, axis=-1, keepdims=True)

    @pl.when(j == nj - 1)
    def _():
        scale = lax.rsqrt(ss_scr[...] * (1.0 / (nj * tn)) + NORM_EPS)
        for jj in range(nj):
            cols = slice(jj * tn, (jj + 1) * tn)
            o_ref[:, cols] = res_scr[jj] * scale * g_ref[:, cols]


def _out_proj(m, w_out, x2, final_g, final_norm):
    t, d = x2.shape
    tm = _tile(t, 512, 16)
    tn = _tile(d, 512, 2 * LANES)
    in_specs = [pl.BlockSpec((tm, d), lambda i, j: (i, 0)),
                pl.BlockSpec((d, tn), lambda i, j: (0, j)),
                pl.BlockSpec((tm, tn), lambda i, j: (i, j))]
    if not final_norm:
        return pl.pallas_call(
            _out_kernel,
            grid=(t // tm, d // tn),
            in_specs=in_specs,
            out_specs=pl.BlockSpec((tm, tn), lambda i, j: (i, j)),
            out_shape=jax.ShapeDtypeStruct((t, d), F32),
            compiler_params=_params(("parallel", "arbitrary"), 56),
            name="out_proj",
        )(m, w_out, x2)
    return pl.pallas_call(
        _out_norm_kernel,
        grid=(t // tm, d // tn),
        in_specs=in_specs + [pl.BlockSpec((1, d), lambda i, j: (0, 0))],
        out_specs=pl.BlockSpec((tm, d), lambda i, j: (i, 0)),
        out_shape=jax.ShapeDtypeStruct((t, d), F32),
        scratch_shapes=[pltpu.VMEM((d // tn, tm, tn), F32), pltpu.VMEM((tm, 1), F32)],
        compiler_params=_params(("parallel", "arbitrary"), 56),
        name="out_proj_norm",
    )(m, w_out, x2, final_g.reshape(1, d))


def _lambda_init(layer_idx):
    return 0.8 - 0.6 * math.exp(-0.3 * layer_idx)


def _mixer_layer(x, l, prm, final_g, final_norm):
    b, s, d = x.shape
    w_a = prm["w_oA"][l].shape[0]
    w_b = prm["w_oB"][l].shape[0]
    r_lora = prm["w_lora"].shape[2]
    c_shift = 3 * w_b + 4 * r_lora
    n_in = prm["w_in"].shape[2]
    assert n_in == 3 * w_a + c_shift + w_a + w_b + 2 * d
    assert w_a % LANES == 0 and w_b % LANES == 0 and s % CHUNK == 0
    s_off = 3 * w_a
    za_off = s_off + c_shift
    zb_off = za_off + w_a
    g_off = zb_off + w_b
    lam_init = _lambda_init(l)

    x2 = x.reshape(b * s, d)
    h = _rmsnorm_bf16(x2, prm["norm_g"][l])
    p2 = _matmul_bf16(h, prm["w_in"][l].astype(BF16))
    p3 = p2.reshape(b, s, n_in)

    qr, kr, vt = _attn_prepass(p3, w_a)
    ua = _diff_attention(qr, kr, vt, p3, za_off, prm["lam_q1"][l], prm["lam_k1"][l], prm["lam_q2"][l],
                         prm["lam_k2"][l], prm["subln_g"][l], lam_init)

    r, v, a, logw, kdir, bdir, bonus = _rwkv_prep(
        p3, s_off, w_b, r_lora, prm["mu_prev"][l], prm["mu_next"][l], prm["w0"][l], prm["w_lora"][l],
        prm["a0"][l], prm["a_lora"][l], prm["k_k"][l], prm["k_a"][l], prm["r_k"][l])
    y2 = _wkv_scan(r, a, v, logw, kdir, bdir)
    ub = _rwkv_out(y2, bonus, p3, zb_off, prm["lnx_g"][l], prm["lnx_b"][l])

    m = _merge(ua.reshape(b * s, w_a), ub.reshape(b * s, w_b), prm["w_oA"][l].astype(BF16),
               prm["w_oB"][l].astype(BF16), p2, g_off)
    out = _out_proj(m, prm["w_out"][l].astype(BF16), x2, final_g, final_norm)
    return out.reshape(b, s, d)


def kernel(x_prompt, x_sample, norm_g, w_in, mu_prev, mu_next, lam_q1, lam_k1, lam_q2, lam_k2, subln_g, w0,
           w_lora, a0, a_lora, k_k, k_a, r_k, lnx_g, lnx_b, w_oA, w_oB, w_out, final_g):
    prm = dict(norm_g=norm_g, w_in=w_in, mu_prev=mu_prev, mu_next=mu_next, lam_q1=lam_q1, lam_k1=lam_k1,
               lam_q2=lam_q2, lam_k2=lam_k2, subln_g=subln_g, w0=w0, w_lora=w_lora, a0=a0, a_lora=a_lora,
               k_k=k_k, k_a=k_a, r_k=r_k, lnx_g=lnx_g, lnx_b=lnx_b, w_oA=w_oA, w_oB=w_oB, w_out=w_out)
    depth = norm_g.shape[0]

    def trunk(x):
        for l in range(depth):
            x = _mixer_layer(x, l, prm, final_g, final_norm=(l == depth - 1))
        return x

    return (trunk(x_prompt), trunk(x_sample))
```

```python
import functools
import math

import jax
import jax.numpy as jnp
import numpy as np
from jax import lax
from jax.experimental import pallas as pl
from jax.experimental.pallas import tpu as pltpu

F32 = jnp.float32
BF16 = jnp.bfloat16

LANES = 128
DH_A = 64
N_B = 64
CHUNK = 64
ROPE_THETA = 10000.0
ATTN_SCALE = DH_A ** -0.5
LOG2_E = math.log2(math.e)
ATTN_KEY_CHUNK = 512
WKV_GROUP_LANES = 2048
VT_ROWS = LANES + 16
NORM_EPS = 1e-6
SUBLN_EPS = 1e-5
LNX_EPS = 64e-5
MIB = 2 ** 20


def _tile(n, target, align):
    if n <= target:
        return n
    t = (target // align) * align
    while t >= align:
        if n % t == 0:
            return t
        t -= align
    raise ValueError(f"no tile for {n} (target {target}, align {align})")


def _params(semantics, vmem_mib):
    return pltpu.CompilerParams(dimension_semantics=semantics, vmem_limit_bytes=vmem_mib * MIB)


def _sigmoid(x):
    return 1.0 / (1.0 + jnp.exp(-x))


def _dot(a, b):
    return jnp.dot(a, b, preferred_element_type=F32)


def _dot_nt(a, b):
    return lax.dot_general(a, b, (((1,), (1,)), ((), ())), preferred_element_type=F32)


def _dot_tn(a, b):
    return lax.dot_general(a, b, (((0,), (0,)), ((), ())), preferred_element_type=F32)


def _rmsnorm_kernel(x_ref, g_ref, o_ref):
    x = x_ref[...]
    ms = jnp.mean(x * x, axis=-1, keepdims=True)
    o_ref[...] = (x * lax.rsqrt(ms + NORM_EPS) * g_ref[...]).astype(o_ref.dtype)


def _rmsnorm_bf16(x2, g):
    t, d = x2.shape
    tm = _tile(t, 256, 16)
    return pl.pallas_call(
        _rmsnorm_kernel,
        grid=(t // tm,),
        in_specs=[pl.BlockSpec((tm, d), lambda i: (i, 0)), pl.BlockSpec((1, d), lambda i: (0, 0))],
        out_specs=pl.BlockSpec((tm, d), lambda i: (i, 0)),
        out_shape=jax.ShapeDtypeStruct((t, d), BF16),
        compiler_params=_params(("parallel",), 32),
        name="rmsnorm_cast",
    )(x2, g.reshape(1, d))


def _matmul_kernel(a_ref, w_ref, o_ref):
    o_ref[...] = _dot(a_ref[...], w_ref[...]).astype(o_ref.dtype)


def _matmul_bf16(a, w):
    m, k = a.shape
    n = w.shape[1]
    tm = _tile(m, 1024, 16)
    tn = _tile(n, 1024, 2 * LANES)
    return pl.pallas_call(
        _matmul_kernel,
        grid=(m // tm, n // tn),
        in_specs=[pl.BlockSpec((tm, k), lambda i, j: (i, 0)), pl.BlockSpec((k, tn), lambda i, j: (0, j))],
        out_specs=pl.BlockSpec((tm, tn), lambda i, j: (i, j)),
        out_shape=jax.ShapeDtypeStruct((m, n), BF16),
        compiler_params=_params(("parallel", "arbitrary"), 56),
        name="in_proj",
    )(a, w)


def _rope_kernel(q_ref, k_ref, v_ref, cos_ref, sin_ref, qo_ref, ko_ref, vo_ref):
    cos = cos_ref[...]
    sin = sin_ref[...]
    lane = lax.broadcasted_iota(jnp.int32, cos.shape, 1)
    first_half = (lane % DH_A) < (DH_A // 2)

    def rope(x):
        partner = jnp.where(first_half,
                            pltpu.roll(x, LANES - DH_A // 2, axis=1),
                            pltpu.roll(x, DH_A // 2, axis=1))
        return x * cos + partner * sin

    for hh in range(qo_ref.shape[1]):
        cols = slice(hh * LANES, (hh + 1) * LANES)
        qo_ref[0, hh] = (rope(q_ref[0, :, cols].astype(F32)) * (ATTN_SCALE * LOG2_E)).astype(qo_ref.dtype)
        ko_ref[0, hh] = rope(k_ref[0, :, cols].astype(F32)).astype(ko_ref.dtype)
        vo_ref[0, hh, 0, 0:LANES, :] = v_ref[0, :, cols].astype(F32).T.astype(vo_ref.dtype)
        vo_ref[0, hh, 0, LANES:, :] = jnp.ones((vo_ref.shape[3] - LANES, vo_ref.shape[4]), vo_ref.dtype)


def _rope_tables(s):
    half = DH_A // 2
    inv = 1.0 / (ROPE_THETA ** (jnp.arange(0, DH_A, 2, dtype=F32) / DH_A))
    ang = jnp.arange(s, dtype=F32)[:, None] * inv[None, :]
    cos, sin = jnp.cos(ang), jnp.sin(ang)
    reps = LANES // half
    return jnp.tile(cos, (1, reps)), jnp.tile(jnp.concatenate([-sin, sin], axis=-1), (1, reps // 2))


def _attn_prepass(p3, w_a):
    b, s, _ = p3.shape
    h = w_a // LANES
    ts = _tile(s // 2, ATTN_KEY_CHUNK, LANES)
    cos, sin = _rope_tables(s)
    head_out = jax.ShapeDtypeStruct((b, h, s, LANES), BF16)
    vt_out = jax.ShapeDtypeStruct((b, h, s // ts, VT_ROWS, ts), BF16)
    hp = _tile(h, 4, 1)
    hg = h // hp
    col = lambda off: pl.BlockSpec((1, ts, hp * LANES), lambda bi, si, hi: (bi, si, off + hi))
    tab = pl.BlockSpec((ts, LANES), lambda bi, si, hi: (si, 0))
    out = pl.BlockSpec((1, hp, ts, LANES), lambda bi, si, hi: (bi, hi, si, 0))
    out_t = pl.BlockSpec((1, hp, 1, VT_ROWS, ts), lambda bi, si, hi: (bi, hi, si, 0, 0))
    return pl.pallas_call(
        _rope_kernel,
        grid=(b, s // ts, hg),
        in_specs=[col(0), col(hg), col(2 * hg), tab, tab],
        out_specs=[out, out, out_t],
        out_shape=[head_out, head_out, vt_out],
        compiler_params=_params(("parallel", "parallel", "arbitrary"), 32),
        name="attn_prepass",
    )(p3, p3, p3, cos, sin)


def _attn_kernel(q_ref, k_ref, vt_ref, z_ref, lq1_ref, lk1_ref, lq2_ref, lk2_ref, g_ref, o_ref,
                 qq_scr, acc_scr, sa_scr, sb_scr, *, tq, qt, lam_init):
    nq = q_ref.shape[2] // tq
    nk = vt_ref.shape[2]
    tk = vt_ref.shape[4]
    tiles = [slice(c * qt, (c + 1) * qt) for c in range(2 * tq // qt)]
    lane = lax.broadcasted_iota(jnp.int32, (tq, LANES), 1)
    lam = (jnp.exp(jnp.sum(lq1_ref[...] * lk1_ref[...], keepdims=True))
           - jnp.exp(jnp.sum(lq2_ref[...] * lk2_ref[...], keepdims=True)) + lam_init)

    def load_queries(qi):
        rows = pl.ds(pl.multiple_of(qi * tq, tq), tq)
        q = q_ref[0, 0, rows, :].astype(F32)
        qq_scr[0:tq, :] = jnp.where(lane < DH_A, q, 0.0).astype(BF16)
        qq_scr[tq:2 * tq, :] = jnp.where(lane >= DH_A, q, 0.0).astype(BF16)

    def produce(i, s_scr):
        off = pl.multiple_of(i * tk, tk)
        ks = k_ref[0, 0, pl.ds(off, tk), :]
        s = [_dot_nt(ks, qq_scr[t, :]) for t in tiles]
        for t, x in zip(tiles, s):
            s_scr[:, t] = x
        return jnp.concatenate([jnp.max(x, axis=0, keepdims=True) for x in s], axis=1)

    def consume(i, s_scr, cmax, m_old):
        vt = vt_ref[0, 0, i]
        m_new = jnp.maximum(m_old, cmax)
        alpha = jnp.exp2(m_old - m_new)
        p = [jnp.exp2(s_scr[:, t] - m_new[:, t]).astype(BF16) for t in tiles]
        pv = [_dot(vt, x) for x in p]
        for t, x in zip(tiles, pv):
            acc_scr[:, t] = alpha[:, t] * acc_scr[:, t] + x
        return m_new

    def pair(j, carry):
        m, cmax_a = carry
        cmax_b = produce(2 * j + 1, sb_scr)
        m = consume(2 * j, sa_scr, cmax_a, m)
        cmax_a = produce(2 * j + 2, sa_scr)
        m = consume(2 * j + 1, sb_scr, cmax_b, m)
        return m, cmax_a

    m0 = jnp.full((1, 2 * tq), -jnp.inf, F32)
    trips = nk // 2 - 1
    unroll = next(u for u in (3, 2, 1) if trips % u == 0 and (trips // u >= 2 or u == 1))

    def query_tile(qi, cmax_a):
        m, cmax_a = lax.fori_loop(0, trips, pair, (m0, cmax_a), unroll=unroll)
        cmax_b = produce(nk - 1, sb_scr)
        m = consume(nk - 2, sa_scr, cmax_a, m)
        load_queries(jnp.minimum(qi + 1, nq - 1))
        cmax_next = produce(0, sa_scr)
        consume(nk - 1, sb_scr, cmax_b, m)

        rows = pl.ds(pl.multiple_of(qi * tq, tq), tq)
        ot = acc_scr[0:LANES, :] / acc_scr[LANES:LANES + 1, :]
        acc_scr[...] = jnp.zeros(acc_scr.shape, F32)
        o = (ot[:, 0:tq] - lam * ot[:, tq:2 * tq]).T
        y = o * lax.rsqrt(jnp.mean(o * o, axis=-1, keepdims=True) + SUBLN_EPS) * g_ref[...] * (1.0 - lam_init)
        z = z_ref[0, rows, :].astype(F32)
        o_ref[0, rows, :] = (y * (z * _sigmoid(z))).astype(o_ref.dtype)
        return cmax_next

    load_queries(0)
    acc_scr[...] = jnp.zeros(acc_scr.shape, F32)
    lax.fori_loop(0, nq, query_tile, produce(0, sa_scr))


def _diff_attention(qr, kr, vt, p3, z_off, lam_q1, lam_k1, lam_q2, lam_k2, subln_g, lam_init):
    b, h, s, _ = qr.shape
    nk, tk = vt.shape[2], vt.shape[4]
    tq = _tile(s, 512, LANES)
    qt = _tile(2 * tq, 256, LANES)
    assert z_off % LANES == 0 and nk % 2 == 0
    zc = z_off // LANES
    vec = lambda n: pl.BlockSpec((1, n), lambda bi, hi: (0, 0))
    head = pl.BlockSpec((1, 1, s, LANES), lambda bi, hi: (bi, hi, 0, 0))
    kern = functools.partial(_attn_kernel, tq=tq, qt=qt, lam_init=lam_init)
    return pl.pallas_call(
        kern,
        grid=(b, h),
        in_specs=[
            head, head,
            pl.BlockSpec((1, 1, nk, VT_ROWS, tk), lambda bi, hi: (bi, hi, 0, 0, 0)),
            pl.BlockSpec((1, s, LANES), lambda bi, hi: (bi, 0, zc + hi)),
            vec(DH_A), vec(DH_A), vec(DH_A), vec(DH_A), vec(LANES),
        ],
        out_specs=pl.BlockSpec((1, s, LANES), lambda bi, hi: (bi, 0, hi)),
        out_shape=jax.ShapeDtypeStruct((b, s, h * LANES), BF16),
        scratch_shapes=[
            pltpu.VMEM((2 * tq, LANES), BF16),
            pltpu.VMEM((VT_ROWS, 2 * tq), F32),
            pltpu.VMEM((tk, 2 * tq), F32),
            pltpu.VMEM((tk, 2 * tq), F32),
        ],
        compiler_params=_params(("parallel", "arbitrary"), 56),
        name="diff_attn",
    )(qr, kr, vt, p3, lam_q1.reshape(1, DH_A), lam_k1.reshape(1, DH_A), lam_q2.reshape(1, DH_A),
      lam_k2.reshape(1, DH_A), subln_g.reshape(1, LANES))


def _group_sum(x, gmat):
    hi = x.astype(BF16)
    lo = (x - hi.astype(F32)).astype(BF16)
    return _dot(hi, gmat) + _dot(lo, gmat)


def _group_matrix(cw):
    r = lax.broadcasted_iota(jnp.int32, (cw, cw), 0) // N_B
    c = lax.broadcasted_iota(jnp.int32, (cw, cw), 1) // N_B
    return jnp.where(r == c, 1.0, 0.0).astype(BF16)


def _rwkv_prep_kernel(r_ref, k_ref, v_ref, lo_ref, rp_ref, kp_ref, vp_ref, lop_ref, rn_ref, kn_ref, vn_ref,
                      lon_ref, mu_ref, mulo_ref, w0_ref, a0_ref, wl_ref, al_ref, kk_ref, ka_ref, rk_ref,
                      ro_ref, vo_ref, ao_ref, lw_ref, kd_ref, bd_ref, bonus_ref, pw_scr, pa_scr):
    ts = r_ref.shape[1]
    si = pl.program_id(1)
    first = si == 0
    last = si == pl.num_programs(1) - 1

    def shift(cur_ref, prev_ref, next_ref, mu_p, mu_n):
        x = cur_ref[0].astype(F32)
        hp = prev_ref[0].astype(F32)
        hn = next_ref[0].astype(F32)
        pr = jnp.where(first, 0.0, hp[hp.shape[0] - 1:hp.shape[0], :])
        nx = jnp.where(last, 0.0, hn[0:1, :])
        row = lax.broadcasted_iota(jnp.int32, x.shape, 0)
        prev = jnp.where(row == 0, pr, pltpu.roll(x, 1, axis=0))
        nxt = jnp.where(row == ts - 1, nx, pltpu.roll(x, ts - 1, axis=0))
        return x + mu_p * (prev - x) + mu_n * (nxt - x)

    r = shift(r_ref, rp_ref, rn_ref, mu_ref[0, 0:1, :], mu_ref[1, 0:1, :])
    k = shift(k_ref, kp_ref, kn_ref, mu_ref[0, 1:2, :], mu_ref[1, 1:2, :])
    v = shift(v_ref, vp_ref, vn_ref, mu_ref[0, 2:3, :], mu_ref[1, 2:3, :])

    @pl.when(pl.program_id(2) == 0)
    def _():
        lo = shift(lo_ref, lop_ref, lon_ref, mulo_ref[0:1, :], mulo_ref[1:2, :])
        rl = lo.shape[1] // 4
        for d in range(2):
            pw_scr[d] = jnp.tanh(lo[:, d * rl:(d + 1) * rl]).astype(BF16)
            pa_scr[d] = lo[:, (2 + d) * rl:(3 + d) * rl].astype(BF16)

    cw = r.shape[1]
    gmat = _group_matrix(cw)
    kkh = k * kk_ref[...]
    nrm = jnp.sqrt(_group_sum(kkh * kkh, gmat))
    kk = kkh / jnp.maximum(nrm, 1e-12)
    ka = ka_ref[...]
    ro_ref[0] = r.astype(ro_ref.dtype)
    vo_ref[0] = v.astype(vo_ref.dtype)
    ao_ref[0] = (-kk).astype(ao_ref.dtype)
    ksum = jnp.zeros_like(k)
    for d in range(2):
        pw = pw_scr[d]
        pa = pa_scr[d]
        wl = w0_ref[d:d + 1, :] + _dot(pw, wl_ref[d])
        lw_ref[d, 0] = -math.exp(-0.5) * _sigmoid(wl)
        a = _sigmoid(a0_ref[d:d + 1, :] + _dot(pa, al_ref[d]))
        kdir = k * (1.0 + (a - 1.0) * ka)
        kd_ref[d, 0] = kdir.astype(kd_ref.dtype)
        bd_ref[d, 0] = (kk * a).astype(bd_ref.dtype)
        ksum = ksum + kdir
    bonus_ref[0] = _group_sum(r * ksum * rk_ref[...], gmat) * v


def _rwkv_prep(p3, s_off, w_b, r_lora, mu_prev, mu_next, w0, w_lora, a0, a_lora, k_k, k_a, r_k):
    b, s, _ = p3.shape
    ts = _tile(s, 512, 16)
    cw = _tile(w_b, 256, LANES)
    lw = 4 * r_lora
    halo = 16
    nblk = s // halo
    assert s_off % cw == 0 and w_b % cw == 0 and (s_off + 3 * w_b) % lw == 0 and ts % halo == 0
    rc, kc, vc, lc = s_off // cw, (s_off + w_b) // cw, (s_off + 2 * w_b) // cw, (s_off + 3 * w_b) // lw
    per = ts // halo

    cur = lambda off: pl.BlockSpec((1, ts, cw), lambda bi, si, ji: (bi, si, off + ji))
    prv = lambda off: pl.BlockSpec((1, halo, cw), lambda bi, si, ji: (bi, jnp.maximum(si * per - 1, 0), off + ji))
    nxt = lambda off: pl.BlockSpec((1, halo, cw),
                                   lambda bi, si, ji: (bi, jnp.minimum((si + 1) * per, nblk - 1), off + ji))
    lo_cur = pl.BlockSpec((1, ts, lw), lambda bi, si, ji: (bi, si, lc))
    lo_prv = pl.BlockSpec((1, halo, lw), lambda bi, si, ji: (bi, jnp.maximum(si * per - 1, 0), lc))
    lo_nxt = pl.BlockSpec((1, halo, lw), lambda bi, si, ji: (bi, jnp.minimum((si + 1) * per, nblk - 1), lc))
    colvec = lambda rows: pl.BlockSpec((rows, cw), lambda bi, si, ji: (0, ji))

    mu_rkv = jnp.stack([mu_prev[:3 * w_b].reshape(3, w_b), mu_next[:3 * w_b].reshape(3, w_b)])
    mu_lo = jnp.stack([mu_prev[3 * w_b:], mu_next[3 * w_b:]])
    tok = pl.BlockSpec((1, ts, cw), lambda bi, si, ji: (bi, si, ji))
    tok2 = pl.BlockSpec((2, 1, ts, cw), lambda bi, si, ji: (0, bi, si, ji))
    shp = lambda dt: jax.ShapeDtypeStruct((b, s, w_b), dt)
    shp2 = lambda dt: jax.ShapeDtypeStruct((2, b, s, w_b), dt)
    return pl.pallas_call(
        _rwkv_prep_kernel,
        grid=(b, s // ts, w_b // cw),
        in_specs=[cur(rc), cur(kc), cur(vc), lo_cur, prv(rc), prv(kc), prv(vc), lo_prv,
                  nxt(rc), nxt(kc), nxt(vc), lo_nxt,
                  pl.BlockSpec((2, 3, cw), lambda bi, si, ji: (0, 0, ji)),
                  pl.BlockSpec((2, lw), lambda bi, si, ji: (0, 0)),
                  colvec(2), colvec(2),
                  pl.BlockSpec((2, r_lora, cw), lambda bi, si, ji: (0, 0, ji)),
                  pl.BlockSpec((2, r_lora, cw), lambda bi, si, ji: (0, 0, ji)),
                  colvec(1), colvec(1), colvec(1)],
        out_specs=[tok, tok, tok, tok2, tok2, tok2, tok],
        out_shape=[shp(BF16), shp(BF16), shp(BF16), shp2(F32), shp2(BF16), shp2(BF16), shp(F32)],
        scratch_shapes=[pltpu.VMEM((2, ts, r_lora), BF16), pltpu.VMEM((2, ts, r_lora), BF16)],
        compiler_params=_params(("parallel", "parallel", "arbitrary"), 48),
        name="rwkv_prep",
    )(p3, p3, p3, p3, p3, p3, p3, p3, p3, p3, p3, p3, mu_rkv, mu_lo, w0, a0,
      w_lora.astype(BF16), a_lora.astype(BF16), k_k.reshape(1, w_b), k_a.reshape(1, w_b), r_k.reshape(1, w_b))


def _wkv_kernel(mask_ref, r_ref, a_ref, v_ref, lw_ref, k_ref, b_ref, y_ref, z_scr, *, npairs):
    assert CHUNK == N_B and 2 * N_B == LANES
    c = CHUNK
    ci = pl.program_id(3)

    @pl.when(ci == 0)
    def _():
        z_scr[...] = jnp.zeros(z_scr.shape, F32)

    strict = mask_ref[0, 0]
    incl = mask_ref[0, 1]
    cum = incl[0:c, 0:c].astype(BF16)
    lane = lax.broadcasted_iota(jnp.int32, (c, LANES), 1)
    low = lane < N_B
    row2 = lax.broadcasted_iota(jnp.int32, (LANES, LANES), 0)
    col2 = lax.broadcasted_iota(jnp.int32, (LANES, LANES), 1)
    own = (row2 < c) == (col2 < N_B)

    def stack(x):
        return jnp.concatenate([jnp.where(low, x, 0.0), jnp.where(low, 0.0, x)], axis=0)

    def each(f, *cols):
        return [f(*xs) for xs in zip(*cols)]

    sls = [slice(j * LANES, (j + 1) * LANES) for j in range(npairs)]
    lw = [lw_ref[0, 0, :, sl] for sl in sls]
    r = [r_ref[0, :, sl].astype(F32) for sl in sls]
    a = [a_ref[0, :, sl].astype(F32) for sl in sls]
    v = [v_ref[0, :, sl].astype(F32) for sl in sls]
    k = [k_ref[0, 0, :, sl].astype(F32) for sl in sls]
    b = [b_ref[0, 0, :, sl].astype(F32) for sl in sls]
    zb = [stack(z_scr[j]).astype(BF16) for j in range(npairs)]
    strict_p = strict[0:c, :] + strict[c:, :]
    incl_p = incl[0:c, :] + incl[c:, :]
    eye_p = lax.broadcasted_iota(jnp.int32, (c, LANES), 0) == (lane % N_B)

    def fold(x):
        return x[0:c, :] + x[c:, :]

    hi = each(lambda x: x.astype(BF16), lw)
    lo = each(lambda x, h: (x - h.astype(F32)).astype(BF16), lw, hi)
    cum2 = jnp.concatenate([cum, cum], axis=1)
    lam = each(lambda h, l: _dot(cum2, jnp.concatenate([h, l], axis=0)), hi, lo)
    tot = each(lambda x: jnp.sum(x, axis=0, keepdims=True), lw)
    gdec = each(jnp.exp, tot)
    a_p = each(lambda x, lm, w: x * jnp.exp(lm - w), a, lam, lw)
    r_p = each(lambda x, lm: x * jnp.exp(lm), r, lam)
    a_s = each(stack, a_p)
    e_neg = each(lambda lm: jnp.exp(-lm), lam)
    e_rem = each(lambda t, lm: jnp.exp(t - lm), tot, lam)
    b_s = each(lambda x, e: stack(x * e).astype(BF16), b, e_neg)
    k_s = each(lambda x, e: stack(x * e).astype(BF16), k, e_neg)
    bh_s = each(lambda x, e: stack(x * e).astype(BF16), b, e_rem)
    kh_s = each(lambda x, e: stack(x * e).astype(BF16), k, e_rem)
    v_s = each(lambda x: stack(x).astype(BF16), v)

    m4 = each(lambda x, y, p, q: _dot_nt(jnp.concatenate([x, y], axis=0).astype(BF16),
                                         jnp.concatenate([p, q], axis=0)), a_p, r_p, b_s, k_s)
    lp = each(lambda m: stack(m[0:c, 0:LANES] * strict_p).astype(BF16), m4)
    lak = each(lambda m: stack(m[0:c, LANES:] * strict_p).astype(BF16), m4)
    mrbk = each(lambda m: jnp.concatenate([stack(m[c:, 0:LANES] * incl_p), stack(m[c:, LANES:] * incl_p)],
                                          axis=1).astype(BF16), m4)

    half = LANES // 2
    x = each(lambda p, q, w: p + pltpu.roll(_dot(q, w), half, axis=1), a_s, lak, v_s)
    steps = int(math.log2(c))
    for it in range(steps - 1):
        res = each(lambda p, q: _dot(p, jnp.concatenate([q.astype(BF16), p], axis=1)), lp, x)
        x = each(lambda q, s: q + s[:, 0:LANES], x, res)
        lp = each(lambda s: s[:, LANES:].astype(BF16), res)
    x = each(lambda p, q: q + _dot(p, q.astype(BF16)), lp, x)
    w_s = each(lambda q: jnp.where(own, q, 0.0).astype(BF16), x)
    u_s = each(lambda q: jnp.where(own, pltpu.roll(q, half, axis=1), 0.0).astype(BF16), x)

    zero = jnp.zeros((LANES, LANES), BF16)
    wuv = each(lambda w, u, vv: jnp.concatenate([jnp.concatenate([w, u], axis=1),
                                                 jnp.concatenate([zero, vv], axis=1)], axis=0), w_s, u_s, v_s)
    ry = each(_dot, mrbk, wuv)
    pq = each(lambda p, q, m: _dot_tn(jnp.concatenate([p, q], axis=0), m), bh_s, kh_s, wuv)
    rw = each(lambda p, s: (p + fold(s[:, 0:LANES])).astype(BF16), r_p, ry)
    pt = each(lambda g, s: (jnp.where(eye_p, g, 0.0) + fold(s[:, 0:LANES])).astype(BF16), gdec, pq)

    ys = each(lambda p, z, s: _dot(p, z) + fold(s[:, LANES:]), rw, zb, ry)
    z_new = each(lambda p, z, s: _dot(p, z) + fold(s[:, LANES:]), pt, zb, pq)
    for j, sl in enumerate(sls):
        y_ref[0, 0, :, sl] = ys[j].astype(y_ref.dtype)
        z_scr[j] = z_new[j]


def _wkv_masks():
    i = np.arange(LANES)
    same = (i[:, None] // CHUNK) == (i[None, :] // CHUNK)
    t, s = i[:, None] % CHUNK, i[None, :] % CHUNK
    fwd = np.stack([same & (s < t), same & (s <= t)])
    bwd = np.stack([same & (s > t), same & (s >= t)])
    return jnp.asarray(np.stack([fwd, bwd]).astype(np.float32))


def _wkv_scan(r, a, v, logw, kdir, bdir):
    b, s, w_b = r.shape
    c = CHUNK
    nc = s // c
    gw = _tile(w_b, WKV_GROUP_LANES, LANES)
    npairs = gw // LANES
    chunk_of = lambda d, ci: ci + d * (nc - 1 - 2 * ci)
    shared = pl.BlockSpec((1, c, gw), lambda d, bi, gi, ci: (bi, chunk_of(d, ci), gi))
    perdir = pl.BlockSpec((1, 1, c, gw), lambda d, bi, gi, ci: (d, bi, chunk_of(d, ci), gi))
    kern = functools.partial(_wkv_kernel, npairs=npairs)
    return pl.pallas_call(
        kern,
        grid=(2, b, w_b // gw, nc),
        in_specs=[pl.BlockSpec((1, 2, LANES, LANES), lambda d, bi, gi, ci: (d, 0, 0, 0)),
                  shared, shared, shared, perdir, perdir, perdir],
        out_specs=perdir,
        out_shape=jax.ShapeDtypeStruct((2, b, s, w_b), BF16),
        scratch_shapes=[pltpu.VMEM((npairs, N_B, LANES), F32)],
        compiler_params=_params(("parallel", "parallel", "parallel", "arbitrary"), 32),
        name="wkv_scan",
    )(_wkv_masks(), r, a, v, logw, kdir, bdir)


def _rwkv_out_kernel(y_ref, bonus_ref, z_ref, g_ref, b_ref, o_ref):
    y = y_ref[0, 0].astype(F32) + y_ref[1, 0].astype(F32)
    gmat = _group_matrix(y.shape[1])
    mu = _group_sum(y, gmat) * (1.0 / N_B)
    yc = y - mu
    var = _group_sum(yc * yc, gmat) * (1.0 / N_B)
    yn = yc * lax.rsqrt(var + LNX_EPS) * g_ref[...] + b_ref[...]
    z = z_ref[0].astype(F32)
    o_ref[0] = ((yn + bonus_ref[0]) * (z * _sigmoid(z))).astype(o_ref.dtype)


def _rwkv_out(y2, bonus, p3, z_off, lnx_g, lnx_b):
    _, b, s, w_b = y2.shape
    ts = _tile(s, 512, 16)
    cw = _tile(w_b, 256, LANES)
    assert z_off % cw == 0
    zc = z_off // cw
    vec = pl.BlockSpec((1, cw), lambda bi, si, ji: (0, ji))
    return pl.pallas_call(
        _rwkv_out_kernel,
        grid=(b, s // ts, w_b // cw),
        in_specs=[pl.BlockSpec((2, 1, ts, cw), lambda bi, si, ji: (0, bi, si, ji)),
                  pl.BlockSpec((1, ts, cw), lambda bi, si, ji: (bi, si, ji)),
                  pl.BlockSpec((1, ts, cw), lambda bi, si, ji: (bi, si, zc + ji)),
                  vec, vec],
        out_specs=pl.BlockSpec((1, ts, cw), lambda bi, si, ji: (bi, si, ji)),
        out_shape=jax.ShapeDtypeStruct((b, s, w_b), BF16),
        compiler_params=_params(("parallel", "parallel", "parallel"), 32),
        name="rwkv_out",
    )(y2, bonus, p3, lnx_g.reshape(1, w_b), lnx_b.reshape(1, w_b))


def _merge_kernel(ua_ref, ub_ref, wa_ref, wb_ref, ga_ref, gb_ref, o_ref):
    oa = _dot(ua_ref[...], wa_ref[...])
    ob = _dot(ub_ref[...], wb_ref[...])
    m = _sigmoid(ga_ref[...].astype(F32)) * oa + _sigmoid(gb_ref[...].astype(F32)) * ob
    o_ref[...] = m.astype(o_ref.dtype)


def _merge(ua, ub, w_oa, w_ob, p2, g_off):
    t, w_a = ua.shape
    w_b = ub.shape[1]
    d = w_oa.shape[1]
    tm = _tile(t, 1024, 16)
    tn = _tile(d, 512, LANES)
    assert g_off % tn == 0 and d % tn == 0
    gc = g_off // tn
    nd = d // tn
    return pl.pallas_call(
        _merge_kernel,
        grid=(t // tm, nd),
        in_specs=[pl.BlockSpec((tm, w_a), lambda i, j: (i, 0)),
                  pl.BlockSpec((tm, w_b), lambda i, j: (i, 0)),
                  pl.BlockSpec((w_a, tn), lambda i, j: (0, j)),
                  pl.BlockSpec((w_b, tn), lambda i, j: (0, j)),
                  pl.BlockSpec((tm, tn), lambda i, j: (i, gc + j)),
                  pl.BlockSpec((tm, tn), lambda i, j: (i, gc + nd + j))],
        out_specs=pl.BlockSpec((tm, tn), lambda i, j: (i, j)),
        out_shape=jax.ShapeDtypeStruct((t, d), BF16),
        compiler_params=_params(("parallel", "arbitrary"), 48),
        name="gate_merge",
    )(ua, ub, w_oa, w_ob, p2, p2)


def _out_kernel(m_ref, w_ref, x_ref, o_ref):
    o_ref[...] = x_ref[...] + _dot(m_ref[...], w_ref[...])


def _out_norm_kernel(m_ref, w_ref, x_ref, g_ref, o_ref, res_scr, ss_scr):
    j = pl.program_id(1)
    nj, _, tn = res_scr.shape
    res = x_ref[...] + _dot(m_ref[...], w_ref[...])
    res_scr[j] = res

    @pl.when(j == 0)
    def _():
        ss_scr[...] = jnp.zeros(ss_scr.shape, F32)

    ss_scr[...] += jnp.sum(res * res, axis=-1, keepdims=True)

    @pl.when(j == nj - 1)
    def _():
        scale = lax.rsqrt(ss_scr[...] * (1.0 / (nj * tn)) + NORM_EPS)
        for jj in range(nj):
            cols = slice(jj * tn, (jj + 1) * tn)
            o_ref[:, cols] = res_scr[jj] * scale * g_ref[:, cols]


def _out_proj(m, w_out, x2, final_g, final_norm):
    t, d = x2.shape
    tm = _tile(t, 512, 16)
    tn = _tile(d, 512, 2 * LANES)
    in_specs = [pl.BlockSpec((tm, d), lambda i, j: (i, 0)),
                pl.BlockSpec((d, tn), lambda i, j: (0, j)),
                pl.BlockSpec((tm, tn), lambda i, j: (i, j))]
    if not final_norm:
        return pl.pallas_call(
            _out_kernel,
            grid=(t // tm, d // tn),
            in_specs=in_specs,
            out_specs=pl.BlockSpec((tm, tn), lambda i, j: (i, j)),
            out_shape=jax.ShapeDtypeStruct((t, d), F32),
            compiler_params=_params(("parallel", "arbitrary"), 56),
            name="out_proj",
        )(m, w_out, x2)
    return pl.pallas_call(
        _out_norm_kernel,
        grid=(t // tm, d // tn),
        in_specs=in_specs + [pl.BlockSpec((1, d), lambda i, j: (0, 0))],
        out_specs=pl.BlockSpec((tm, d), lambda i, j: (i, 0)),
        out_shape=jax.ShapeDtypeStruct((t, d), F32),
        scratch_shapes=[pltpu.VMEM((d // tn, tm, tn), F32), pltpu.VMEM((tm, 1), F32)],
        compiler_params=_params(("parallel", "arbitrary"), 56),
        name="out_proj_norm",
    )(m, w_out, x2, final_g.reshape(1, d))


def _lambda_init(layer_idx):
    return 0.8 - 0.6 * math.exp(-0.3 * layer_idx)


def _mixer_layer(x, l, prm, final_g, final_norm):
    b, s, d = x.shape
    w_a = prm["w_oA"][l].shape[0]
    w_b = prm["w_oB"][l].shape[0]
    r_lora = prm["w_lora"].shape[2]
    c_shift = 3 * w_b + 4 * r_lora
    n_in = prm["w_in"].shape[2]
    assert n_in == 3 * w_a + c_shift + w_a + w_b + 2 * d
    assert w_a % LANES == 0 and w_b % LANES == 0 and s % CHUNK == 0
    s_off = 3 * w_a
    za_off = s_off + c_shift
    zb_off = za_off + w_a
    g_off = zb_off + w_b
    lam_init = _lambda_init(l)

    x2 = x.reshape(b * s, d)
    h = _rmsnorm_bf16(x2, prm["norm_g"][l])
    p2 = _matmul_bf16(h, prm["w_in"][l].astype(BF16))
    p3 = p2.reshape(b, s, n_in)

    qr, kr, vt = _attn_prepass(p3, w_a)
    ua = _diff_attention(qr, kr, vt, p3, za_off, prm["lam_q1"][l], prm["lam_k1"][l], prm["lam_q2"][l],
                         prm["lam_k2"][l], prm["subln_g"][l], lam_init)

    r, v, a, logw, kdir, bdir, bonus = _rwkv_prep(
        p3, s_off, w_b, r_lora, prm["mu_prev"][l], prm["mu_next"][l], prm["w0"][l], prm["w_lora"][l],
        prm["a0"][l], prm["a_lora"][l], prm["k_k"][l], prm["k_a"][l], prm["r_k"][l])
    y2 = _wkv_scan(r, a, v, logw, kdir, bdir)
    ub = _rwkv_out(y2, bonus, p3, zb_off, prm["lnx_g"][l], prm["lnx_b"][l])

    m = _merge(ua.reshape(b * s, w_a), ub.reshape(b * s, w_b), prm["w_oA"][l].astype(BF16),
               prm["w_oB"][l].astype(BF16), p2, g_off)
    out = _out_proj(m, prm["w_out"][l].astype(BF16), x2, final_g, final_norm)
    return out.reshape(b, s, d)


def kernel(x_prompt, x_sample, norm_g, w_in, mu_prev, mu_next, lam_q1, lam_k1, lam_q2, lam_k2, subln_g, w0,
           w_lora, a0, a_lora, k_k, k_a, r_k, lnx_g, lnx_b, w_oA, w_oB, w_out, final_g):
    prm = dict(norm_g=norm_g, w_in=w_in, mu_prev=mu_prev, mu_next=mu_next, lam_q1=lam_q1, lam_k1=lam_k1,
               lam_q2=lam_q2, lam_k2=lam_k2, subln_g=subln_g, w0=w0, w_lora=w_lora, a0=a0, a_lora=a_lora,
               k_k=k_k, k_a=k_a, r_k=r_k, lnx_g=lnx_g, lnx_b=lnx_b, w_oA=w_oA, w_oB=w_oB, w_out=w_out)
    depth = norm_g.shape[0]

    def trunk(x):
        for l in range(depth):
            x = _mixer_layer(x, l, prm, final_g, final_norm=(l == depth - 1))
        return x

    return (trunk(x_prompt), trunk(x_sample))
```

```python
import functools
import math

import jax
import jax.numpy as jnp
import numpy as np
from jax import lax
from jax.experimental import pallas as pl
from jax.experimental.pallas import tpu as pltpu

F32 = jnp.float32
BF16 = jnp.bfloat16

LANES = 128
DH_A = 64
N_B = 64
CHUNK = 64
ROPE_THETA = 10000.0
ATTN_SCALE = DH_A ** -0.5
LOG2_E = math.log2(math.e)
ATTN_KEY_CHUNK = 512
WKV_GROUP_LANES = 2048
VT_ROWS = LANES + 16
NORM_EPS = 1e-6
SUBLN_EPS = 1e-5
LNX_EPS = 64e-5
MIB = 2 ** 20


def _tile(n, target, align):
    if n <= target:
        return n
    t = (target // align) * align
    while t >= align:
        if n % t == 0:
            return t
        t -= align
    raise ValueError(f"no tile for {n} (target {target}, align {align})")


def _params(semantics, vmem_mib):
    return pltpu.CompilerParams(dimension_semantics=semantics, vmem_limit_bytes=vmem_mib * MIB)


def _sigmoid(x):
    return 1.0 / (1.0 + jnp.exp(-x))


def _dot(a, b):
    return jnp.dot(a, b, preferred_element_type=F32)


def _dot_nt(a, b):
    return lax.dot_general(a, b, (((1,), (1,)), ((), ())), preferred_element_type=F32)


def _dot_tn(a, b):
    return lax.dot_general(a, b, (((0,), (0,)), ((), ())), preferred_element_type=F32)


def _rmsnorm_kernel(x_ref, g_ref, o_ref):
    x = x_ref[...]
    ms = jnp.mean(x * x, axis=-1, keepdims=True)
    o_ref[...] = (x * lax.rsqrt(ms + NORM_EPS) * g_ref[...]).astype(o_ref.dtype)


def _rmsnorm_bf16(x2, g):
    t, d = x2.shape
    tm = _tile(t, 256, 16)
    return pl.pallas_call(
        _rmsnorm_kernel,
        grid=(t // tm,),
        in_specs=[pl.BlockSpec((tm, d), lambda i: (i, 0)), pl.BlockSpec((1, d), lambda i: (0, 0))],
        out_specs=pl.BlockSpec((tm, d), lambda i: (i, 0)),
        out_shape=jax.ShapeDtypeStruct((t, d), BF16),
        compiler_params=_params(("parallel",), 32),
        name="rmsnorm_cast",
    )(x2, g.reshape(1, d))


def _matmul_kernel(a_ref, w_ref, o_ref):
    o_ref[...] = _dot(a_ref[...], w_ref[...].astype(BF16)).astype(o_ref.dtype)


def _matmul_bf16(a, w):
    m, k = a.shape
    n = w.shape[1]
    tm = _tile(m, 1024, 16)
    tn = _tile(n, 1024, 2 * LANES)
    return pl.pallas_call(
        _matmul_kernel,
        grid=(m // tm, n // tn),
        in_specs=[pl.BlockSpec((tm, k), lambda i, j: (i, 0)), pl.BlockSpec((k, tn), lambda i, j: (0, j))],
        out_specs=pl.BlockSpec((tm, tn), lambda i, j: (i, j)),
        out_shape=jax.ShapeDtypeStruct((m, n), BF16),
        compiler_params=_params(("parallel", "arbitrary"), 56),
        name="in_proj",
    )(a, w)


def _rope_kernel(q_ref, k_ref, v_ref, cos_ref, sin_ref, qo_ref, ko_ref, vo_ref):
    cos = cos_ref[...]
    sin = sin_ref[...]
    lane = lax.broadcasted_iota(jnp.int32, cos.shape, 1)
    first_half = (lane % DH_A) < (DH_A // 2)

    def rope(x):
        partner = jnp.where(first_half,
                            pltpu.roll(x, LANES - DH_A // 2, axis=1),
                            pltpu.roll(x, DH_A // 2, axis=1))
        return x * cos + partner * sin

    for hh in range(qo_ref.shape[1]):
        cols = slice(hh * LANES, (hh + 1) * LANES)
        qo_ref[0, hh] = (rope(q_ref[0, :, cols].astype(F32)) * (ATTN_SCALE * LOG2_E)).astype(qo_ref.dtype)
        ko_ref[0, hh] = rope(k_ref[0, :, cols].astype(F32)).astype(ko_ref.dtype)
        vo_ref[0, hh, 0, 0:LANES, :] = v_ref[0, :, cols].astype(F32).T.astype(vo_ref.dtype)
        vo_ref[0, hh, 0, LANES:, :] = jnp.ones((vo_ref.shape[3] - LANES, vo_ref.shape[4]), vo_ref.dtype)


def _rope_tables(s):
    half = DH_A // 2
    inv = 1.0 / (ROPE_THETA ** (jnp.arange(0, DH_A, 2, dtype=F32) / DH_A))
    ang = jnp.arange(s, dtype=F32)[:, None] * inv[None, :]
    cos, sin = jnp.cos(ang), jnp.sin(ang)
    reps = LANES // half
    return jnp.tile(cos, (1, reps)), jnp.tile(jnp.concatenate([-sin, sin], axis=-1), (1, reps // 2))


def _attn_prepass(p3, w_a):
    b, s, _ = p3.shape
    h = w_a // LANES
    ts = _tile(s // 2, ATTN_KEY_CHUNK, LANES)
    cos, sin = _rope_tables(s)
    head_out = jax.ShapeDtypeStruct((b, h, s, LANES), BF16)
    vt_out = jax.ShapeDtypeStruct((b, h, s // ts, VT_ROWS, ts), BF16)
    hp = _tile(h, 4, 1)
    hg = h // hp
    col = lambda off: pl.BlockSpec((1, ts, hp * LANES), lambda bi, si, hi: (bi, si, off + hi))
    tab = pl.BlockSpec((ts, LANES), lambda bi, si, hi: (si, 0))
    out = pl.BlockSpec((1, hp, ts, LANES), lambda bi, si, hi: (bi, hi, si, 0))
    out_t = pl.BlockSpec((1, hp, 1, VT_ROWS, ts), lambda bi, si, hi: (bi, hi, si, 0, 0))
    return pl.pallas_call(
        _rope_kernel,
        grid=(b, s // ts, hg),
        in_specs=[col(0), col(hg), col(2 * hg), tab, tab],
        out_specs=[out, out, out_t],
        out_shape=[head_out, head_out, vt_out],
        compiler_params=_params(("parallel", "parallel", "arbitrary"), 32),
        name="attn_prepass",
    )(p3, p3, p3, cos, sin)


def _attn_kernel(q_ref, k_ref, vt_ref, z_ref, lq1_ref, lk1_ref, lq2_ref, lk2_ref, g_ref, o_ref,
                 qq_scr, acc_scr, sa_scr, sb_scr, *, tq, qt, lam_init):
    nq = q_ref.shape[2] // tq
    nk = vt_ref.shape[2]
    tk = vt_ref.shape[4]
    tiles = [slice(c * qt, (c + 1) * qt) for c in range(2 * tq // qt)]
    lane = lax.broadcasted_iota(jnp.int32, (tq, LANES), 1)
    lam = (jnp.exp(jnp.sum(lq1_ref[...] * lk1_ref[...], keepdims=True))
           - jnp.exp(jnp.sum(lq2_ref[...] * lk2_ref[...], keepdims=True)) + lam_init)

    def load_queries(qi):
        rows = pl.ds(pl.multiple_of(qi * tq, tq), tq)
        q = q_ref[0, 0, rows, :].astype(F32)
        qq_scr[0:tq, :] = jnp.where(lane < DH_A, q, 0.0).astype(BF16)
        qq_scr[tq:2 * tq, :] = jnp.where(lane >= DH_A, q, 0.0).astype(BF16)

    def produce(i, s_scr):
        off = pl.multiple_of(i * tk, tk)
        ks = k_ref[0, 0, pl.ds(off, tk), :]
        s = [_dot_nt(ks, qq_scr[t, :]) for t in tiles]
        for t, x in zip(tiles, s):
            s_scr[:, t] = x
        return jnp.concatenate([jnp.max(x, axis=0, keepdims=True) for x in s], axis=1)

    def consume(i, s_scr, cmax, m_old):
        vt = vt_ref[0, 0, i]
        m_new = jnp.maximum(m_old, cmax)
        alpha = jnp.exp2(m_old - m_new)
        p = [jnp.exp2(s_scr[:, t] - m_new[:, t]).astype(BF16) for t in tiles]
        pv = [_dot(vt, x) for x in p]
        for t, x in zip(tiles, pv):
            acc_scr[:, t] = alpha[:, t] * acc_scr[:, t] + x
        return m_new

    def pair(j, carry):
        m, cmax_a = carry
        cmax_b = produce(2 * j + 1, sb_scr)
        m = consume(2 * j, sa_scr, cmax_a, m)
        cmax_a = produce(2 * j + 2, sa_scr)
        m = consume(2 * j + 1, sb_scr, cmax_b, m)
        return m, cmax_a

    m0 = jnp.full((1, 2 * tq), -jnp.inf, F32)
    trips = nk // 2 - 1
    unroll = next(u for u in (3, 2, 1) if trips % u == 0 and (trips // u >= 2 or u == 1))

    def query_tile(qi, cmax_a):
        m, cmax_a = lax.fori_loop(0, trips, pair, (m0, cmax_a), unroll=unroll)
        cmax_b = produce(nk - 1, sb_scr)
        m = consume(nk - 2, sa_scr, cmax_a, m)
        load_queries(jnp.minimum(qi + 1, nq - 1))
        cmax_next = produce(0, sa_scr)
        consume(nk - 1, sb_scr, cmax_b, m)

        rows = pl.ds(pl.multiple_of(qi * tq, tq), tq)
        ot = acc_scr[0:LANES, :] * (1.0 / acc_scr[LANES:LANES + 1, :])
        acc_scr[...] = jnp.zeros(acc_scr.shape, F32)
        o = (ot[:, 0:tq] - lam * ot[:, tq:2 * tq]).T
        y = o * lax.rsqrt(jnp.mean(o * o, axis=-1, keepdims=True) + SUBLN_EPS) * g_ref[...] * (1.0 - lam_init)
        z = z_ref[0, rows, :].astype(F32)
        o_ref[0, rows, :] = (y * (z * _sigmoid(z))).astype(o_ref.dtype)
        return cmax_next

    load_queries(0)
    acc_scr[...] = jnp.zeros(acc_scr.shape, F32)
    lax.fori_loop(0, nq, query_tile, produce(0, sa_scr))


def _diff_attention(qr, kr, vt, p3, z_off, lam_q1, lam_k1, lam_q2, lam_k2, subln_g, lam_init):
    b, h, s, _ = qr.shape
    nk, tk = vt.shape[2], vt.shape[4]
    tq = _tile(s, 512, LANES)
    qt = _tile(2 * tq, 256, LANES)
    assert z_off % LANES == 0 and nk % 2 == 0
    zc = z_off // LANES
    vec = lambda n: pl.BlockSpec((1, n), lambda bi, hi: (0, 0))
    head = pl.BlockSpec((1, 1, s, LANES), lambda bi, hi: (bi, hi, 0, 0))
    kern = functools.partial(_attn_kernel, tq=tq, qt=qt, lam_init=lam_init)
    return pl.pallas_call(
        kern,
        grid=(b, h),
        in_specs=[
            head, head,
            pl.BlockSpec((1, 1, nk, VT_ROWS, tk), lambda bi, hi: (bi, hi, 0, 0, 0)),
            pl.BlockSpec((1, s, LANES), lambda bi, hi: (bi, 0, zc + hi)),
            vec(DH_A), vec(DH_A), vec(DH_A), vec(DH_A), vec(LANES),
        ],
        out_specs=pl.BlockSpec((1, s, LANES), lambda bi, hi: (bi, 0, hi)),
        out_shape=jax.ShapeDtypeStruct((b, s, h * LANES), BF16),
        scratch_shapes=[
            pltpu.VMEM((2 * tq, LANES), BF16),
            pltpu.VMEM((VT_ROWS, 2 * tq), F32),
            pltpu.VMEM((tk, 2 * tq), F32),
            pltpu.VMEM((tk, 2 * tq), F32),
        ],
        compiler_params=_params(("parallel", "arbitrary"), 56),
        name="diff_attn",
    )(qr, kr, vt, p3, lam_q1.reshape(1, DH_A), lam_k1.reshape(1, DH_A), lam_q2.reshape(1, DH_A),
      lam_k2.reshape(1, DH_A), subln_g.reshape(1, LANES))


def _group_sum(x, gmat):
    hi = x.astype(BF16)
    lo = (x - hi.astype(F32)).astype(BF16)
    return _dot(hi, gmat) + _dot(lo, gmat)


def _group_matrix(cw):
    r = lax.broadcasted_iota(jnp.int32, (cw, cw), 0) // N_B
    c = lax.broadcasted_iota(jnp.int32, (cw, cw), 1) // N_B
    return jnp.where(r == c, 1.0, 0.0).astype(BF16)


def _rwkv_prep_kernel(r_ref, k_ref, v_ref, lo_ref, rp_ref, kp_ref, vp_ref, lop_ref, rn_ref, kn_ref, vn_ref,
                      lon_ref, mu_ref, mulo_ref, w0_ref, a0_ref, wl_ref, al_ref, kk_ref, ka_ref, rk_ref,
                      ro_ref, vo_ref, ao_ref, lw_ref, kd_ref, bd_ref, bonus_ref, pw_scr, pa_scr):
    ts = r_ref.shape[1]
    si = pl.program_id(1)
    first = si == 0
    last = si == pl.num_programs(1) - 1

    def shift(cur_ref, prev_ref, next_ref, mu_p, mu_n):
        x = cur_ref[0].astype(F32)
        hp = prev_ref[0].astype(F32)
        hn = next_ref[0].astype(F32)
        pr = jnp.where(first, 0.0, hp[hp.shape[0] - 1:hp.shape[0], :])
        nx = jnp.where(last, 0.0, hn[0:1, :])
        row = lax.broadcasted_iota(jnp.int32, x.shape, 0)
        prev = jnp.where(row == 0, pr, pltpu.roll(x, 1, axis=0))
        nxt = jnp.where(row == ts - 1, nx, pltpu.roll(x, ts - 1, axis=0))
        return x + mu_p * (prev - x) + mu_n * (nxt - x)

    r = shift(r_ref, rp_ref, rn_ref, mu_ref[0, 0:1, :], mu_ref[1, 0:1, :])
    k = shift(k_ref, kp_ref, kn_ref, mu_ref[0, 1:2, :], mu_ref[1, 1:2, :])
    v = shift(v_ref, vp_ref, vn_ref, mu_ref[0, 2:3, :], mu_ref[1, 2:3, :])

    @pl.when(pl.program_id(2) == 0)
    def _():
        lo = shift(lo_ref, lop_ref, lon_ref, mulo_ref[0:1, :], mulo_ref[1:2, :])
        rl = lo.shape[1] // 4
        for d in range(2):
            pw_scr[d] = jnp.tanh(lo[:, d * rl:(d + 1) * rl]).astype(BF16)
            pa_scr[d] = lo[:, (2 + d) * rl:(3 + d) * rl].astype(BF16)

    cw = r.shape[1]
    gmat = _group_matrix(cw)
    kkh = k * kk_ref[...]
    nrm = jnp.sqrt(_group_sum(kkh * kkh, gmat))
    kk = kkh / jnp.maximum(nrm, 1e-12)
    ka = ka_ref[...]
    ro_ref[0] = r.astype(ro_ref.dtype)
    vo_ref[0] = v.astype(vo_ref.dtype)
    ao_ref[0] = (-kk).astype(ao_ref.dtype)
    ksum = jnp.zeros_like(k)
    for d in range(2):
        pw = pw_scr[d]
        pa = pa_scr[d]
        wl = w0_ref[d:d + 1, :] + _dot(pw, wl_ref[d])
        lw_ref[d, 0] = -math.exp(-0.5) * _sigmoid(wl)
        a = _sigmoid(a0_ref[d:d + 1, :] + _dot(pa, al_ref[d]))
        kdir = k * (1.0 + (a - 1.0) * ka)
        kd_ref[d, 0] = kdir.astype(kd_ref.dtype)
        bd_ref[d, 0] = (kk * a).astype(bd_ref.dtype)
        ksum = ksum + kdir
    bonus_ref[0] = _group_sum(r * ksum * rk_ref[...], gmat) * v


def _rwkv_prep(p3, s_off, w_b, r_lora, mu_prev, mu_next, w0, w_lora, a0, a_lora, k_k, k_a, r_k):
    b, s, _ = p3.shape
    ts = _tile(s, 512, 16)
    cw = _tile(w_b, 256, LANES)
    lw = 4 * r_lora
    halo = 16
    nblk = s // halo
    assert s_off % cw == 0 and w_b % cw == 0 and (s_off + 3 * w_b) % lw == 0 and ts % halo == 0
    rc, kc, vc, lc = s_off // cw, (s_off + w_b) // cw, (s_off + 2 * w_b) // cw, (s_off + 3 * w_b) // lw
    per = ts // halo

    cur = lambda off: pl.BlockSpec((1, ts, cw), lambda bi, si, ji: (bi, si, off + ji))
    prv = lambda off: pl.BlockSpec((1, halo, cw), lambda bi, si, ji: (bi, jnp.maximum(si * per - 1, 0), off + ji))
    nxt = lambda off: pl.BlockSpec((1, halo, cw),
                                   lambda bi, si, ji: (bi, jnp.minimum((si + 1) * per, nblk - 1), off + ji))
    lo_cur = pl.BlockSpec((1, ts, lw), lambda bi, si, ji: (bi, si, lc))
    lo_prv = pl.BlockSpec((1, halo, lw), lambda bi, si, ji: (bi, jnp.maximum(si * per - 1, 0), lc))
    lo_nxt = pl.BlockSpec((1, halo, lw), lambda bi, si, ji: (bi, jnp.minimum((si + 1) * per, nblk - 1), lc))
    colvec = lambda rows: pl.BlockSpec((rows, cw), lambda bi, si, ji: (0, ji))

    mu_rkv = jnp.stack([mu_prev[:3 * w_b].reshape(3, w_b), mu_next[:3 * w_b].reshape(3, w_b)])
    mu_lo = jnp.stack([mu_prev[3 * w_b:], mu_next[3 * w_b:]])
    tok = pl.BlockSpec((1, ts, cw), lambda bi, si, ji: (bi, si, ji))
    tok2 = pl.BlockSpec((2, 1, ts, cw), lambda bi, si, ji: (0, bi, si, ji))
    shp = lambda dt: jax.ShapeDtypeStruct((b, s, w_b), dt)
    shp2 = lambda dt: jax.ShapeDtypeStruct((2, b, s, w_b), dt)
    return pl.pallas_call(
        _rwkv_prep_kernel,
        grid=(b, s // ts, w_b // cw),
        in_specs=[cur(rc), cur(kc), cur(vc), lo_cur, prv(rc), prv(kc), prv(vc), lo_prv,
                  nxt(rc), nxt(kc), nxt(vc), lo_nxt,
                  pl.BlockSpec((2, 3, cw), lambda bi, si, ji: (0, 0, ji)),
                  pl.BlockSpec((2, lw), lambda bi, si, ji: (0, 0)),
                  colvec(2), colvec(2),
                  pl.BlockSpec((2, r_lora, cw), lambda bi, si, ji: (0, 0, ji)),
                  pl.BlockSpec((2, r_lora, cw), lambda bi, si, ji: (0, 0, ji)),
                  colvec(1), colvec(1), colvec(1)],
        out_specs=[tok, tok, tok, tok2, tok2, tok2, tok],
        out_shape=[shp(BF16), shp(BF16), shp(BF16), shp2(F32), shp2(BF16), shp2(BF16), shp(F32)],
        scratch_shapes=[pltpu.VMEM((2, ts, r_lora), BF16), pltpu.VMEM((2, ts, r_lora), BF16)],
        compiler_params=_params(("parallel", "parallel", "arbitrary"), 48),
        name="rwkv_prep",
    )(p3, p3, p3, p3, p3, p3, p3, p3, p3, p3, p3, p3, mu_rkv, mu_lo, w0, a0,
      w_lora.astype(BF16), a_lora.astype(BF16), k_k.reshape(1, w_b), k_a.reshape(1, w_b), r_k.reshape(1, w_b))


def _wkv_kernel(mask_ref, r_ref, a_ref, v_ref, lw_ref, k_ref, b_ref, y_ref, z_scr, *, npairs):
    assert CHUNK == N_B and 2 * N_B == LANES
    c = CHUNK
    ci = pl.program_id(3)

    @pl.when(ci == 0)
    def _():
        z_scr[...] = jnp.zeros(z_scr.shape, F32)

    strict = mask_ref[0, 0]
    incl = mask_ref[0, 1]
    cum = incl[0:c, 0:c].astype(BF16)
    lane = lax.broadcasted_iota(jnp.int32, (c, LANES), 1)
    low = lane < N_B
    row2 = lax.broadcasted_iota(jnp.int32, (LANES, LANES), 0)
    col2 = lax.broadcasted_iota(jnp.int32, (LANES, LANES), 1)
    own = (row2 < c) == (col2 < N_B)

    def stack(x):
        return jnp.concatenate([jnp.where(low, x, 0.0), jnp.where(low, 0.0, x)], axis=0)

    def each(f, *cols):
        return [f(*xs) for xs in zip(*cols)]

    sls = [slice(j * LANES, (j + 1) * LANES) for j in range(npairs)]
    lw = [lw_ref[0, 0, :, sl] for sl in sls]
    r = [r_ref[0, :, sl].astype(F32) for sl in sls]
    a = [a_ref[0, :, sl].astype(F32) for sl in sls]
    v = [v_ref[0, :, sl].astype(F32) for sl in sls]
    k = [k_ref[0, 0, :, sl].astype(F32) for sl in sls]
    b = [b_ref[0, 0, :, sl].astype(F32) for sl in sls]
    zb = [stack(z_scr[j]).astype(BF16) for j in range(npairs)]
    strict_p = strict[0:c, :] + strict[c:, :]
    incl_p = incl[0:c, :] + incl[c:, :]
    eye_p = lax.broadcasted_iota(jnp.int32, (c, LANES), 0) == (lane % N_B)

    def fold(x):
        return x[0:c, :] + x[c:, :]

    hi = each(lambda x: x.astype(BF16), lw)
    lo = each(lambda x, h: (x - h.astype(F32)).astype(BF16), lw, hi)
    cum2 = jnp.concatenate([cum, cum], axis=1)
    lam = each(lambda h, l: _dot(cum2, jnp.concatenate([h, l], axis=0)), hi, lo)
    tot = each(lambda x: jnp.sum(x, axis=0, keepdims=True), lw)
    gdec = each(jnp.exp, tot)
    a_p = each(lambda x, lm, w: x * jnp.exp(lm - w), a, lam, lw)
    r_p = each(lambda x, lm: x * jnp.exp(lm), r, lam)
    a_s = each(stack, a_p)
    e_neg = each(lambda lm: jnp.exp(-lm), lam)
    e_rem = each(lambda t, lm: jnp.exp(t - lm), tot, lam)
    b_s = each(lambda x, e: stack(x * e).astype(BF16), b, e_neg)
    k_s = each(lambda x, e: stack(x * e).astype(BF16), k, e_neg)
    bh_s = each(lambda x, e: stack(x * e).astype(BF16), b, e_rem)
    kh_s = each(lambda x, e: stack(x * e).astype(BF16), k, e_rem)
    v_s = each(lambda x: stack(x).astype(BF16), v)

    m4 = each(lambda x, y, p, q: _dot_nt(jnp.concatenate([x, y], axis=0).astype(BF16),
                                         jnp.concatenate([p, q], axis=0)), a_p, r_p, b_s, k_s)
    lp = each(lambda m: stack(m[0:c, 0:LANES] * strict_p).astype(BF16), m4)
    lak = each(lambda m: stack(m[0:c, LANES:] * strict_p).astype(BF16), m4)
    mrbk = each(lambda m: jnp.concatenate([stack(m[c:, 0:LANES] * incl_p), stack(m[c:, LANES:] * incl_p)],
                                          axis=1).astype(BF16), m4)

    half = LANES // 2
    x = each(lambda p, q, w: p + pltpu.roll(_dot(q, w), half, axis=1), a_s, lak, v_s)
    steps = int(math.log2(c))
    for it in range(steps - 1):
        res = each(lambda p, q: _dot(p, jnp.concatenate([q.astype(BF16), p], axis=1)), lp, x)
        x = each(lambda q, s: q + s[:, 0:LANES], x, res)
        lp = each(lambda s: s[:, LANES:].astype(BF16), res)
    x = each(lambda p, q: q + _dot(p, q.astype(BF16)), lp, x)
    w_s = each(lambda q: jnp.where(own, q, 0.0).astype(BF16), x)
    u_s = each(lambda q: jnp.where(own, pltpu.roll(q, half, axis=1), 0.0).astype(BF16), x)

    zero = jnp.zeros((LANES, LANES), BF16)
    wuv = each(lambda w, u, vv: jnp.concatenate([jnp.concatenate([w, u], axis=1),
                                                 jnp.concatenate([zero, vv], axis=1)], axis=0), w_s, u_s, v_s)
    ry = each(_dot, mrbk, wuv)
    pq = each(lambda p, q, m: _dot_tn(jnp.concatenate([p, q], axis=0), m), bh_s, kh_s, wuv)
    rw = each(lambda p, s: (p + fold(s[:, 0:LANES])).astype(BF16), r_p, ry)
    pt = each(lambda g, s: (jnp.where(eye_p, g, 0.0) + fold(s[:, 0:LANES])).astype(BF16), gdec, pq)

    ys = each(lambda p, z, s: _dot(p, z) + fold(s[:, LANES:]), rw, zb, ry)
    z_new = each(lambda p, z, s: _dot(p, z) + fold(s[:, LANES:]), pt, zb, pq)
    for j, sl in enumerate(sls):
        y_ref[0, 0, :, sl] = ys[j].astype(y_ref.dtype)
        z_scr[j] = z_new[j]


def _wkv_masks():
    i = np.arange(LANES)
    same = (i[:, None] // CHUNK) == (i[None, :] // CHUNK)
    t, s = i[:, None] % CHUNK, i[None, :] % CHUNK
    fwd = np.stack([same & (s < t), same & (s <= t)])
    bwd = np.stack([same & (s > t), same & (s >= t)])
    return jnp.asarray(np.stack([fwd, bwd]).astype(np.float32))


def _wkv_scan(r, a, v, logw, kdir, bdir):
    b, s, w_b = r.shape
    c = CHUNK
    nc = s // c
    gw = _tile(w_b, WKV_GROUP_LANES, LANES)
    npairs = gw // LANES
    chunk_of = lambda d, ci: ci + d * (nc - 1 - 2 * ci)
    shared = pl.BlockSpec((1, c, gw), lambda d, bi, gi, ci: (bi, chunk_of(d, ci), gi))
    perdir = pl.BlockSpec((1, 1, c, gw), lambda d, bi, gi, ci: (d, bi, chunk_of(d, ci), gi))
    kern = functools.partial(_wkv_kernel, npairs=npairs)
    return pl.pallas_call(
        kern,
        grid=(2, b, w_b // gw, nc),
        in_specs=[pl.BlockSpec((1, 2, LANES, LANES), lambda d, bi, gi, ci: (d, 0, 0, 0)),
                  shared, shared, shared, perdir, perdir, perdir],
        out_specs=perdir,
        out_shape=jax.ShapeDtypeStruct((2, b, s, w_b), BF16),
        scratch_shapes=[pltpu.VMEM((npairs, N_B, LANES), F32)],
        compiler_params=_params(("parallel", "parallel", "parallel", "arbitrary"), 32),
        name="wkv_scan",
    )(_wkv_masks(), r, a, v, logw, kdir, bdir)


def _rwkv_out_kernel(y_ref, bonus_ref, z_ref, g_ref, b_ref, o_ref):
    y = y_ref[0, 0].astype(F32) + y_ref[1, 0].astype(F32)
    gmat = _group_matrix(y.shape[1])
    mu = _group_sum(y, gmat) * (1.0 / N_B)
    yc = y - mu
    var = _group_sum(yc * yc, gmat) * (1.0 / N_B)
    yn = yc * lax.rsqrt(var + LNX_EPS) * g_ref[...] + b_ref[...]
    z = z_ref[0].astype(F32)
    o_ref[0] = ((yn + bonus_ref[0]) * (z * _sigmoid(z))).astype(o_ref.dtype)


def _rwkv_out(y2, bonus, p3, z_off, lnx_g, lnx_b):
    _, b, s, w_b = y2.shape
    ts = _tile(s, 512, 16)
    cw = _tile(w_b, 256, LANES)
    assert z_off % cw == 0
    zc = z_off // cw
    vec = pl.BlockSpec((1, cw), lambda bi, si, ji: (0, ji))
    return pl.pallas_call(
        _rwkv_out_kernel,
        grid=(b, s // ts, w_b // cw),
        in_specs=[pl.BlockSpec((2, 1, ts, cw), lambda bi, si, ji: (0, bi, si, ji)),
                  pl.BlockSpec((1, ts, cw), lambda bi, si, ji: (bi, si, ji)),
                  pl.BlockSpec((1, ts, cw), lambda bi, si, ji: (bi, si, zc + ji)),
                  vec, vec],
        out_specs=pl.BlockSpec((1, ts, cw), lambda bi, si, ji: (bi, si, ji)),
        out_shape=jax.ShapeDtypeStruct((b, s, w_b), BF16),
        compiler_params=_params(("parallel", "parallel", "parallel"), 32),
        name="rwkv_out",
    )(y2, bonus, p3, lnx_g.reshape(1, w_b), lnx_b.reshape(1, w_b))


def _merge_kernel(ua_ref, ub_ref, wa_ref, wb_ref, ga_ref, gb_ref, o_ref):
    oa = _dot(ua_ref[...], wa_ref[...])
    ob = _dot(ub_ref[...], wb_ref[...])
    m = _sigmoid(ga_ref[...].astype(F32)) * oa + _sigmoid(gb_ref[...].astype(F32)) * ob
    o_ref[...] = m.astype(o_ref.dtype)


def _merge(ua, ub, w_oa, w_ob, p2, g_off):
    t, w_a = ua.shape
    w_b = ub.shape[1]
    d = w_oa.shape[1]
    tm = _tile(t, 1024, 16)
    tn = _tile(d, 512, LANES)
    assert g_off % tn == 0 and d % tn == 0
    gc = g_off // tn
    nd = d // tn
    return pl.pallas_call(
        _merge_kernel,
        grid=(t // tm, nd),
        in_specs=[pl.BlockSpec((tm, w_a), lambda i, j: (i, 0)),
                  pl.BlockSpec((tm, w_b), lambda i, j: (i, 0)),
                  pl.BlockSpec((w_a, tn), lambda i, j: (0, j)),
                  pl.BlockSpec((w_b, tn), lambda i, j: (0, j)),
                  pl.BlockSpec((tm, tn), lambda i, j: (i, gc + j)),
                  pl.BlockSpec((tm, tn), lambda i, j: (i, gc + nd + j))],
        out_specs=pl.BlockSpec((tm, tn), lambda i, j: (i, j)),
        out_shape=jax.ShapeDtypeStruct((t, d), BF16),
        compiler_params=_params(("parallel", "arbitrary"), 48),
        name="gate_merge",
    )(ua, ub, w_oa, w_ob, p2, p2)


def _out_kernel(m_ref, w_ref, x_ref, o_ref):
    o_ref[...] = x_ref[...] + _dot(m_ref[...], w_ref[...])


def _out_norm_kernel(m_ref, w_ref, x_ref, g_ref, o_ref, res_scr, ss_scr):
    j = pl.program_id(1)
    nj, _, tn = res_scr.shape
    res = x_ref[...] + _dot(m_ref[...], w_ref[...])
    res_scr[j] = res

    @pl.when(j == 0)
    def _():
        ss_scr[...] = jnp.zeros(ss_scr.shape, F32)

    ss_scr[...] += jnp.sum(res * res, axis=-1, keepdims=True)

    @pl.when(j == nj - 1)
    def _():
        scale = lax.rsqrt(ss_scr[...] * (1.0 / (nj * tn)) + NORM_EPS)
        for jj in range(nj):
            cols = slice(jj * tn, (jj + 1) * tn)
            o_ref[:, cols] = res_scr[jj] * scale * g_ref[:, cols]


def _out_proj(m, w_out, x2, final_g, final_norm):
    t, d = x2.shape
    tm = _tile(t, 512, 16)
    tn = _tile(d, 512, 2 * LANES)
    in_specs = [pl.BlockSpec((tm, d), lambda i, j: (i, 0)),
                pl.BlockSpec((d, tn), lambda i, j: (0, j)),
                pl.BlockSpec((tm, tn), lambda i, j: (i, j))]
    if not final_norm:
        return pl.pallas_call(
            _out_kernel,
            grid=(t // tm, d // tn),
            in_specs=in_specs,
            out_specs=pl.BlockSpec((tm, tn), lambda i, j: (i, j)),
            out_shape=jax.ShapeDtypeStruct((t, d), F32),
            compiler_params=_params(("parallel", "arbitrary"), 56),
            name="out_proj",
        )(m, w_out, x2)
    return pl.pallas_call(
        _out_norm_kernel,
        grid=(t // tm, d // tn),
        in_specs=in_specs + [pl.BlockSpec((1, d), lambda i, j: (0, 0))],
        out_specs=pl.BlockSpec((tm, d), lambda i, j: (i, 0)),
        out_shape=jax.ShapeDtypeStruct((t, d), F32),
        scratch_shapes=[pltpu.VMEM((d // tn, tm, tn), F32), pltpu.VMEM((tm, 1), F32)],
        compiler_params=_params(("parallel", "arbitrary"), 56),
        name="out_proj_norm",
    )(m, w_out, x2, final_g.reshape(1, d))


def _lambda_init(layer_idx):
    return 0.8 - 0.6 * math.exp(-0.3 * layer_idx)


def _mixer_layer(x, l, prm, final_g, final_norm):
    b, s, d = x.shape
    w_a = prm["w_oA"][l].shape[0]
    w_b = prm["w_oB"][l].shape[0]
    r_lora = prm["w_lora"].shape[2]
    c_shift = 3 * w_b + 4 * r_lora
    n_in = prm["w_in"].shape[2]
    assert n_in == 3 * w_a + c_shift + w_a + w_b + 2 * d
    assert w_a % LANES == 0 and w_b % LANES == 0 and s % CHUNK == 0
    s_off = 3 * w_a
    za_off = s_off + c_shift
    zb_off = za_off + w_a
    g_off = zb_off + w_b
    lam_init = _lambda_init(l)

    x2 = x.reshape(b * s, d)
    h = _rmsnorm_bf16(x2, prm["norm_g"][l])
    p2 = _matmul_bf16(h, prm["w_in"][l])
    p3 = p2.reshape(b, s, n_in)

    qr, kr, vt = _attn_prepass(p3, w_a)
    ua = _diff_attention(qr, kr, vt, p3, za_off, prm["lam_q1"][l], prm["lam_k1"][l], prm["lam_q2"][l],
                         prm["lam_k2"][l], prm["subln_g"][l], lam_init)

    r, v, a, logw, kdir, bdir, bonus = _rwkv_prep(
        p3, s_off, w_b, r_lora, prm["mu_prev"][l], prm["mu_next"][l], prm["w0"][l], prm["w_lora"][l],
        prm["a0"][l], prm["a_lora"][l], prm["k_k"][l], prm["k_a"][l], prm["r_k"][l])
    y2 = _wkv_scan(r, a, v, logw, kdir, bdir)
    ub = _rwkv_out(y2, bonus, p3, zb_off, prm["lnx_g"][l], prm["lnx_b"][l])

    m = _merge(ua.reshape(b * s, w_a), ub.reshape(b * s, w_b), prm["w_oA"][l].astype(BF16),
               prm["w_oB"][l].astype(BF16), p2, g_off)
    out = _out_proj(m, prm["w_out"][l].astype(BF16), x2, final_g, final_norm)
    return out.reshape(b, s, d)


def kernel(x_prompt, x_sample, norm_g, w_in, mu_prev, mu_next, lam_q1, lam_k1, lam_q2, lam_k2, subln_g, w0,
           w_lora, a0, a_lora, k_k, k_a, r_k, lnx_g, lnx_b, w_oA, w_oB, w_out, final_g):
    prm = dict(norm_g=norm_g, w_in=w_in, mu_prev=mu_prev, mu_next=mu_next, lam_q1=lam_q1, lam_k1=lam_k1,
               lam_q2=lam_q2, lam_k2=lam_k2, subln_g=subln_g, w0=w0, w_lora=w_lora, a0=a0, a_lora=a_lora,
               k_k=k_k, k_a=k_a, r_k=r_k, lnx_g=lnx_g, lnx_b=lnx_b, w_oA=w_oA, w_oB=w_oB, w_out=w_out)
    depth = norm_g.shape[0]

    def trunk(x):
        for l in range(depth):
            x = _mixer_layer(x, l, prm, final_g, final_norm=(l == depth - 1))
        return x

    return (trunk(x_prompt), trunk(x_sample))
```

```python
import functools
import math

import jax
import jax.numpy as jnp
import numpy as np
from jax import lax
from jax.experimental import pallas as pl
from jax.experimental.pallas import tpu as pltpu

F32 = jnp.float32
BF16 = jnp.bfloat16

LANES = 128
DH_A = 64
N_B = 64
CHUNK = 64
ROPE_THETA = 10000.0
ATTN_SCALE = DH_A ** -0.5
LOG2_E = math.log2(math.e)
ATTN_KEY_CHUNK = 512
WKV_GROUP_LANES = 2048
VT_ROWS = LANES + 16
NORM_EPS = 1e-6
SUBLN_EPS = 1e-5
LNX_EPS = 64e-5
MIB = 2 ** 20


def _tile(n, target, align):
    if n <= target:
        return n
    t = (target // align) * align
    while t >= align:
        if n % t == 0:
            return t
        t -= align
    raise ValueError(f"no tile for {n} (target {target}, align {align})")


def _params(semantics, vmem_mib):
    return pltpu.CompilerParams(dimension_semantics=semantics, vmem_limit_bytes=vmem_mib * MIB)


def _sigmoid(x):
    return jax.nn.sigmoid(x)


def _dot(a, b):
    return jnp.dot(a, b, preferred_element_type=F32)


def _dot_nt(a, b):
    return lax.dot_general(a, b, (((1,), (1,)), ((), ())), preferred_element_type=F32)


def _dot_tn(a, b):
    return lax.dot_general(a, b, (((0,), (0,)), ((), ())), preferred_element_type=F32)


def _rmsnorm_kernel(x_ref, g_ref, o_ref):
    x = x_ref[...]
    ms = jnp.mean(x * x, axis=-1, keepdims=True)
    o_ref[...] = (x * lax.rsqrt(ms + NORM_EPS) * g_ref[...]).astype(o_ref.dtype)


def _rmsnorm_bf16(x2, g):
    t, d = x2.shape
    tm = _tile(t, 256, 16)
    return pl.pallas_call(
        _rmsnorm_kernel,
        grid=(t // tm,),
        in_specs=[pl.BlockSpec((tm, d), lambda i: (i, 0)), pl.BlockSpec((1, d), lambda i: (0, 0))],
        out_specs=pl.BlockSpec((tm, d), lambda i: (i, 0)),
        out_shape=jax.ShapeDtypeStruct((t, d), BF16),
        compiler_params=_params(("parallel",), 32),
        name="rmsnorm_cast",
    )(x2, g.reshape(1, d))


def _matmul_kernel(a_ref, w_ref, o_ref):
    o_ref[...] = _dot(a_ref[...], w_ref[...].astype(BF16)).astype(o_ref.dtype)


def _matmul_bf16(a, w):
    m, k = a.shape
    n = w.shape[1]
    tm = _tile(m, 1024, 16)
    tn = _tile(n, 1024, 2 * LANES)
    return pl.pallas_call(
        _matmul_kernel,
        grid=(m // tm, n // tn),
        in_specs=[pl.BlockSpec((tm, k), lambda i, j: (i, 0)), pl.BlockSpec((k, tn), lambda i, j: (0, j))],
        out_specs=pl.BlockSpec((tm, tn), lambda i, j: (i, j)),
        out_shape=jax.ShapeDtypeStruct((m, n), BF16),
        compiler_params=_params(("parallel", "arbitrary"), 56),
        name="in_proj",
    )(a, w)


def _rope_kernel(q_ref, k_ref, v_ref, cos_ref, sin_ref, qo_ref, ko_ref, vo_ref):
    cos = cos_ref[...]
    sin = sin_ref[...]
    lane = lax.broadcasted_iota(jnp.int32, cos.shape, 1)
    first_half = (lane % DH_A) < (DH_A // 2)

    def rope(x):
        partner = jnp.where(first_half,
                            pltpu.roll(x, LANES - DH_A // 2, axis=1),
                            pltpu.roll(x, DH_A // 2, axis=1))
        return x * cos + partner * sin

    for hh in range(qo_ref.shape[1]):
        cols = slice(hh * LANES, (hh + 1) * LANES)
        qo_ref[0, hh] = (rope(q_ref[0, :, cols].astype(F32)) * (ATTN_SCALE * LOG2_E)).astype(qo_ref.dtype)
        ko_ref[0, hh] = rope(k_ref[0, :, cols].astype(F32)).astype(ko_ref.dtype)
        vo_ref[0, hh, 0, 0:LANES, :] = v_ref[0, :, cols].astype(F32).T.astype(vo_ref.dtype)
        vo_ref[0, hh, 0, LANES:, :] = jnp.ones((vo_ref.shape[3] - LANES, vo_ref.shape[4]), vo_ref.dtype)


def _rope_tables(s):
    half = DH_A // 2
    inv = 1.0 / (ROPE_THETA ** (jnp.arange(0, DH_A, 2, dtype=F32) / DH_A))
    ang = jnp.arange(s, dtype=F32)[:, None] * inv[None, :]
    cos, sin = jnp.cos(ang), jnp.sin(ang)
    reps = LANES // half
    return jnp.tile(cos, (1, reps)), jnp.tile(jnp.concatenate([-sin, sin], axis=-1), (1, reps // 2))


def _attn_prepass(p3, w_a):
    b, s, _ = p3.shape
    h = w_a // LANES
    ts = _tile(s // 2, ATTN_KEY_CHUNK, LANES)
    cos, sin = _rope_tables(s)
    head_out = jax.ShapeDtypeStruct((b, h, s, LANES), BF16)
    vt_out = jax.ShapeDtypeStruct((b, h, s // ts, VT_ROWS, ts), BF16)
    hp = _tile(h, 4, 1)
    hg = h // hp
    col = lambda off: pl.BlockSpec((1, ts, hp * LANES), lambda bi, si, hi: (bi, si, off + hi))
    tab = pl.BlockSpec((ts, LANES), lambda bi, si, hi: (si, 0))
    out = pl.BlockSpec((1, hp, ts, LANES), lambda bi, si, hi: (bi, hi, si, 0))
    out_t = pl.BlockSpec((1, hp, 1, VT_ROWS, ts), lambda bi, si, hi: (bi, hi, si, 0, 0))
    return pl.pallas_call(
        _rope_kernel,
        grid=(b, s // ts, hg),
        in_specs=[col(0), col(hg), col(2 * hg), tab, tab],
        out_specs=[out, out, out_t],
        out_shape=[head_out, head_out, vt_out],
        compiler_params=_params(("parallel", "parallel", "arbitrary"), 32),
        name="attn_prepass",
    )(p3, p3, p3, cos, sin)


def _attn_kernel(q_ref, k_ref, vt_ref, z_ref, lq1_ref, lk1_ref, lq2_ref, lk2_ref, g_ref, o_ref,
                 qq_scr, acc_scr, sa_scr, sb_scr, *, tq, qt, lam_init):
    nq = q_ref.shape[2] // tq
    nk = vt_ref.shape[2]
    tk = vt_ref.shape[4]
    tiles = [slice(c * qt, (c + 1) * qt) for c in range(2 * tq // qt)]
    lane = lax.broadcasted_iota(jnp.int32, (tq, LANES), 1)
    lam = (jnp.exp(jnp.sum(lq1_ref[...] * lk1_ref[...], keepdims=True))
           - jnp.exp(jnp.sum(lq2_ref[...] * lk2_ref[...], keepdims=True)) + lam_init)

    def load_queries(qi):
        rows = pl.ds(pl.multiple_of(qi * tq, tq), tq)
        q = q_ref[0, 0, rows, :].astype(F32)
        qq_scr[0:tq, :] = jnp.where(lane < DH_A, q, 0.0).astype(BF16)
        qq_scr[tq:2 * tq, :] = jnp.where(lane >= DH_A, q, 0.0).astype(BF16)

    def produce(i, s_scr):
        off = pl.multiple_of(i * tk, tk)
        ks = k_ref[0, 0, pl.ds(off, tk), :]
        s = [_dot_nt(ks, qq_scr[t, :]) for t in tiles]
        for t, x in zip(tiles, s):
            s_scr[:, t] = x
        return jnp.concatenate([jnp.max(x, axis=0, keepdims=True) for x in s], axis=1)

    def consume(i, s_scr, cmax, m_old):
        vt = vt_ref[0, 0, i]
        m_new = jnp.maximum(m_old, cmax)
        alpha = jnp.exp2(m_old - m_new)
        p = [jnp.exp2(s_scr[:, t] - m_new[:, t]).astype(BF16) for t in tiles]
        pv = [_dot(vt, x) for x in p]
        for t, x in zip(tiles, pv):
            acc_scr[:, t] = alpha[:, t] * acc_scr[:, t] + x
        return m_new

    def pair(j, carry):
        m, cmax_a = carry
        cmax_b = produce(2 * j + 1, sb_scr)
        m = consume(2 * j, sa_scr, cmax_a, m)
        cmax_a = produce(2 * j + 2, sa_scr)
        m = consume(2 * j + 1, sb_scr, cmax_b, m)
        return m, cmax_a

    m0 = jnp.full((1, 2 * tq), -jnp.inf, F32)
    trips = nk // 2 - 1
    unroll = next(u for u in (3, 2, 1) if trips % u == 0 and (trips // u >= 2 or u == 1))

    def query_tile(qi, cmax_a):
        m, cmax_a = lax.fori_loop(0, trips, pair, (m0, cmax_a), unroll=unroll)
        cmax_b = produce(nk - 1, sb_scr)
        m = consume(nk - 2, sa_scr, cmax_a, m)
        load_queries(jnp.minimum(qi + 1, nq - 1))
        cmax_next = produce(0, sa_scr)
        consume(nk - 1, sb_scr, cmax_b, m)

        rows = pl.ds(pl.multiple_of(qi * tq, tq), tq)
        ot = acc_scr[0:LANES, :] * (1.0 / acc_scr[LANES:LANES + 1, :])
        acc_scr[...] = jnp.zeros(acc_scr.shape, F32)
        o = (ot[:, 0:tq] - lam * ot[:, tq:2 * tq]).T
        y = o * lax.rsqrt(jnp.mean(o * o, axis=-1, keepdims=True) + SUBLN_EPS) * g_ref[...] * (1.0 - lam_init)
        z = z_ref[0, rows, :].astype(F32)
        o_ref[0, rows, :] = (y * (z * _sigmoid(z))).astype(o_ref.dtype)
        return cmax_next

    load_queries(0)
    acc_scr[...] = jnp.zeros(acc_scr.shape, F32)
    lax.fori_loop(0, nq, query_tile, produce(0, sa_scr))


def _diff_attention(qr, kr, vt, p3, z_off, lam_q1, lam_k1, lam_q2, lam_k2, subln_g, lam_init):
    b, h, s, _ = qr.shape
    nk, tk = vt.shape[2], vt.shape[4]
    tq = _tile(s, 512, LANES)
    qt = _tile(2 * tq, 256, LANES)
    assert z_off % LANES == 0 and nk % 2 == 0
    zc = z_off // LANES
    vec = lambda n: pl.BlockSpec((1, n), lambda bi, hi: (0, 0))
    head = pl.BlockSpec((1, 1, s, LANES), lambda bi, hi: (bi, hi, 0, 0))
    kern = functools.partial(_attn_kernel, tq=tq, qt=qt, lam_init=lam_init)
    return pl.pallas_call(
        kern,
        grid=(b, h),
        in_specs=[
            head, head,
            pl.BlockSpec((1, 1, nk, VT_ROWS, tk), lambda bi, hi: (bi, hi, 0, 0, 0)),
            pl.BlockSpec((1, s, LANES), lambda bi, hi: (bi, 0, zc + hi)),
            vec(DH_A), vec(DH_A), vec(DH_A), vec(DH_A), vec(LANES),
        ],
        out_specs=pl.BlockSpec((1, s, LANES), lambda bi, hi: (bi, 0, hi)),
        out_shape=jax.ShapeDtypeStruct((b, s, h * LANES), BF16),
        scratch_shapes=[
            pltpu.VMEM((2 * tq, LANES), BF16),
            pltpu.VMEM((VT_ROWS, 2 * tq), F32),
            pltpu.VMEM((tk, 2 * tq), F32),
            pltpu.VMEM((tk, 2 * tq), F32),
        ],
        compiler_params=_params(("parallel", "arbitrary"), 56),
        name="diff_attn",
    )(qr, kr, vt, p3, lam_q1.reshape(1, DH_A), lam_k1.reshape(1, DH_A), lam_q2.reshape(1, DH_A),
      lam_k2.reshape(1, DH_A), subln_g.reshape(1, LANES))


def _group_sum(x, gmat):
    hi = x.astype(BF16)
    lo = (x - hi.astype(F32)).astype(BF16)
    return _dot(hi, gmat) + _dot(lo, gmat)


def _group_matrix(cw):
    r = lax.broadcasted_iota(jnp.int32, (cw, cw), 0) // N_B
    c = lax.broadcasted_iota(jnp.int32, (cw, cw), 1) // N_B
    return jnp.where(r == c, 1.0, 0.0).astype(BF16)


def _rwkv_prep_kernel(r_ref, k_ref, v_ref, lo_ref, rp_ref, kp_ref, vp_ref, lop_ref, rn_ref, kn_ref, vn_ref,
                      lon_ref, mu_ref, mulo_ref, w0_ref, a0_ref, wl_ref, al_ref, kk_ref, ka_ref, rk_ref,
                      ro_ref, vo_ref, ao_ref, lw_ref, kd_ref, bd_ref, bonus_ref, pw_scr, pa_scr, ext_scr, *, rows):
    ts = r_ref.shape[1]
    halo = rp_ref.shape[1]
    si = pl.program_id(1)
    first = si == 0
    last = si == pl.num_programs(1) - 1

    def edge(ref, drop):
        return jnp.where(drop, 0.0, ref[0].astype(F32))

    def mix(xs, lo_row, n, mu_p, mu_n):
        x = xs[lo_row:lo_row + n]
        prev = pltpu.roll(xs, 1, axis=0)[lo_row:lo_row + n]
        nxt = pltpu.roll(xs, xs.shape[0] - 1, axis=0)[lo_row:lo_row + n]
        return x + mu_p * (prev - x) + mu_n * (nxt - x)

    @pl.when(pl.program_id(2) == 0)
    def _():
        xs = jnp.concatenate([edge(lop_ref, first), lo_ref[0].astype(F32), edge(lon_ref, last)], axis=0)
        lo = mix(xs, halo, ts, mulo_ref[0:1, :], mulo_ref[1:2, :])
        rl = lo.shape[1] // 4
        for d in range(2):
            pw_scr[d] = jnp.tanh(lo[:, d * rl:(d + 1) * rl]).astype(BF16)
            pa_scr[d] = lo[:, (2 + d) * rl:(3 + d) * rl].astype(BF16)

    for idx, (cur, prv, nxt) in enumerate(((r_ref, rp_ref, rn_ref), (k_ref, kp_ref, kn_ref),
                                           (v_ref, vp_ref, vn_ref))):
        ext_scr[idx, 0:halo, :] = edge(prv, first).astype(BF16)
        ext_scr[idx, halo:halo + ts, :] = cur[0]
        ext_scr[idx, halo + ts:, :] = edge(nxt, last).astype(BF16)

    gmat = _group_matrix(r_ref.shape[2])
    ka = ka_ref[...]

    def block(i, carry):
        r0 = pl.multiple_of(i * rows, rows)
        win = pl.ds(r0, rows + 2 * halo)
        out = pl.ds(r0, rows)
        r, k, v = [mix(ext_scr[idx, win, :].astype(F32), halo, rows, mu_ref[0, idx:idx + 1, :],
                       mu_ref[1, idx:idx + 1, :]) for idx in range(3)]
        kkh = k * kk_ref[...]
        kk = kkh * jnp.minimum(lax.rsqrt(_group_sum(kkh * kkh, gmat)), 1e12)
        ro_ref[0, out, :] = r.astype(ro_ref.dtype)
        vo_ref[0, out, :] = v.astype(vo_ref.dtype)
        ao_ref[0, out, :] = (-kk).astype(ao_ref.dtype)
        ksum = jnp.zeros_like(k)
        for d in range(2):
            wl = w0_ref[d:d + 1, :] + _dot(pw_scr[d, out, :], wl_ref[d])
            lw_ref[d, 0, out, :] = -math.exp(-0.5) * _sigmoid(wl)
            a = _sigmoid(a0_ref[d:d + 1, :] + _dot(pa_scr[d, out, :], al_ref[d]))
            kdir = k * (1.0 + (a - 1.0) * ka)
            kd_ref[d, 0, out, :] = kdir.astype(kd_ref.dtype)
            bd_ref[d, 0, out, :] = (kk * a).astype(bd_ref.dtype)
            ksum = ksum + kdir
        bonus_ref[0, out, :] = _group_sum(r * ksum * rk_ref[...], gmat) * v
        return carry

    nblk = ts // rows
    lax.fori_loop(0, nblk, block, 0, unroll=2 if nblk % 2 == 0 else 1)


def _rwkv_prep(p3, s_off, w_b, r_lora, mu_prev, mu_next, w0, w_lora, a0, a_lora, k_k, k_a, r_k):
    b, s, _ = p3.shape
    ts = _tile(s, 512, 16)
    cw = _tile(w_b, 256, LANES)
    lw = 4 * r_lora
    halo = 16
    nblk = s // halo
    assert s_off % cw == 0 and w_b % cw == 0 and (s_off + 3 * w_b) % lw == 0 and ts % halo == 0
    rc, kc, vc, lc = s_off // cw, (s_off + w_b) // cw, (s_off + 2 * w_b) // cw, (s_off + 3 * w_b) // lw
    per = ts // halo

    cur = lambda off: pl.BlockSpec((1, ts, cw), lambda bi, si, ji: (bi, si, off + ji))
    prv = lambda off: pl.BlockSpec((1, halo, cw), lambda bi, si, ji: (bi, jnp.maximum(si * per - 1, 0), off + ji))
    nxt = lambda off: pl.BlockSpec((1, halo, cw),
                                   lambda bi, si, ji: (bi, jnp.minimum((si + 1) * per, nblk - 1), off + ji))
    lo_cur = pl.BlockSpec((1, ts, lw), lambda bi, si, ji: (bi, si, lc))
    lo_prv = pl.BlockSpec((1, halo, lw), lambda bi, si, ji: (bi, jnp.maximum(si * per - 1, 0), lc))
    lo_nxt = pl.BlockSpec((1, halo, lw), lambda bi, si, ji: (bi, jnp.minimum((si + 1) * per, nblk - 1), lc))
    colvec = lambda rows: pl.BlockSpec((rows, cw), lambda bi, si, ji: (0, ji))

    mu_rkv = jnp.stack([mu_prev[:3 * w_b].reshape(3, w_b), mu_next[:3 * w_b].reshape(3, w_b)])
    mu_lo = jnp.stack([mu_prev[3 * w_b:], mu_next[3 * w_b:]])
    tok = pl.BlockSpec((1, ts, cw), lambda bi, si, ji: (bi, si, ji))
    tok2 = pl.BlockSpec((2, 1, ts, cw), lambda bi, si, ji: (0, bi, si, ji))
    shp = lambda dt: jax.ShapeDtypeStruct((b, s, w_b), dt)
    shp2 = lambda dt: jax.ShapeDtypeStruct((2, b, s, w_b), dt)
    rows = _tile(ts, 64, halo)
    return pl.pallas_call(
        functools.partial(_rwkv_prep_kernel, rows=rows),
        grid=(b, s // ts, w_b // cw),
        in_specs=[cur(rc), cur(kc), cur(vc), lo_cur, prv(rc), prv(kc), prv(vc), lo_prv,
                  nxt(rc), nxt(kc), nxt(vc), lo_nxt,
                  pl.BlockSpec((2, 3, cw), lambda bi, si, ji: (0, 0, ji)),
                  pl.BlockSpec((2, lw), lambda bi, si, ji: (0, 0)),
                  colvec(2), colvec(2),
                  pl.BlockSpec((2, r_lora, cw), lambda bi, si, ji: (0, 0, ji)),
                  pl.BlockSpec((2, r_lora, cw), lambda bi, si, ji: (0, 0, ji)),
                  colvec(1), colvec(1), colvec(1)],
        out_specs=[tok, tok, tok, tok2, tok2, tok2, tok],
        out_shape=[shp(BF16), shp(BF16), shp(BF16), shp2(F32), shp2(BF16), shp2(BF16), shp(F32)],
        scratch_shapes=[pltpu.VMEM((2, ts, r_lora), BF16), pltpu.VMEM((2, ts, r_lora), BF16),
                        pltpu.VMEM((3, ts + 2 * halo, cw), BF16)],
        compiler_params=_params(("parallel", "parallel", "arbitrary"), 48),
        name="rwkv_prep",
    )(p3, p3, p3, p3, p3, p3, p3, p3, p3, p3, p3, p3, mu_rkv, mu_lo, w0, a0,
      w_lora.astype(BF16), a_lora.astype(BF16), k_k.reshape(1, w_b), k_a.reshape(1, w_b), r_k.reshape(1, w_b))


def _wkv_kernel(mask_ref, r_ref, a_ref, v_ref, lw_ref, k_ref, b_ref, y_ref, z_scr, *, npairs):
    assert CHUNK == N_B and 2 * N_B == LANES
    c = CHUNK
    ci = pl.program_id(3)

    @pl.when(ci == 0)
    def _():
        z_scr[...] = jnp.zeros(z_scr.shape, F32)

    strict = mask_ref[0, 0]
    incl = mask_ref[0, 1]
    cum = incl[0:c, 0:c].astype(BF16)
    lane = lax.broadcasted_iota(jnp.int32, (c, LANES), 1)
    low = lane < N_B
    row2 = lax.broadcasted_iota(jnp.int32, (LANES, LANES), 0)
    col2 = lax.broadcasted_iota(jnp.int32, (LANES, LANES), 1)
    own = (row2 < c) == (col2 < N_B)

    def stack(x):
        return jnp.concatenate([jnp.where(low, x, 0.0), jnp.where(low, 0.0, x)], axis=0)

    def each(f, *cols):
        return [f(*xs) for xs in zip(*cols)]

    sls = [slice(j * LANES, (j + 1) * LANES) for j in range(npairs)]
    lw = [lw_ref[0, 0, :, sl] for sl in sls]
    r = [r_ref[0, :, sl].astype(F32) for sl in sls]
    a = [a_ref[0, :, sl].astype(F32) for sl in sls]
    v = [v_ref[0, :, sl].astype(F32) for sl in sls]
    k = [k_ref[0, 0, :, sl].astype(F32) for sl in sls]
    b = [b_ref[0, 0, :, sl].astype(F32) for sl in sls]
    zb = [stack(z_scr[j]).astype(BF16) for j in range(npairs)]
    strict_p = strict[0:c, :] + strict[c:, :]
    incl_p = incl[0:c, :] + incl[c:, :]
    eye_p = lax.broadcasted_iota(jnp.int32, (c, LANES), 0) == (lane % N_B)

    def fold(x):
        return x[0:c, :] + x[c:, :]

    hi = each(lambda x: x.astype(BF16), lw)
    lo = each(lambda x, h: (x - h.astype(F32)).astype(BF16), lw, hi)
    cum2 = jnp.concatenate([cum, cum], axis=1)
    lam = each(lambda h, l: _dot(cum2, jnp.concatenate([h, l], axis=0)), hi, lo)
    tot = each(lambda x: jnp.sum(x, axis=0, keepdims=True), lw)
    gdec = each(jnp.exp, tot)
    a_p = each(lambda x, lm, w: x * jnp.exp(lm - w), a, lam, lw)
    r_p = each(lambda x, lm: x * jnp.exp(lm), r, lam)
    a_s = each(stack, a_p)
    e_neg = each(lambda lm: jnp.exp(-lm), lam)
    e_rem = each(lambda t, lm: jnp.exp(t - lm), tot, lam)
    b_s = each(lambda x, e: stack(x * e).astype(BF16), b, e_neg)
    k_s = each(lambda x, e: stack(x * e).astype(BF16), k, e_neg)
    bh_s = each(lambda x, e: stack(x * e).astype(BF16), b, e_rem)
    kh_s = each(lambda x, e: stack(x * e).astype(BF16), k, e_rem)
    v_s = each(lambda x: stack(x).astype(BF16), v)

    m4 = each(lambda x, y, p, q: _dot_nt(jnp.concatenate([x, y], axis=0).astype(BF16),
                                         jnp.concatenate([p, q], axis=0)), a_p, r_p, b_s, k_s)
    lp = each(lambda m: stack(m[0:c, 0:LANES] * strict_p).astype(BF16), m4)
    lak = each(lambda m: stack(m[0:c, LANES:] * strict_p).astype(BF16), m4)
    mrbk = each(lambda m: jnp.concatenate([stack(m[c:, 0:LANES] * incl_p), stack(m[c:, LANES:] * incl_p)],
                                          axis=1).astype(BF16), m4)

    half = LANES // 2
    x = each(lambda p, q, w: p + pltpu.roll(_dot(q, w), half, axis=1), a_s, lak, v_s)
    steps = int(math.log2(c))
    for it in range(steps - 1):
        res = each(lambda p, q: _dot(p, jnp.concatenate([q.astype(BF16), p], axis=1)), lp, x)
        x = each(lambda q, s: q + s[:, 0:LANES], x, res)
        lp = each(lambda s: s[:, LANES:].astype(BF16), res)
    x = each(lambda p, q: q + _dot(p, q.astype(BF16)), lp, x)
    w_s = each(lambda q: jnp.where(own, q, 0.0).astype(BF16), x)
    u_s = each(lambda q: jnp.where(own, pltpu.roll(q, half, axis=1), 0.0).astype(BF16), x)

    zero = jnp.zeros((LANES, LANES), BF16)
    wuv = each(lambda w, u, vv: jnp.concatenate([jnp.concatenate([w, u], axis=1),
                                                 jnp.concatenate([zero, vv], axis=1)], axis=0), w_s, u_s, v_s)
    ry = each(_dot, mrbk, wuv)
    pq = each(lambda p, q, m: _dot_tn(jnp.concatenate([p, q], axis=0), m), bh_s, kh_s, wuv)
    rw = each(lambda p, s: (p + fold(s[:, 0:LANES])).astype(BF16), r_p, ry)
    pt = each(lambda g, s: (jnp.where(eye_p, g, 0.0) + fold(s[:, 0:LANES])).astype(BF16), gdec, pq)

    ys = each(lambda p, z, s: _dot(p, z) + fold(s[:, LANES:]), rw, zb, ry)
    z_new = each(lambda p, z, s: _dot(p, z) + fold(s[:, LANES:]), pt, zb, pq)
    for j, sl in enumerate(sls):
        y_ref[0, 0, :, sl] = ys[j].astype(y_ref.dtype)
        z_scr[j] = z_new[j]


def _wkv_masks():
    i = np.arange(LANES)
    same = (i[:, None] // CHUNK) == (i[None, :] // CHUNK)
    t, s = i[:, None] % CHUNK, i[None, :] % CHUNK
    fwd = np.stack([same & (s < t), same & (s <= t)])
    bwd = np.stack([same & (s > t), same & (s >= t)])
    return jnp.asarray(np.stack([fwd, bwd]).astype(np.float32))


def _wkv_scan(r, a, v, logw, kdir, bdir):
    b, s, w_b = r.shape
    c = CHUNK
    nc = s // c
    gw = _tile(w_b, WKV_GROUP_LANES, LANES)
    npairs = gw // LANES
    chunk_of = lambda d, ci: ci + d * (nc - 1 - 2 * ci)
    shared = pl.BlockSpec((1, c, gw), lambda d, bi, gi, ci: (bi, chunk_of(d, ci), gi))
    perdir = pl.BlockSpec((1, 1, c, gw), lambda d, bi, gi, ci: (d, bi, chunk_of(d, ci), gi))
    kern = functools.partial(_wkv_kernel, npairs=npairs)
    return pl.pallas_call(
        kern,
        grid=(2, b, w_b // gw, nc),
        in_specs=[pl.BlockSpec((1, 2, LANES, LANES), lambda d, bi, gi, ci: (d, 0, 0, 0)),
                  shared, shared, shared, perdir, perdir, perdir],
        out_specs=perdir,
        out_shape=jax.ShapeDtypeStruct((2, b, s, w_b), BF16),
        scratch_shapes=[pltpu.VMEM((npairs, N_B, LANES), F32)],
        compiler_params=_params(("parallel", "parallel", "parallel", "arbitrary"), 32),
        name="wkv_scan",
    )(_wkv_masks(), r, a, v, logw, kdir, bdir)


def _rwkv_out_kernel(y_ref, bonus_ref, z_ref, g_ref, b_ref, o_ref):
    y = y_ref[0, 0].astype(F32) + y_ref[1, 0].astype(F32)
    gmat = _group_matrix(y.shape[1])
    mu = _group_sum(y, gmat) * (1.0 / N_B)
    yc = y - mu
    var = _group_sum(yc * yc, gmat) * (1.0 / N_B)
    yn = yc * lax.rsqrt(var + LNX_EPS) * g_ref[...] + b_ref[...]
    z = z_ref[0].astype(F32)
    o_ref[0] = ((yn + bonus_ref[0]) * (z * _sigmoid(z))).astype(o_ref.dtype)


def _rwkv_out(y2, bonus, p3, z_off, lnx_g, lnx_b):
    _, b, s, w_b = y2.shape
    ts = _tile(s, 512, 16)
    cw = _tile(w_b, 256, LANES)
    assert z_off % cw == 0
    zc = z_off // cw
    vec = pl.BlockSpec((1, cw), lambda bi, si, ji: (0, ji))
    return pl.pallas_call(
        _rwkv_out_kernel,
        grid=(b, s // ts, w_b // cw),
        in_specs=[pl.BlockSpec((2, 1, ts, cw), lambda bi, si, ji: (0, bi, si, ji)),
                  pl.BlockSpec((1, ts, cw), lambda bi, si, ji: (bi, si, ji)),
                  pl.BlockSpec((1, ts, cw), lambda bi, si, ji: (bi, si, zc + ji)),
                  vec, vec],
        out_specs=pl.BlockSpec((1, ts, cw), lambda bi, si, ji: (bi, si, ji)),
        out_shape=jax.ShapeDtypeStruct((b, s, w_b), BF16),
        compiler_params=_params(("parallel", "parallel", "parallel"), 32),
        name="rwkv_out",
    )(y2, bonus, p3, lnx_g.reshape(1, w_b), lnx_b.reshape(1, w_b))


def _merge_kernel(ua_ref, ub_ref, wa_ref, wb_ref, ga_ref, gb_ref, o_ref):
    oa = _dot(ua_ref[...], wa_ref[...])
    ob = _dot(ub_ref[...], wb_ref[...])
    m = _sigmoid(ga_ref[...].astype(F32)) * oa + _sigmoid(gb_ref[...].astype(F32)) * ob
    o_ref[...] = m.astype(o_ref.dtype)


def _merge(ua, ub, w_oa, w_ob, p2, g_off):
    t, w_a = ua.shape
    w_b = ub.shape[1]
    d = w_oa.shape[1]
    tm = _tile(t, 1024, 16)
    tn = _tile(d, 512, LANES)
    assert g_off % tn == 0 and d % tn == 0
    gc = g_off // tn
    nd = d // tn
    return pl.pallas_call(
        _merge_kernel,
        grid=(t // tm, nd),
        in_specs=[pl.BlockSpec((tm, w_a), lambda i, j: (i, 0)),
                  pl.BlockSpec((tm, w_b), lambda i, j: (i, 0)),
                  pl.BlockSpec((w_a, tn), lambda i, j: (0, j)),
                  pl.BlockSpec((w_b, tn), lambda i, j: (0, j)),
                  pl.BlockSpec((tm, tn), lambda i, j: (i, gc + j)),
                  pl.BlockSpec((tm, tn), lambda i, j: (i, gc + nd + j))],
        out_specs=pl.BlockSpec((tm, tn), lambda i, j: (i, j)),
        out_shape=jax.ShapeDtypeStruct((t, d), BF16),
        compiler_params=_params(("parallel", "arbitrary"), 48),
        name="gate_merge",
    )(ua, ub, w_oa, w_ob, p2, p2)


def _out_kernel(m_ref, w_ref, x_ref, o_ref):
    o_ref[...] = x_ref[...] + _dot(m_ref[...], w_ref[...])


def _out_norm_kernel(m_ref, w_ref, x_ref, g_ref, o_ref, res_scr, ss_scr):
    j = pl.program_id(1)
    nj, _, tn = res_scr.shape
    res = x_ref[...] + _dot(m_ref[...], w_ref[...])
    res_scr[j] = res

    @pl.when(j == 0)
    def _():
        ss_scr[...] = jnp.zeros(ss_scr.shape, F32)

    ss_scr[...] += jnp.sum(res * res, axis=-1, keepdims=True)

    @pl.when(j == nj - 1)
    def _():
        scale = lax.rsqrt(ss_scr[...] * (1.0 / (nj * tn)) + NORM_EPS)
        for jj in range(nj):
            cols = slice(jj * tn, (jj + 1) * tn)
            o_ref[:, cols] = res_scr[jj] * scale * g_ref[:, cols]


def _out_proj(m, w_out, x2, final_g, final_norm):
    t, d = x2.shape
    tm = _tile(t, 512, 16)
    tn = _tile(d, 512, 2 * LANES)
    in_specs = [pl.BlockSpec((tm, d), lambda i, j: (i, 0)),
                pl.BlockSpec((d, tn), lambda i, j: (0, j)),
                pl.BlockSpec((tm, tn), lambda i, j: (i, j))]
    if not final_norm:
        return pl.pallas_call(
            _out_kernel,
            grid=(t // tm, d // tn),
            in_specs=in_specs,
            out_specs=pl.BlockSpec((tm, tn), lambda i, j: (i, j)),
            out_shape=jax.ShapeDtypeStruct((t, d), F32),
            compiler_params=_params(("parallel", "arbitrary"), 56),
            name="out_proj",
        )(m, w_out, x2)
    return pl.pallas_call(
        _out_norm_kernel,
        grid=(t // tm, d // tn),
        in_specs=in_specs + [pl.BlockSpec((1, d), lambda i, j: (0, 0))],
        out_specs=pl.BlockSpec((tm, d), lambda i, j: (i, 0)),
        out_shape=jax.ShapeDtypeStruct((t, d), F32),
        scratch_shapes=[pltpu.VMEM((d // tn, tm, tn), F32), pltpu.VMEM((tm, 1), F32)],
        compiler_params=_params(("parallel", "arbitrary"), 56),
        name="out_proj_norm",
    )(m, w_out, x2, final_g.reshape(1, d))


def _lambda_init(layer_idx):
    return 0.8 - 0.6 * math.exp(-0.3 * layer_idx)


def _mixer_layer(x, l, prm, final_g, final_norm):
    b, s, d = x.shape
    w_a = prm["w_oA"][l].shape[0]
    w_b = prm["w_oB"][l].shape[0]
    r_lora = prm["w_lora"].shape[2]
    c_shift = 3 * w_b + 4 * r_lora
    n_in = prm["w_in"].shape[2]
    assert n_in == 3 * w_a + c_shift + w_a + w_b + 2 * d
    assert w_a % LANES == 0 and w_b % LANES == 0 and s % CHUNK == 0
    s_off = 3 * w_a
    za_off = s_off + c_shift
    zb_off = za_off + w_a
    g_off = zb_off + w_b
    lam_init = _lambda_init(l)

    x2 = x.reshape(b * s, d)
    h = _rmsnorm_bf16(x2, prm["norm_g"][l])
    p2 = _matmul_bf16(h, prm["w_in"][l])
    p3 = p2.reshape(b, s, n_in)

    qr, kr, vt = _attn_prepass(p3, w_a)
    ua = _diff_attention(qr, kr, vt, p3, za_off, prm["lam_q1"][l], prm["lam_k1"][l], prm["lam_q2"][l],
                         prm["lam_k2"][l], prm["subln_g"][l], lam_init)

    r, v, a, logw, kdir, bdir, bonus = _rwkv_prep(
        p3, s_off, w_b, r_lora, prm["mu_prev"][l], prm["mu_next"][l], prm["w0"][l], prm["w_lora"][l],
        prm["a0"][l], prm["a_lora"][l], prm["k_k"][l], prm["k_a"][l], prm["r_k"][l])
    y2 = _wkv_scan(r, a, v, logw, kdir, bdir)
    ub = _rwkv_out(y2, bonus, p3, zb_off, prm["lnx_g"][l], prm["lnx_b"][l])

    m = _merge(ua.reshape(b * s, w_a), ub.reshape(b * s, w_b), prm["w_oA"][l].astype(BF16),
               prm["w_oB"][l].astype(BF16), p2, g_off)
    out = _out_proj(m, prm["w_out"][l].astype(BF16), x2, final_g, final_norm)
    return out.reshape(b, s, d)


def kernel(x_prompt, x_sample, norm_g, w_in, mu_prev, mu_next, lam_q1, lam_k1, lam_q2, lam_k2, subln_g, w0,
           w_lora, a0, a_lora, k_k, k_a, r_k, lnx_g, lnx_b, w_oA, w_oB, w_out, final_g):
    prm = dict(norm_g=norm_g, w_in=w_in, mu_prev=mu_prev, mu_next=mu_next, lam_q1=lam_q1, lam_k1=lam_k1,
               lam_q2=lam_q2, lam_k2=lam_k2, subln_g=subln_g, w0=w0, w_lora=w_lora, a0=a0, a_lora=a_lora,
               k_k=k_k, k_a=k_a, r_k=r_k, lnx_g=lnx_g, lnx_b=lnx_b, w_oA=w_oA, w_oB=w_oB, w_out=w_out)
    depth = norm_g.shape[0]

    def trunk(x):
        for l in range(depth):
            x = _mixer_layer(x, l, prm, final_g, final_norm=(l == depth - 1))
        return x

    return (trunk(x_prompt), trunk(x_sample))
```

```python
import functools
import math

import jax
import jax.numpy as jnp
import numpy as np
from jax import lax
from jax.experimental import pallas as pl
from jax.experimental.pallas import tpu as pltpu

F32 = jnp.float32
BF16 = jnp.bfloat16

LANES = 128
BF16_ROWS = 16
MXU_COLS = 256
PROJ_ROWS = 1024
PROJ_COLS = 512
TOKEN_ROWS = 512
NORM_ROWS = 256
CHANNEL_COLS = 256
VMEM_SMALL, VMEM_MID, VMEM_BIG = 32, 48, 56
DH_A = 64
N_B = 64
CHUNK = 64
ROPE_THETA = 10000.0
ATTN_SCALE = DH_A ** -0.5
LOG2_E = math.log2(math.e)
ATTN_KEY_CHUNK = 512
PREPASS_HEADS = 4
WKV_GROUP_LANES = 2048
VT_ROWS = LANES + BF16_ROWS
NORM_EPS = 1e-6
SUBLN_EPS = 1e-5
LNX_EPS = 64e-5
MIB = 2 ** 20


def _tile(n, target, align):
    if n <= target:
        return n
    t = (target // align) * align
    while t >= align:
        if n % t == 0:
            return t
        t -= align
    raise ValueError(f"no tile for {n} (target {target}, align {align})")


def _params(semantics, vmem_mib):
    return pltpu.CompilerParams(dimension_semantics=semantics, vmem_limit_bytes=vmem_mib * MIB)


def _sigmoid(x):
    return 1.0 / (1.0 + jnp.exp(-x))


def _dot(a, b):
    return jnp.dot(a, b, preferred_element_type=F32)


def _dot_nt(a, b):
    return lax.dot_general(a, b, (((1,), (1,)), ((), ())), preferred_element_type=F32)


def _dot_tn(a, b):
    return lax.dot_general(a, b, (((0,), (0,)), ((), ())), preferred_element_type=F32)


def _rmsnorm_kernel(x_ref, g_ref, o_ref):
    x = x_ref[...]
    ms = jnp.mean(x * x, axis=-1, keepdims=True)
    o_ref[...] = (x * lax.rsqrt(ms + NORM_EPS) * g_ref[...]).astype(o_ref.dtype)


def _rmsnorm_bf16(x2, g):
    t, d = x2.shape
    tm = _tile(t, NORM_ROWS, BF16_ROWS)
    return pl.pallas_call(
        _rmsnorm_kernel,
        grid=(t // tm,),
        in_specs=[pl.BlockSpec((tm, d), lambda i: (i, 0)), pl.BlockSpec((1, d), lambda i: (0, 0))],
        out_specs=pl.BlockSpec((tm, d), lambda i: (i, 0)),
        out_shape=jax.ShapeDtypeStruct((t, d), BF16),
        compiler_params=_params(("parallel",), VMEM_SMALL),
        name="rmsnorm_cast",
    )(x2, g.reshape(1, d))


def _matmul_kernel(a_ref, w_ref, o_ref):
    o_ref[...] = _dot(a_ref[...], w_ref[...].astype(BF16)).astype(o_ref.dtype)


def _matmul_bf16(a, w):
    m, k = a.shape
    n = w.shape[1]
    tm = _tile(m, PROJ_ROWS, BF16_ROWS)
    tn = _tile(n, PROJ_COLS, MXU_COLS)
    return pl.pallas_call(
        _matmul_kernel,
        grid=(m // tm, n // tn),
        in_specs=[pl.BlockSpec((tm, k), lambda i, j: (i, 0)), pl.BlockSpec((k, tn), lambda i, j: (0, j))],
        out_specs=pl.BlockSpec((tm, tn), lambda i, j: (i, j)),
        out_shape=jax.ShapeDtypeStruct((m, n), BF16),
        compiler_params=_params(("parallel", "arbitrary"), VMEM_BIG),
        name="in_proj",
    )(a, w)


def _rope_kernel(q_ref, k_ref, v_ref, cos_ref, sin_ref, qo_ref, ko_ref, vo_ref):
    cos = cos_ref[...]
    sin = sin_ref[...]
    lane = lax.broadcasted_iota(jnp.int32, cos.shape, 1)
    first_half = (lane % DH_A) < (DH_A // 2)

    def rope(x):
        partner = jnp.where(first_half,
                            pltpu.roll(x, LANES - DH_A // 2, axis=1),
                            pltpu.roll(x, DH_A // 2, axis=1))
        return x * cos + partner * sin

    for hh in range(qo_ref.shape[1]):
        cols = slice(hh * LANES, (hh + 1) * LANES)
        qo_ref[0, hh] = (rope(q_ref[0, :, cols].astype(F32)) * (ATTN_SCALE * LOG2_E)).astype(qo_ref.dtype)
        ko_ref[0, hh] = rope(k_ref[0, :, cols].astype(F32)).astype(ko_ref.dtype)
        vo_ref[0, hh, 0, 0:LANES, :] = v_ref[0, :, cols].astype(F32).T.astype(vo_ref.dtype)
        vo_ref[0, hh, 0, LANES:, :] = jnp.ones((vo_ref.shape[3] - LANES, vo_ref.shape[4]), vo_ref.dtype)


def _rope_tables(s):
    half = DH_A // 2
    inv = 1.0 / (ROPE_THETA ** (jnp.arange(0, DH_A, 2, dtype=F32) / DH_A))
    ang = jnp.arange(s, dtype=F32)[:, None] * inv[None, :]
    cos, sin = jnp.cos(ang), jnp.sin(ang)
    reps = LANES // half
    return jnp.tile(cos, (1, reps)), jnp.tile(jnp.concatenate([-sin, sin], axis=-1), (1, reps // 2))


def _attn_prepass(p3, w_a):
    b, s, _ = p3.shape
    h = w_a // LANES
    ts = _tile(s // 2, ATTN_KEY_CHUNK, LANES)
    cos, sin = _rope_tables(s)
    head_out = jax.ShapeDtypeStruct((b, h, s, LANES), BF16)
    vt_out = jax.ShapeDtypeStruct((b, h, s // ts, VT_ROWS, ts), BF16)
    hp = _tile(h, PREPASS_HEADS, 1)
    hg = h // hp
    col = lambda off: pl.BlockSpec((1, ts, hp * LANES), lambda bi, si, hi: (bi, si, off + hi))
    tab = pl.BlockSpec((ts, LANES), lambda bi, si, hi: (si, 0))
    out = pl.BlockSpec((1, hp, ts, LANES), lambda bi, si, hi: (bi, hi, si, 0))
    out_t = pl.BlockSpec((1, hp, 1, VT_ROWS, ts), lambda bi, si, hi: (bi, hi, si, 0, 0))
    return pl.pallas_call(
        _rope_kernel,
        grid=(b, s // ts, hg),
        in_specs=[col(0), col(hg), col(2 * hg), tab, tab],
        out_specs=[out, out, out_t],
        out_shape=[head_out, head_out, vt_out],
        compiler_params=_params(("parallel", "parallel", "arbitrary"), VMEM_SMALL),
        name="attn_prepass",
    )(p3, p3, p3, cos, sin)


def _attn_kernel(q_ref, k_ref, vt_ref, z_ref, lq1_ref, lk1_ref, lq2_ref, lk2_ref, g_ref, o_ref,
                 qq_scr, acc_scr, sa_scr, sb_scr, *, tq, qt, lam_init):
    nq = q_ref.shape[2] // tq
    nk = vt_ref.shape[2]
    tk = vt_ref.shape[4]
    tiles = [slice(c * qt, (c + 1) * qt) for c in range(2 * tq // qt)]
    lane = lax.broadcasted_iota(jnp.int32, (tq, LANES), 1)
    lam = (jnp.exp(jnp.sum(lq1_ref[...] * lk1_ref[...], keepdims=True))
           - jnp.exp(jnp.sum(lq2_ref[...] * lk2_ref[...], keepdims=True)) + lam_init)

    def load_queries(qi):
        rows = pl.ds(pl.multiple_of(qi * tq, tq), tq)
        q = q_ref[0, 0, rows, :].astype(F32)
        qq_scr[0:tq, :] = jnp.where(lane < DH_A, q, 0.0).astype(BF16)
        qq_scr[tq:2 * tq, :] = jnp.where(lane >= DH_A, q, 0.0).astype(BF16)

    def produce(i, s_scr):
        off = pl.multiple_of(i * tk, tk)
        ks = k_ref[0, 0, pl.ds(off, tk), :]
        s = [_dot_nt(ks, qq_scr[t, :]) for t in tiles]
        for t, x in zip(tiles, s):
            s_scr[:, t] = x
        return jnp.concatenate([jnp.max(x, axis=0, keepdims=True) for x in s], axis=1)

    def consume(i, s_scr, cmax, m_old):
        vt = vt_ref[0, 0, i]
        m_new = jnp.maximum(m_old, cmax)
        alpha = jnp.exp2(m_old - m_new)
        p = [jnp.exp2(s_scr[:, t] - m_new[:, t]).astype(BF16) for t in tiles]
        pv = [_dot(vt, x) for x in p]
        for t, x in zip(tiles, pv):
            acc_scr[:, t] = alpha[:, t] * acc_scr[:, t] + x
        return m_new

    def pair(j, carry):
        m, cmax_a = carry
        cmax_b = produce(2 * j + 1, sb_scr)
        m = consume(2 * j, sa_scr, cmax_a, m)
        cmax_a = produce(2 * j + 2, sa_scr)
        m = consume(2 * j + 1, sb_scr, cmax_b, m)
        return m, cmax_a

    m0 = jnp.full((1, 2 * tq), -jnp.inf, F32)
    trips = nk // 2 - 1
    unroll = next(u for u in (5, 3, 2, 1) if trips % u == 0 and (trips // u >= 2 or u == 1))

    def query_tile(qi, cmax_a):
        m, cmax_a = lax.fori_loop(0, trips, pair, (m0, cmax_a), unroll=unroll)
        cmax_b = produce(nk - 1, sb_scr)
        m = consume(nk - 2, sa_scr, cmax_a, m)
        load_queries(jnp.minimum(qi + 1, nq - 1))
        cmax_next = produce(0, sa_scr)
        consume(nk - 1, sb_scr, cmax_b, m)

        rows = pl.ds(pl.multiple_of(qi * tq, tq), tq)
        ot = acc_scr[0:LANES, :] * (1.0 / acc_scr[LANES:LANES + 1, :])
        acc_scr[...] = jnp.zeros(acc_scr.shape, F32)
        o = (ot[:, 0:tq] - lam * ot[:, tq:2 * tq]).T
        y = o * lax.rsqrt(jnp.mean(o * o, axis=-1, keepdims=True) + SUBLN_EPS) * g_ref[...] * (1.0 - lam_init)
        z = z_ref[0, rows, :].astype(F32)
        o_ref[0, rows, :] = (y * (z * _sigmoid(z))).astype(o_ref.dtype)
        return cmax_next

    load_queries(0)
    acc_scr[...] = jnp.zeros(acc_scr.shape, F32)
    lax.fori_loop(0, nq, query_tile, produce(0, sa_scr))


def _diff_attention(qr, kr, vt, p3, z_off, lam_q1, lam_k1, lam_q2, lam_k2, subln_g, lam_init):
    b, h, s, _ = qr.shape
    nk, tk = vt.shape[2], vt.shape[4]
    tq = _tile(s, TOKEN_ROWS, LANES)
    qt = _tile(2 * tq, CHANNEL_COLS, LANES)
    assert z_off % LANES == 0 and nk % 2 == 0
    zc = z_off // LANES
    vec = lambda n: pl.BlockSpec((1, n), lambda bi, hi: (0, 0))
    head = pl.BlockSpec((1, 1, s, LANES), lambda bi, hi: (bi, hi, 0, 0))
    kern = functools.partial(_attn_kernel, tq=tq, qt=qt, lam_init=lam_init)
    return pl.pallas_call(
        kern,
        grid=(b, h),
        in_specs=[
            head, head,
            pl.BlockSpec((1, 1, nk, VT_ROWS, tk), lambda bi, hi: (bi, hi, 0, 0, 0)),
            pl.BlockSpec((1, s, LANES), lambda bi, hi: (bi, 0, zc + hi)),
            vec(DH_A), vec(DH_A), vec(DH_A), vec(DH_A), vec(LANES),
        ],
        out_specs=pl.BlockSpec((1, s, LANES), lambda bi, hi: (bi, 0, hi)),
        out_shape=jax.ShapeDtypeStruct((b, s, h * LANES), BF16),
        scratch_shapes=[
            pltpu.VMEM((2 * tq, LANES), BF16),
            pltpu.VMEM((VT_ROWS, 2 * tq), F32),
            pltpu.VMEM((tk, 2 * tq), F32),
            pltpu.VMEM((tk, 2 * tq), F32),
        ],
        compiler_params=_params(("parallel", "arbitrary"), VMEM_BIG),
        name="diff_attn",
    )(qr, kr, vt, p3, lam_q1.reshape(1, DH_A), lam_k1.reshape(1, DH_A), lam_q2.reshape(1, DH_A),
      lam_k2.reshape(1, DH_A), subln_g.reshape(1, LANES))


def _group_sum(x, gmat):
    hi = x.astype(BF16)
    lo = (x - hi.astype(F32)).astype(BF16)
    return _dot(hi, gmat) + _dot(lo, gmat)


def _group_matrix(cw):
    r = lax.broadcasted_iota(jnp.int32, (cw, cw), 0) // N_B
    c = lax.broadcasted_iota(jnp.int32, (cw, cw), 1) // N_B
    return jnp.where(r == c, 1.0, 0.0).astype(BF16)


def _rwkv_prep_kernel(r_ref, k_ref, v_ref, lo_ref, rp_ref, kp_ref, vp_ref, lop_ref, rn_ref, kn_ref, vn_ref,
                      lon_ref, mu_ref, mulo_ref, w0_ref, a0_ref, wl_ref, al_ref, kk_ref, ka_ref, rk_ref,
                      ro_ref, vo_ref, ao_ref, lw_ref, kd_ref, bd_ref, bonus_ref, pw_scr, pa_scr):
    ts = r_ref.shape[1]
    si = pl.program_id(1)
    first = si == 0
    last = si == pl.num_programs(1) - 1

    def shift(cur_ref, prev_ref, next_ref, mu_p, mu_n):
        x = cur_ref[0].astype(F32)
        hp = prev_ref[0].astype(F32)
        hn = next_ref[0].astype(F32)
        pr = jnp.where(first, 0.0, hp[hp.shape[0] - 1:hp.shape[0], :])
        nx = jnp.where(last, 0.0, hn[0:1, :])
        row = lax.broadcasted_iota(jnp.int32, x.shape, 0)
        prev = jnp.where(row == 0, pr, pltpu.roll(x, 1, axis=0))
        nxt = jnp.where(row == ts - 1, nx, pltpu.roll(x, ts - 1, axis=0))
        return x + mu_p * (prev - x) + mu_n * (nxt - x)

    r = shift(r_ref, rp_ref, rn_ref, mu_ref[0, 0:1, :], mu_ref[1, 0:1, :])
    k = shift(k_ref, kp_ref, kn_ref, mu_ref[0, 1:2, :], mu_ref[1, 1:2, :])
    v = shift(v_ref, vp_ref, vn_ref, mu_ref[0, 2:3, :], mu_ref[1, 2:3, :])

    @pl.when(pl.program_id(2) == 0)
    def _():
        lo = shift(lo_ref, lop_ref, lon_ref, mulo_ref[0:1, :], mulo_ref[1:2, :])
        rl = lo.shape[1] // 4
        for d in range(2):
            pw_scr[d] = jnp.tanh(lo[:, d * rl:(d + 1) * rl]).astype(BF16)
            pa_scr[d] = lo[:, (2 + d) * rl:(3 + d) * rl].astype(BF16)

    cw = r.shape[1]
    gmat = _group_matrix(cw)
    kkh = k * kk_ref[...]
    nrm = jnp.sqrt(_group_sum(kkh * kkh, gmat))
    kk = kkh / jnp.maximum(nrm, 1e-12)
    ka = ka_ref[...]
    ro_ref[0] = r.astype(ro_ref.dtype)
    vo_ref[0] = v.astype(vo_ref.dtype)
    ao_ref[0] = (-kk).astype(ao_ref.dtype)
    ksum = jnp.zeros_like(k)
    for d in range(2):
        pw = pw_scr[d]
        pa = pa_scr[d]
        wl = w0_ref[d:d + 1, :] + _dot(pw, wl_ref[d])
        lw_ref[d, 0] = -math.exp(-0.5) * _sigmoid(wl)
        a = _sigmoid(a0_ref[d:d + 1, :] + _dot(pa, al_ref[d]))
        kdir = k * (1.0 + (a - 1.0) * ka)
        kd_ref[d, 0] = kdir.astype(kd_ref.dtype)
        bd_ref[d, 0] = (kk * a).astype(bd_ref.dtype)
        ksum = ksum + kdir
    bonus_ref[0] = _group_sum(r * ksum * rk_ref[...], gmat) * v


def _rwkv_prep(p3, s_off, w_b, r_lora, mu_prev, mu_next, w0, w_lora, a0, a_lora, k_k, k_a, r_k):
    b, s, _ = p3.shape
    ts = _tile(s, TOKEN_ROWS, BF16_ROWS)
    cw = _tile(w_b, CHANNEL_COLS, LANES)
    lw = 4 * r_lora
    halo = BF16_ROWS
    nblk = s // halo
    assert s_off % cw == 0 and w_b % cw == 0 and (s_off + 3 * w_b) % lw == 0 and ts % halo == 0
    rc, kc, vc, lc = s_off // cw, (s_off + w_b) // cw, (s_off + 2 * w_b) // cw, (s_off + 3 * w_b) // lw
    per = ts // halo

    cur = lambda off: pl.BlockSpec((1, ts, cw), lambda bi, si, ji: (bi, si, off + ji))
    prv = lambda off: pl.BlockSpec((1, halo, cw), lambda bi, si, ji: (bi, jnp.maximum(si * per - 1, 0), off + ji))
    nxt = lambda off: pl.BlockSpec((1, halo, cw),
                                   lambda bi, si, ji: (bi, jnp.minimum((si + 1) * per, nblk - 1), off + ji))
    lo_cur = pl.BlockSpec((1, ts, lw), lambda bi, si, ji: (bi, si, lc))
    lo_prv = pl.BlockSpec((1, halo, lw), lambda bi, si, ji: (bi, jnp.maximum(si * per - 1, 0), lc))
    lo_nxt = pl.BlockSpec((1, halo, lw), lambda bi, si, ji: (bi, jnp.minimum((si + 1) * per, nblk - 1), lc))
    colvec = lambda rows: pl.BlockSpec((rows, cw), lambda bi, si, ji: (0, ji))

    mu_rkv = jnp.stack([mu_prev[:3 * w_b].reshape(3, w_b), mu_next[:3 * w_b].reshape(3, w_b)])
    mu_lo = jnp.stack([mu_prev[3 * w_b:], mu_next[3 * w_b:]])
    tok = pl.BlockSpec((1, ts, cw), lambda bi, si, ji: (bi, si, ji))
    tok2 = pl.BlockSpec((2, 1, ts, cw), lambda bi, si, ji: (0, bi, si, ji))
    shp = lambda dt: jax.ShapeDtypeStruct((b, s, w_b), dt)
    shp2 = lambda dt: jax.ShapeDtypeStruct((2, b, s, w_b), dt)
    return pl.pallas_call(
        _rwkv_prep_kernel,
        grid=(b, s // ts, w_b // cw),
        in_specs=[cur(rc), cur(kc), cur(vc), lo_cur, prv(rc), prv(kc), prv(vc), lo_prv,
                  nxt(rc), nxt(kc), nxt(vc), lo_nxt,
                  pl.BlockSpec((2, 3, cw), lambda bi, si, ji: (0, 0, ji)),
                  pl.BlockSpec((2, lw), lambda bi, si, ji: (0, 0)),
                  colvec(2), colvec(2),
                  pl.BlockSpec((2, r_lora, cw), lambda bi, si, ji: (0, 0, ji)),
                  pl.BlockSpec((2, r_lora, cw), lambda bi, si, ji: (0, 0, ji)),
                  colvec(1), colvec(1), colvec(1)],
        out_specs=[tok, tok, tok, tok2, tok2, tok2, tok],
        out_shape=[shp(BF16), shp(BF16), shp(BF16), shp2(F32), shp2(BF16), shp2(BF16), shp(F32)],
        scratch_shapes=[pltpu.VMEM((2, ts, r_lora), BF16), pltpu.VMEM((2, ts, r_lora), BF16)],
        compiler_params=_params(("parallel", "parallel", "arbitrary"), VMEM_MID),
        name="rwkv_prep",
    )(p3, p3, p3, p3, p3, p3, p3, p3, p3, p3, p3, p3, mu_rkv, mu_lo, w0, a0,
      w_lora.astype(BF16), a_lora.astype(BF16), k_k.reshape(1, w_b), k_a.reshape(1, w_b), r_k.reshape(1, w_b))


def _wkv_kernel(mask_ref, r_ref, a_ref, v_ref, lw_ref, k_ref, b_ref, y_ref, z_scr, *, npairs):
    assert CHUNK == N_B and 2 * N_B == LANES
    c = CHUNK
    ci = pl.program_id(3)

    @pl.when(ci == 0)
    def _():
        z_scr[...] = jnp.zeros(z_scr.shape, F32)

    strict = mask_ref[0, 0]
    incl = mask_ref[0, 1]
    cum = incl[0:c, 0:c].astype(BF16)
    lane = lax.broadcasted_iota(jnp.int32, (c, LANES), 1)
    low = lane < N_B
    row2 = lax.broadcasted_iota(jnp.int32, (LANES, LANES), 0)
    col2 = lax.broadcasted_iota(jnp.int32, (LANES, LANES), 1)
    own = (row2 < c) == (col2 < N_B)

    def stack(x):
        return jnp.concatenate([jnp.where(low, x, 0.0), jnp.where(low, 0.0, x)], axis=0)

    def each(f, *cols):
        return [f(*xs) for xs in zip(*cols)]

    sls = [slice(j * LANES, (j + 1) * LANES) for j in range(npairs)]
    lw = [lw_ref[0, 0, :, sl] for sl in sls]
    r = [r_ref[0, :, sl].astype(F32) for sl in sls]
    a = [a_ref[0, :, sl].astype(F32) for sl in sls]
    v = [v_ref[0, :, sl].astype(F32) for sl in sls]
    k = [k_ref[0, 0, :, sl].astype(F32) for sl in sls]
    b = [b_ref[0, 0, :, sl].astype(F32) for sl in sls]
    zb = [stack(z_scr[j]).astype(BF16) for j in range(npairs)]
    strict_p = strict[0:c, :] + strict[c:, :]
    incl_p = incl[0:c, :] + incl[c:, :]
    eye_p = lax.broadcasted_iota(jnp.int32, (c, LANES), 0) == (lane % N_B)

    def fold(x):
        return x[0:c, :] + x[c:, :]

    hi = each(lambda x: x.astype(BF16), lw)
    lo = each(lambda x, h: (x - h.astype(F32)).astype(BF16), lw, hi)
    cum2 = jnp.concatenate([cum, cum], axis=1)
    lam = each(lambda h, l: _dot(cum2, jnp.concatenate([h, l], axis=0)), hi, lo)
    tot = each(lambda x: jnp.sum(x, axis=0, keepdims=True), lw)
    gdec = each(jnp.exp, tot)
    a_p = each(lambda x, lm, w: x * jnp.exp(lm - w), a, lam, lw)
    r_p = each(lambda x, lm: x * jnp.exp(lm), r, lam)
    a_s = each(stack, a_p)
    e_neg = each(lambda lm: jnp.exp(-lm), lam)
    e_rem = each(lambda t, lm: jnp.exp(t - lm), tot, lam)
    b_s = each(lambda x, e: stack(x * e).astype(BF16), b, e_neg)
    k_s = each(lambda x, e: stack(x * e).astype(BF16), k, e_neg)
    bh_s = each(lambda x, e: stack(x * e).astype(BF16), b, e_rem)
    kh_s = each(lambda x, e: stack(x * e).astype(BF16), k, e_rem)
    v_s = each(lambda x: stack(x).astype(BF16), v)

    m4 = each(lambda x, y, p, q: _dot_nt(jnp.concatenate([x, y], axis=0).astype(BF16),
                                         jnp.concatenate([p, q], axis=0)), a_p, r_p, b_s, k_s)
    lp = each(lambda m: stack(m[0:c, 0:LANES] * strict_p).astype(BF16), m4)
    lak = each(lambda m: stack(m[0:c, LANES:] * strict_p).astype(BF16), m4)
    mrbk = each(lambda m: jnp.concatenate([stack(m[c:, 0:LANES] * incl_p), stack(m[c:, LANES:] * incl_p)],
                                          axis=1).astype(BF16), m4)

    half = LANES // 2
    x = each(lambda p, q, w: p + pltpu.roll(_dot(q, w), half, axis=1), a_s, lak, v_s)
    steps = int(math.log2(c))
    for it in range(steps - 1):
        res = each(lambda p, q: _dot(p, jnp.concatenate([q.astype(BF16), p], axis=1)), lp, x)
        x = each(lambda q, s: q + s[:, 0:LANES], x, res)
        lp = each(lambda s: s[:, LANES:].astype(BF16), res)
    x = each(lambda p, q: q + _dot(p, q.astype(BF16)), lp, x)
    w_s = each(lambda q: jnp.where(own, q, 0.0).astype(BF16), x)
    u_s = each(lambda q: jnp.where(own, pltpu.roll(q, half, axis=1), 0.0).astype(BF16), x)

    zero = jnp.zeros((LANES, LANES), BF16)
    wuv = each(lambda w, u, vv: jnp.concatenate([jnp.concatenate([w, u], axis=1),
                                                 jnp.concatenate([zero, vv], axis=1)], axis=0), w_s, u_s, v_s)
    ry = each(_dot, mrbk, wuv)
    pq = each(lambda p, q, m: _dot_tn(jnp.concatenate([p, q], axis=0), m), bh_s, kh_s, wuv)
    rw = each(lambda p, s: (p + fold(s[:, 0:LANES])).astype(BF16), r_p, ry)
    pt = each(lambda g, s: (jnp.where(eye_p, g, 0.0) + fold(s[:, 0:LANES])).astype(BF16), gdec, pq)

    ys = each(lambda p, z, s: _dot(p, z) + fold(s[:, LANES:]), rw, zb, ry)
    z_new = each(lambda p, z, s: _dot(p, z) + fold(s[:, LANES:]), pt, zb, pq)
    for j, sl in enumerate(sls):
        y_ref[0, 0, :, sl] = ys[j].astype(y_ref.dtype)
        z_scr[j] = z_new[j]


def _wkv_masks():
    i = np.arange(LANES)
    same = (i[:, None] // CHUNK) == (i[None, :] // CHUNK)
    t, s = i[:, None] % CHUNK, i[None, :] % CHUNK
    fwd = np.stack([same & (s < t), same & (s <= t)])
    bwd = np.stack([same & (s > t), same & (s >= t)])
    return jnp.asarray(np.stack([fwd, bwd]).astype(np.float32))


def _wkv_scan(r, a, v, logw, kdir, bdir):
    b, s, w_b = r.shape
    c = CHUNK
    nc = s // c
    gw = _tile(w_b, WKV_GROUP_LANES, LANES)
    npairs = gw // LANES
    chunk_of = lambda d, ci: ci + d * (nc - 1 - 2 * ci)
    shared = pl.BlockSpec((1, c, gw), lambda d, bi, gi, ci: (bi, chunk_of(d, ci), gi))
    perdir = pl.BlockSpec((1, 1, c, gw), lambda d, bi, gi, ci: (d, bi, chunk_of(d, ci), gi))
    kern = functools.partial(_wkv_kernel, npairs=npairs)
    return pl.pallas_call(
        kern,
        grid=(2, b, w_b // gw, nc),
        in_specs=[pl.BlockSpec((1, 2, LANES, LANES), lambda d, bi, gi, ci: (d, 0, 0, 0)),
                  shared, shared, shared, perdir, perdir, perdir],
        out_specs=perdir,
        out_shape=jax.ShapeDtypeStruct((2, b, s, w_b), BF16),
        scratch_shapes=[pltpu.VMEM((npairs, N_B, LANES), F32)],
        compiler_params=_params(("parallel", "parallel", "parallel", "arbitrary"), VMEM_SMALL),
        name="wkv_scan",
    )(_wkv_masks(), r, a, v, logw, kdir, bdir)


def _rwkv_out_kernel(y_ref, bonus_ref, z_ref, g_ref, b_ref, o_ref):
    y = y_ref[0, 0].astype(F32) + y_ref[1, 0].astype(F32)
    gmat = _group_matrix(y.shape[1])
    mu = _group_sum(y, gmat) * (1.0 / N_B)
    yc = y - mu
    var = _group_sum(yc * yc, gmat) * (1.0 / N_B)
    yn = yc * lax.rsqrt(var + LNX_EPS) * g_ref[...] + b_ref[...]
    z = z_ref[0].astype(F32)
    o_ref[0] = ((yn + bonus_ref[0]) * (z * _sigmoid(z))).astype(o_ref.dtype)


def _rwkv_out(y2, bonus, p3, z_off, lnx_g, lnx_b):
    _, b, s, w_b = y2.shape
    ts = _tile(s, TOKEN_ROWS, BF16_ROWS)
    cw = _tile(w_b, CHANNEL_COLS, LANES)
    assert z_off % cw == 0
    zc = z_off // cw
    vec = pl.BlockSpec((1, cw), lambda bi, si, ji: (0, ji))
    return pl.pallas_call(
        _rwkv_out_kernel,
        grid=(b, s // ts, w_b // cw),
        in_specs=[pl.BlockSpec((2, 1, ts, cw), lambda bi, si, ji: (0, bi, si, ji)),
                  pl.BlockSpec((1, ts, cw), lambda bi, si, ji: (bi, si, ji)),
                  pl.BlockSpec((1, ts, cw), lambda bi, si, ji: (bi, si, zc + ji)),
                  vec, vec],
        out_specs=pl.BlockSpec((1, ts, cw), lambda bi, si, ji: (bi, si, ji)),
        out_shape=jax.ShapeDtypeStruct((b, s, w_b), BF16),
        compiler_params=_params(("parallel", "parallel", "parallel"), VMEM_SMALL),
        name="rwkv_out",
    )(y2, bonus, p3, lnx_g.reshape(1, w_b), lnx_b.reshape(1, w_b))


def _merge_kernel(ua_ref, ub_ref, wa_ref, wb_ref, ga_ref, gb_ref, o_ref):
    oa = _dot(ua_ref[...], wa_ref[...])
    ob = _dot(ub_ref[...], wb_ref[...])
    m = _sigmoid(ga_ref[...].astype(F32)) * oa + _sigmoid(gb_ref[...].astype(F32)) * ob
    o_ref[...] = m.astype(o_ref.dtype)


def _merge(ua, ub, w_oa, w_ob, p2, g_off):
    t, w_a = ua.shape
    w_b = ub.shape[1]
    d = w_oa.shape[1]
    tm = _tile(t, PROJ_ROWS, BF16_ROWS)
    tn = _tile(d, PROJ_COLS, MXU_COLS)
    assert g_off % tn == 0 and d % tn == 0
    gc = g_off // tn
    nd = d // tn
    return pl.pallas_call(
        _merge_kernel,
        grid=(t // tm, nd),
        in_specs=[pl.BlockSpec((tm, w_a), lambda i, j: (i, 0)),
                  pl.BlockSpec((tm, w_b), lambda i, j: (i, 0)),
                  pl.BlockSpec((w_a, tn), lambda i, j: (0, j)),
                  pl.BlockSpec((w_b, tn), lambda i, j: (0, j)),
                  pl.BlockSpec((tm, tn), lambda i, j: (i, gc + j)),
                  pl.BlockSpec((tm, tn), lambda i, j: (i, gc + nd + j))],
        out_specs=pl.BlockSpec((tm, tn), lambda i, j: (i, j)),
        out_shape=jax.ShapeDtypeStruct((t, d), BF16),
        compiler_params=_params(("parallel", "arbitrary"), VMEM_MID),
        name="gate_merge",
    )(ua, ub, w_oa, w_ob, p2, p2)


def _out_kernel(m_ref, w_ref, x_ref, o_ref):
    o_ref[...] = x_ref[...] + _dot(m_ref[...], w_ref[...])


def _out_norm_kernel(m_ref, w_ref, x_ref, g_ref, o_ref, res_scr, ss_scr):
    j = pl.program_id(1)
    nj, _, tn = res_scr.shape
    res = x_ref[...] + _dot(m_ref[...], w_ref[...])
    res_scr[j] = res

    @pl.when(j == 0)
    def _():
        ss_scr[...] = jnp.zeros(ss_scr.shape, F32)

    ss_scr[...] += jnp.sum(res * res, axis=-1, keepdims=True)

    @pl.when(j == nj - 1)
    def _():
        scale = lax.rsqrt(ss_scr[...] * (1.0 / (nj * tn)) + NORM_EPS)
        for jj in range(nj):
            cols = slice(jj * tn, (jj + 1) * tn)
            o_ref[:, cols] = res_scr[jj] * scale * g_ref[:, cols]


def _out_proj(m, w_out, x2, final_g, final_norm):
    t, d = x2.shape
    tm = _tile(t, TOKEN_ROWS, BF16_ROWS)
    tn = _tile(d, PROJ_COLS, MXU_COLS)
    in_specs = [pl.BlockSpec((tm, d), lambda i, j: (i, 0)),
                pl.BlockSpec((d, tn), lambda i, j: (0, j)),
                pl.BlockSpec((tm, tn), lambda i, j: (i, j))]
    if not final_norm:
        return pl.pallas_call(
            _out_kernel,
            grid=(t // tm, d // tn),
            in_specs=in_specs,
            out_specs=pl.BlockSpec((tm, tn), lambda i, j: (i, j)),
            out_shape=jax.ShapeDtypeStruct((t, d), F32),
            compiler_params=_params(("parallel", "arbitrary"), VMEM_BIG),
            name="out_proj",
        )(m, w_out, x2)
    return pl.pallas_call(
        _out_norm_kernel,
        grid=(t // tm, d // tn),
        in_specs=in_specs + [pl.BlockSpec((1, d), lambda i, j: (0, 0))],
        out_specs=pl.BlockSpec((tm, d), lambda i, j: (i, 0)),
        out_shape=jax.ShapeDtypeStruct((t, d), F32),
        scratch_shapes=[pltpu.VMEM((d // tn, tm, tn), F32), pltpu.VMEM((tm, 1), F32)],
        compiler_params=_params(("parallel", "arbitrary"), VMEM_BIG),
        name="out_proj_norm",
    )(m, w_out, x2, final_g.reshape(1, d))


def _lambda_init(layer_idx):
    return 0.8 - 0.6 * math.exp(-0.3 * layer_idx)


def _mixer_layer(x, l, prm, final_g, final_norm):
    b, s, d = x.shape
    w_a = prm["w_oA"][l].shape[0]
    w_b = prm["w_oB"][l].shape[0]
    r_lora = prm["w_lora"].shape[2]
    c_shift = 3 * w_b + 4 * r_lora
    n_in = prm["w_in"].shape[2]
    assert n_in == 3 * w_a + c_shift + w_a + w_b + 2 * d
    assert w_a % LANES == 0 and w_b % LANES == 0 and s % CHUNK == 0
    s_off = 3 * w_a
    za_off = s_off + c_shift
    zb_off = za_off + w_a
    g_off = zb_off + w_b
    lam_init = _lambda_init(l)

    x2 = x.reshape(b * s, d)
    h = _rmsnorm_bf16(x2, prm["norm_g"][l])
    p2 = _matmul_bf16(h, prm["w_in"][l])
    p3 = p2.reshape(b, s, n_in)

    qr, kr, vt = _attn_prepass(p3, w_a)
    ua = _diff_attention(qr, kr, vt, p3, za_off, prm["lam_q1"][l], prm["lam_k1"][l], prm["lam_q2"][l],
                         prm["lam_k2"][l], prm["subln_g"][l], lam_init)

    r, v, a, logw, kdir, bdir, bonus = _rwkv_prep(
        p3, s_off, w_b, r_lora, prm["mu_prev"][l], prm["mu_next"][l], prm["w0"][l], prm["w_lora"][l],
        prm["a0"][l], prm["a_lora"][l], prm["k_k"][l], prm["k_a"][l], prm["r_k"][l])
    y2 = _wkv_scan(r, a, v, logw, kdir, bdir)
    ub = _rwkv_out(y2, bonus, p3, zb_off, prm["lnx_g"][l], prm["lnx_b"][l])

    m = _merge(ua.reshape(b * s, w_a), ub.reshape(b * s, w_b), prm["w_oA"][l].astype(BF16),
               prm["w_oB"][l].astype(BF16), p2, g_off)
    out = _out_proj(m, prm["w_out"][l].astype(BF16), x2, final_g, final_norm)
    return out.reshape(b, s, d)


def kernel(x_prompt, x_sample, norm_g, w_in, mu_prev, mu_next, lam_q1, lam_k1, lam_q2, lam_k2, subln_g, w0,
           w_lora, a0, a_lora, k_k, k_a, r_k, lnx_g, lnx_b, w_oA, w_oB, w_out, final_g):
    prm = dict(norm_g=norm_g, w_in=w_in, mu_prev=mu_prev, mu_next=mu_next, lam_q1=lam_q1, lam_k1=lam_k1,
               lam_q2=lam_q2, lam_k2=lam_k2, subln_g=subln_g, w0=w0, w_lora=w_lora, a0=a0, a_lora=a_lora,
               k_k=k_k, k_a=k_a, r_k=r_k, lnx_g=lnx_g, lnx_b=lnx_b, w_oA=w_oA, w_oB=w_oB, w_out=w_out)
    depth = norm_g.shape[0]

    def trunk(x):
        for l in range(depth):
            x = _mixer_layer(x, l, prm, final_g, final_norm=(l == depth - 1))
        return x

    return (trunk(x_prompt), trunk(x_sample))
```

```python
import functools
import math

import jax
import jax.numpy as jnp
import numpy as np
from jax import lax
from jax.experimental import pallas as pl
from jax.experimental.pallas import tpu as pltpu

F32 = jnp.float32
BF16 = jnp.bfloat16

LANES = 128
BF16_ROWS = 16
MXU_COLS = 256
PROJ_ROWS = 1024
PROJ_COLS = 512
TOKEN_ROWS = 512
NORM_ROWS = 256
CHANNEL_COLS = 256
VMEM_SMALL, VMEM_MID, VMEM_BIG = 32, 48, 56
DH_A = 64
N_B = 64
CHUNK = 64
ROPE_THETA = 10000.0
ATTN_SCALE = DH_A ** -0.5
LOG2_E = math.log2(math.e)
ATTN_KEY_CHUNK = 512
PREPASS_HEADS = 4
WKV_GROUP_LANES = 2048
VT_ROWS = LANES + BF16_ROWS
NORM_EPS = 1e-6
SUBLN_EPS = 1e-5
LNX_EPS = 64e-5
MIB = 2 ** 20


def _tile(n, target, align):
    if n <= target:
        return n
    t = (target // align) * align
    while t >= align:
        if n % t == 0:
            return t
        t -= align
    raise ValueError(f"no tile for {n} (target {target}, align {align})")


def _params(semantics, vmem_mib):
    return pltpu.CompilerParams(dimension_semantics=semantics, vmem_limit_bytes=vmem_mib * MIB)


def _sigmoid(x):
    return 1.0 / (1.0 + jnp.exp(-x))


def _dot(a, b):
    return jnp.dot(a, b, preferred_element_type=F32)


def _dot_nt(a, b):
    return lax.dot_general(a, b, (((1,), (1,)), ((), ())), preferred_element_type=F32)


def _dot_tn(a, b):
    return lax.dot_general(a, b, (((0,), (0,)), ((), ())), preferred_element_type=F32)


def _rmsnorm_kernel(x_ref, g_ref, o_ref):
    x = x_ref[...]
    ms = jnp.mean(x * x, axis=-1, keepdims=True)
    o_ref[...] = (x * lax.rsqrt(ms + NORM_EPS) * g_ref[...]).astype(o_ref.dtype)


def _rmsnorm_bf16(x2, g):
    t, d = x2.shape
    tm = _tile(t, NORM_ROWS, BF16_ROWS)
    return pl.pallas_call(
        _rmsnorm_kernel,
        grid=(t // tm,),
        in_specs=[pl.BlockSpec((tm, d), lambda i: (i, 0)), pl.BlockSpec((1, d), lambda i: (0, 0))],
        out_specs=pl.BlockSpec((tm, d), lambda i: (i, 0)),
        out_shape=jax.ShapeDtypeStruct((t, d), BF16),
        compiler_params=_params(("parallel",), VMEM_SMALL),
        name="rmsnorm_cast",
    )(x2, g.reshape(1, d))


def _matmul_kernel(a_ref, w_ref, o_ref):
    o_ref[...] = _dot(a_ref[...], w_ref[...].astype(BF16)).astype(o_ref.dtype)


def _matmul_bf16(a, w):
    m, k = a.shape
    n = w.shape[1]
    tm = _tile(m, PROJ_ROWS, BF16_ROWS)
    tn = _tile(n, PROJ_COLS, MXU_COLS)
    return pl.pallas_call(
        _matmul_kernel,
        grid=(m // tm, n // tn),
        in_specs=[pl.BlockSpec((tm, k), lambda i, j: (i, 0)), pl.BlockSpec((k, tn), lambda i, j: (0, j))],
        out_specs=pl.BlockSpec((tm, tn), lambda i, j: (i, j)),
        out_shape=jax.ShapeDtypeStruct((m, n), BF16),
        compiler_params=_params(("parallel", "arbitrary"), VMEM_BIG),
        name="in_proj",
    )(a, w)


def _rope_kernel(q_ref, k_ref, v_ref, cos_ref, sin_ref, qo_ref, ko_ref, vo_ref):
    cos = cos_ref[...]
    sin = sin_ref[...]
    lane = lax.broadcasted_iota(jnp.int32, cos.shape, 1)
    first_half = (lane % DH_A) < (DH_A // 2)

    def rope(x):
        partner = jnp.where(first_half,
                            pltpu.roll(x, LANES - DH_A // 2, axis=1),
                            pltpu.roll(x, DH_A // 2, axis=1))
        return x * cos + partner * sin

    for hh in range(qo_ref.shape[1]):
        cols = slice(hh * LANES, (hh + 1) * LANES)
        qo_ref[0, hh] = (rope(q_ref[0, :, cols].astype(F32)) * (ATTN_SCALE * LOG2_E)).astype(qo_ref.dtype)
        ko_ref[0, hh] = rope(k_ref[0, :, cols].astype(F32)).astype(ko_ref.dtype)
        vo_ref[0, hh, 0, 0:LANES, :] = v_ref[0, :, cols].astype(F32).T.astype(vo_ref.dtype)
        vo_ref[0, hh, 0, LANES:, :] = jnp.ones((vo_ref.shape[3] - LANES, vo_ref.shape[4]), vo_ref.dtype)


def _rope_tables(s):
    half = DH_A // 2
    inv = 1.0 / (ROPE_THETA ** (jnp.arange(0, DH_A, 2, dtype=F32) / DH_A))
    ang = jnp.arange(s, dtype=F32)[:, None] * inv[None, :]
    cos, sin = jnp.cos(ang), jnp.sin(ang)
    reps = LANES // half
    return jnp.tile(cos, (1, reps)), jnp.tile(jnp.concatenate([-sin, sin], axis=-1), (1, reps // 2))


def _attn_prepass(p3, w_a):
    b, s, _ = p3.shape
    h = w_a // LANES
    ts = _tile(s // 2, ATTN_KEY_CHUNK, LANES)
    cos, sin = _rope_tables(s)
    head_out = jax.ShapeDtypeStruct((b, h, s, LANES), BF16)
    vt_out = jax.ShapeDtypeStruct((b, h, s // ts, VT_ROWS, ts), BF16)
    hp = _tile(h, PREPASS_HEADS, 1)
    hg = h // hp
    col = lambda off: pl.BlockSpec((1, ts, hp * LANES), lambda bi, si, hi: (bi, si, off + hi))
    tab = pl.BlockSpec((ts, LANES), lambda bi, si, hi: (si, 0))
    out = pl.BlockSpec((1, hp, ts, LANES), lambda bi, si, hi: (bi, hi, si, 0))
    out_t = pl.BlockSpec((1, hp, 1, VT_ROWS, ts), lambda bi, si, hi: (bi, hi, si, 0, 0))
    return pl.pallas_call(
        _rope_kernel,
        grid=(b, s // ts, hg),
        in_specs=[col(0), col(hg), col(2 * hg), tab, tab],
        out_specs=[out, out, out_t],
        out_shape=[head_out, head_out, vt_out],
        compiler_params=_params(("parallel", "parallel", "arbitrary"), VMEM_SMALL),
        name="attn_prepass",
    )(p3, p3, p3, cos, sin)


def _attn_kernel(q_ref, k_ref, vt_ref, z_ref, lq1_ref, lk1_ref, lq2_ref, lk2_ref, g_ref, o_ref,
                 qq_scr, acc_scr, sa_scr, sb_scr, *, tq, qt, lam_init):
    nq = q_ref.shape[2] // tq
    nk = vt_ref.shape[2]
    tk = vt_ref.shape[4]
    tiles = [slice(c * qt, (c + 1) * qt) for c in range(2 * tq // qt)]
    lane = lax.broadcasted_iota(jnp.int32, (tq, LANES), 1)
    lam = (jnp.exp(jnp.sum(lq1_ref[...] * lk1_ref[...], keepdims=True))
           - jnp.exp(jnp.sum(lq2_ref[...] * lk2_ref[...], keepdims=True)) + lam_init)

    def load_queries(qi):
        rows = pl.ds(pl.multiple_of(qi * tq, tq), tq)
        q = q_ref[0, 0, rows, :].astype(F32)
        qq_scr[0:tq, :] = jnp.where(lane < DH_A, q, 0.0).astype(BF16)
        qq_scr[tq:2 * tq, :] = jnp.where(lane >= DH_A, q, 0.0).astype(BF16)

    def produce(i, s_scr):
        off = pl.multiple_of(i * tk, tk)
        ks = k_ref[0, 0, pl.ds(off, tk), :]
        s = [_dot_nt(ks, qq_scr[t, :]) for t in tiles]
        for t, x in zip(tiles, s):
            s_scr[:, t] = x
        return jnp.concatenate([jnp.max(x, axis=0, keepdims=True) for x in s], axis=1)

    def consume(i, s_scr, cmax, m_old):
        vt = vt_ref[0, 0, i]
        m_new = jnp.maximum(m_old, cmax)
        alpha = jnp.exp2(m_old - m_new)
        p = [jnp.exp2(s_scr[:, t] - m_new[:, t]).astype(BF16) for t in tiles]
        pv = [_dot(vt, x) for x in p]
        for t, x in zip(tiles, pv):
            acc_scr[:, t] = alpha[:, t] * acc_scr[:, t] + x
        return m_new

    def pair(j, carry):
        m, cmax_a = carry
        cmax_b = produce(2 * j + 1, sb_scr)
        m = consume(2 * j, sa_scr, cmax_a, m)
        cmax_a = produce(2 * j + 2, sa_scr)
        m = consume(2 * j + 1, sb_scr, cmax_b, m)
        return m, cmax_a

    m0 = jnp.full((1, 2 * tq), -jnp.inf, F32)
    trips = nk // 2 - 1
    unroll = next(u for u in (5, 3, 2, 1) if trips % u == 0 and (trips // u >= 2 or u == 1))

    def query_tile(qi, cmax_a):
        m, cmax_a = lax.fori_loop(0, trips, pair, (m0, cmax_a), unroll=unroll)
        cmax_b = produce(nk - 1, sb_scr)
        m = consume(nk - 2, sa_scr, cmax_a, m)
        load_queries(jnp.minimum(qi + 1, nq - 1))
        cmax_next = produce(0, sa_scr)
        consume(nk - 1, sb_scr, cmax_b, m)

        rows = pl.ds(pl.multiple_of(qi * tq, tq), tq)
        ot = acc_scr[0:LANES, :] * (1.0 / acc_scr[LANES:LANES + 1, :])
        acc_scr[...] = jnp.zeros(acc_scr.shape, F32)
        o = (ot[:, 0:tq] - lam * ot[:, tq:2 * tq]).T
        y = o * lax.rsqrt(jnp.mean(o * o, axis=-1, keepdims=True) + SUBLN_EPS) * g_ref[...] * (1.0 - lam_init)
        z = z_ref[0, rows, :].astype(F32)
        o_ref[0, rows, :] = (y * (z * _sigmoid(z))).astype(o_ref.dtype)
        return cmax_next

    load_queries(0)
    acc_scr[...] = jnp.zeros(acc_scr.shape, F32)
    lax.fori_loop(0, nq, query_tile, produce(0, sa_scr))


def _diff_attention(qr, kr, vt, p3, z_off, lam_q1, lam_k1, lam_q2, lam_k2, subln_g, lam_init):
    b, h, s, _ = qr.shape
    nk, tk = vt.shape[2], vt.shape[4]
    tq = _tile(s, TOKEN_ROWS, LANES)
    qt = _tile(2 * tq, CHANNEL_COLS, LANES)
    assert z_off % LANES == 0 and nk % 2 == 0
    zc = z_off // LANES
    vec = lambda n: pl.BlockSpec((1, n), lambda bi, hi: (0, 0))
    head = pl.BlockSpec((1, 1, s, LANES), lambda bi, hi: (bi, hi, 0, 0))
    kern = functools.partial(_attn_kernel, tq=tq, qt=qt, lam_init=lam_init)
    return pl.pallas_call(
        kern,
        grid=(b, h),
        in_specs=[
            head, head,
            pl.BlockSpec((1, 1, nk, VT_ROWS, tk), lambda bi, hi: (bi, hi, 0, 0, 0)),
            pl.BlockSpec((1, s, LANES), lambda bi, hi: (bi, 0, zc + hi)),
            vec(DH_A), vec(DH_A), vec(DH_A), vec(DH_A), vec(LANES),
        ],
        out_specs=pl.BlockSpec((1, s, LANES), lambda bi, hi: (bi, 0, hi)),
        out_shape=jax.ShapeDtypeStruct((b, s, h * LANES), BF16),
        scratch_shapes=[
            pltpu.VMEM((2 * tq, LANES), BF16),
            pltpu.VMEM((VT_ROWS, 2 * tq), F32),
            pltpu.VMEM((tk, 2 * tq), F32),
            pltpu.VMEM((tk, 2 * tq), F32),
        ],
        compiler_params=_params(("parallel", "arbitrary"), VMEM_BIG),
        name="diff_attn",
    )(qr, kr, vt, p3, lam_q1.reshape(1, DH_A), lam_k1.reshape(1, DH_A), lam_q2.reshape(1, DH_A),
      lam_k2.reshape(1, DH_A), subln_g.reshape(1, LANES))


def _group_sum(x, gmat):
    hi = x.astype(BF16)
    lo = (x - hi.astype(F32)).astype(BF16)
    return _dot(hi, gmat) + _dot(lo, gmat)


def _group_matrix(cw):
    r = lax.broadcasted_iota(jnp.int32, (cw, cw), 0) // N_B
    c = lax.broadcasted_iota(jnp.int32, (cw, cw), 1) // N_B
    return jnp.where(r == c, 1.0, 0.0).astype(BF16)


def _rwkv_prep_kernel(r_ref, k_ref, v_ref, lo_ref, rp_ref, kp_ref, vp_ref, lop_ref, rn_ref, kn_ref, vn_ref,
                      lon_ref, mu_ref, mulo_ref, w0_ref, a0_ref, wl_ref, al_ref, kk_ref, ka_ref, rk_ref,
                      ro_ref, vo_ref, ao_ref, lw_ref, kd_ref, bd_ref, bonus_ref, pw_scr, pa_scr):
    ts = r_ref.shape[1]
    si = pl.program_id(1)
    first = si == 0
    last = si == pl.num_programs(1) - 1

    def shift(cur_ref, prev_ref, next_ref, mu_p, mu_n):
        x = cur_ref[0].astype(F32)
        hp = prev_ref[0].astype(F32)
        hn = next_ref[0].astype(F32)
        pr = jnp.where(first, 0.0, hp[hp.shape[0] - 1:hp.shape[0], :])
        nx = jnp.where(last, 0.0, hn[0:1, :])
        row = lax.broadcasted_iota(jnp.int32, x.shape, 0)
        prev = jnp.where(row == 0, pr, pltpu.roll(x, 1, axis=0))
        nxt = jnp.where(row == ts - 1, nx, pltpu.roll(x, ts - 1, axis=0))
        return x + mu_p * (prev - x) + mu_n * (nxt - x)

    r = shift(r_ref, rp_ref, rn_ref, mu_ref[0, 0:1, :], mu_ref[1, 0:1, :])
    k = shift(k_ref, kp_ref, kn_ref, mu_ref[0, 1:2, :], mu_ref[1, 1:2, :])
    v = shift(v_ref, vp_ref, vn_ref, mu_ref[0, 2:3, :], mu_ref[1, 2:3, :])

    @pl.when(pl.program_id(2) == 0)
    def _():
        lo = shift(lo_ref, lop_ref, lon_ref, mulo_ref[0:1, :], mulo_ref[1:2, :])
        rl = lo.shape[1] // 4
        for d in range(2):
            pw_scr[d] = jnp.tanh(lo[:, d * rl:(d + 1) * rl]).astype(BF16)
            pa_scr[d] = lo[:, (2 + d) * rl:(3 + d) * rl].astype(BF16)

    cw = r.shape[1]
    gmat = _group_matrix(cw)
    kkh = k * kk_ref[...]
    nrm = jnp.sqrt(_group_sum(kkh * kkh, gmat))
    kk = kkh / jnp.maximum(nrm, 1e-12)
    ka = ka_ref[...]
    ro_ref[0] = r.astype(ro_ref.dtype)
    vo_ref[0] = v.astype(vo_ref.dtype)
    ao_ref[0] = (-kk).astype(ao_ref.dtype)
    ksum = jnp.zeros_like(k)
    for d in range(2):
        pw = pw_scr[d]
        pa = pa_scr[d]
        wl = w0_ref[d:d + 1, :] + _dot(pw, wl_ref[d])
        lw_ref[d, 0] = -math.exp(-0.5) * _sigmoid(wl)
        a = _sigmoid(a0_ref[d:d + 1, :] + _dot(pa, al_ref[d]))
        kdir = k * (1.0 + (a - 1.0) * ka)
        kd_ref[d, 0] = kdir.astype(kd_ref.dtype)
        bd_ref[d, 0] = (kk * a).astype(bd_ref.dtype)
        ksum = ksum + kdir
    bonus_ref[0] = _group_sum(r * ksum * rk_ref[...], gmat) * v


def _rwkv_prep(p3, s_off, w_b, r_lora, mu_prev, mu_next, w0, w_lora, a0, a_lora, k_k, k_a, r_k):
    b, s, _ = p3.shape
    ts = _tile(s, TOKEN_ROWS, BF16_ROWS)
    cw = _tile(w_b, CHANNEL_COLS, LANES)
    lw = 4 * r_lora
    halo = BF16_ROWS
    nblk = s // halo
    assert s_off % cw == 0 and w_b % cw == 0 and (s_off + 3 * w_b) % lw == 0 and ts % halo == 0
    rc, kc, vc, lc = s_off // cw, (s_off + w_b) // cw, (s_off + 2 * w_b) // cw, (s_off + 3 * w_b) // lw
    per = ts // halo

    cur = lambda off: pl.BlockSpec((1, ts, cw), lambda bi, si, ji: (bi, si, off + ji))
    prv = lambda off: pl.BlockSpec((1, halo, cw), lambda bi, si, ji: (bi, jnp.maximum(si * per - 1, 0), off + ji))
    nxt = lambda off: pl.BlockSpec((1, halo, cw),
                                   lambda bi, si, ji: (bi, jnp.minimum((si + 1) * per, nblk - 1), off + ji))
    lo_cur = pl.BlockSpec((1, ts, lw), lambda bi, si, ji: (bi, si, lc))
    lo_prv = pl.BlockSpec((1, halo, lw), lambda bi, si, ji: (bi, jnp.maximum(si * per - 1, 0), lc))
    lo_nxt = pl.BlockSpec((1, halo, lw), lambda bi, si, ji: (bi, jnp.minimum((si + 1) * per, nblk - 1), lc))
    colvec = lambda rows: pl.BlockSpec((rows, cw), lambda bi, si, ji: (0, ji))

    mu_rkv = jnp.stack([mu_prev[:3 * w_b].reshape(3, w_b), mu_next[:3 * w_b].reshape(3, w_b)])
    mu_lo = jnp.stack([mu_prev[3 * w_b:], mu_next[3 * w_b:]])
    tok = pl.BlockSpec((1, ts, cw), lambda bi, si, ji: (bi, si, ji))
    tok2 = pl.BlockSpec((2, 1, ts, cw), lambda bi, si, ji: (0, bi, si, ji))
    shp = lambda dt: jax.ShapeDtypeStruct((b, s, w_b), dt)
    shp2 = lambda dt: jax.ShapeDtypeStruct((2, b, s, w_b), dt)
    return pl.pallas_call(
        _rwkv_prep_kernel,
        grid=(b, s // ts, w_b // cw),
        in_specs=[cur(rc), cur(kc), cur(vc), lo_cur, prv(rc), prv(kc), prv(vc), lo_prv,
                  nxt(rc), nxt(kc), nxt(vc), lo_nxt,
                  pl.BlockSpec((2, 3, cw), lambda bi, si, ji: (0, 0, ji)),
                  pl.BlockSpec((2, lw), lambda bi, si, ji: (0, 0)),
                  colvec(2), colvec(2),
                  pl.BlockSpec((2, r_lora, cw), lambda bi, si, ji: (0, 0, ji)),
                  pl.BlockSpec((2, r_lora, cw), lambda bi, si, ji: (0, 0, ji)),
                  colvec(1), colvec(1), colvec(1)],
        out_specs=[tok, tok, tok, tok2, tok2, tok2, tok],
        out_shape=[shp(BF16), shp(BF16), shp(BF16), shp2(F32), shp2(BF16), shp2(BF16), shp(F32)],
        scratch_shapes=[pltpu.VMEM((2, ts, r_lora), BF16), pltpu.VMEM((2, ts, r_lora), BF16)],
        compiler_params=_params(("parallel", "parallel", "arbitrary"), VMEM_MID),
        name="rwkv_prep",
    )(p3, p3, p3, p3, p3, p3, p3, p3, p3, p3, p3, p3, mu_rkv, mu_lo, w0, a0,
      w_lora.astype(BF16), a_lora.astype(BF16), k_k.reshape(1, w_b), k_a.reshape(1, w_b), r_k.reshape(1, w_b))


def _wkv_kernel(mask_ref, r_ref, a_ref, v_ref, lw_ref, k_ref, b_ref, y_ref, z_scr, *, npairs, reverse):
    assert CHUNK == N_B and 2 * N_B == LANES
    c = CHUNK
    ci = pl.program_id(2)

    @pl.when(ci == 0)
    def _():
        z_scr[...] = jnp.zeros(z_scr.shape, F32)

    strict = mask_ref[0]
    incl = mask_ref[1]
    cum = incl[0:c, 0:c].astype(BF16)
    lane = lax.broadcasted_iota(jnp.int32, (c, LANES), 1)
    low = lane < N_B
    row2 = lax.broadcasted_iota(jnp.int32, (LANES, LANES), 0)
    col2 = lax.broadcasted_iota(jnp.int32, (LANES, LANES), 1)
    own = (row2 < c) == (col2 < N_B)

    def stack(x):
        return jnp.concatenate([jnp.where(low, x, 0.0), jnp.where(low, 0.0, x)], axis=0)

    def each(f, *cols):
        return [f(*xs) for xs in zip(*cols)]

    sls = [slice(j * LANES, (j + 1) * LANES) for j in range(npairs)]
    lw = [lw_ref[0, 0, :, sl] for sl in sls]
    r = [r_ref[0, :, sl].astype(F32) for sl in sls]
    a = [a_ref[0, :, sl].astype(F32) for sl in sls]
    v = [v_ref[0, :, sl].astype(F32) for sl in sls]
    k = [k_ref[0, 0, :, sl].astype(F32) for sl in sls]
    b = [b_ref[0, 0, :, sl].astype(F32) for sl in sls]
    zb = [stack(z_scr[j]).astype(BF16) for j in range(npairs)]
    strict_p = strict[0:c, :] + strict[c:, :]
    incl_p = incl[0:c, :] + incl[c:, :]
    eye_p = lax.broadcasted_iota(jnp.int32, (c, LANES), 0) == (lane % N_B)

    def fold(x):
        return x[0:c, :] + x[c:, :]

    hi = each(lambda x: x.astype(BF16), lw)
    lo = each(lambda x, h: (x - h.astype(F32)).astype(BF16), lw, hi)
    cum2 = jnp.concatenate([cum, cum], axis=1)
    lam = each(lambda h, l: _dot(cum2, jnp.concatenate([h, l], axis=0)), hi, lo)
    tot = each(lambda x: jnp.sum(x, axis=0, keepdims=True), lw)
    gdec = each(jnp.exp, tot)
    a_p = each(lambda x, lm, w: x * jnp.exp(lm - w), a, lam, lw)
    r_p = each(lambda x, lm: x * jnp.exp(lm), r, lam)
    a_s = each(stack, a_p)
    e_neg = each(lambda lm: jnp.exp(-lm), lam)
    e_rem = each(lambda t, lm: jnp.exp(t - lm), tot, lam)
    b_s = each(lambda x, e: stack(x * e).astype(BF16), b, e_neg)
    k_s = each(lambda x, e: stack(x * e).astype(BF16), k, e_neg)
    bh_s = each(lambda x, e: stack(x * e).astype(BF16), b, e_rem)
    kh_s = each(lambda x, e: stack(x * e).astype(BF16), k, e_rem)
    v_s = each(lambda x: stack(x).astype(BF16), v)

    m4 = each(lambda x, y, p, q: _dot_nt(jnp.concatenate([x, y], axis=0).astype(BF16),
                                         jnp.concatenate([p, q], axis=0)), a_p, r_p, b_s, k_s)
    lp = each(lambda m: stack(m[0:c, 0:LANES] * strict_p).astype(BF16), m4)
    lak = each(lambda m: stack(m[0:c, LANES:] * strict_p).astype(BF16), m4)
    mrbk = each(lambda m: jnp.concatenate([stack(m[c:, 0:LANES] * incl_p), stack(m[c:, LANES:] * incl_p)],
                                          axis=1).astype(BF16), m4)

    half = LANES // 2
    x = each(lambda p, q, w: p + pltpu.roll(_dot(q, w), half, axis=1), a_s, lak, v_s)

    def live(arr, p):
        lo0 = 0 if reverse else p
        return jnp.concatenate([arr[lo0:lo0 + c - p], arr[c + lo0:c + lo0 + c - p]], axis=0)

    def add_live(full, part, p):
        n, lo0 = c - p, (0 if reverse else p)
        pieces = []
        for blk in range(2):
            base = blk * c
            pieces += [full[base:base + lo0], full[base + lo0:base + lo0 + n] + part[blk * n:(blk + 1) * n],
                       full[base + lo0 + n:base + c]]
        return jnp.concatenate([q for q in pieces if q.shape[0]], axis=0)

    steps = int(math.log2(c))
    for it in range(steps - 1):
        p = 2 ** it
        if p < BF16_ROWS:
            res = each(lambda l, q: _dot(l, jnp.concatenate([q.astype(BF16), l], axis=1)), lp, x)
            x = each(lambda q, s: q + s[:, 0:LANES], x, res)
            lp = each(lambda s: s[:, LANES:].astype(BF16), res)
        else:
            res = each(lambda l, q: _dot(live(l, p), jnp.concatenate([q.astype(BF16), l], axis=1)), lp, x)
            x = each(lambda q, s: add_live(q, s[:, 0:LANES], p), x, res)
            lp = each(lambda s: add_live(jnp.zeros((2 * c, LANES), F32), s[:, LANES:], p).astype(BF16), res)
    p = 2 ** (steps - 1)
    x = each(lambda l, q: add_live(q, _dot(live(l, p), q.astype(BF16)), p), lp, x)
    w_s = each(lambda q: jnp.where(own, q, 0.0).astype(BF16), x)
    u_s = each(lambda q: jnp.where(own, pltpu.roll(q, half, axis=1), 0.0).astype(BF16), x)

    zero = jnp.zeros((LANES, LANES), BF16)
    wuv = each(lambda w, u, vv: jnp.concatenate([jnp.concatenate([w, u], axis=1),
                                                 jnp.concatenate([zero, vv], axis=1)], axis=0), w_s, u_s, v_s)
    ry = each(_dot, mrbk, wuv)
    pq = each(lambda p, q, m: _dot_tn(jnp.concatenate([p, q], axis=0), m), bh_s, kh_s, wuv)
    rw = each(lambda p, s: (p + fold(s[:, 0:LANES])).astype(BF16), r_p, ry)
    pt = each(lambda g, s: (jnp.where(eye_p, g, 0.0) + fold(s[:, 0:LANES])).astype(BF16), gdec, pq)

    ys = each(lambda p, z, s: _dot(p, z) + fold(s[:, LANES:]), rw, zb, ry)
    z_new = each(lambda p, z, s: _dot(p, z) + fold(s[:, LANES:]), pt, zb, pq)
    for j, sl in enumerate(sls):
        y_ref[0, :, sl] = ys[j].astype(y_ref.dtype)
        z_scr[j] = z_new[j]


def _wkv_masks(reverse):
    i = np.arange(LANES)
    same = (i[:, None] // CHUNK) == (i[None, :] // CHUNK)
    t, s = i[:, None] % CHUNK, i[None, :] % CHUNK
    before = (s > t) if reverse else (s < t)
    return jnp.asarray(np.stack([same & before, same & (before | (s == t))]).astype(np.float32))


def _wkv_scan(r, a, v, logw, kdir, bdir, reverse):
    b, s, w_b = r.shape
    c = CHUNK
    nc = s // c
    d = int(reverse)
    gw = _tile(w_b, WKV_GROUP_LANES, LANES)
    npairs = gw // LANES
    chunk_of = (lambda ci: nc - 1 - ci) if reverse else (lambda ci: ci)
    shared = pl.BlockSpec((1, c, gw), lambda bi, gi, ci: (bi, chunk_of(ci), gi))
    perdir = pl.BlockSpec((1, 1, c, gw), lambda bi, gi, ci: (d, bi, chunk_of(ci), gi))
    kern = functools.partial(_wkv_kernel, npairs=npairs, reverse=reverse)
    return pl.pallas_call(
        kern,
        grid=(b, w_b // gw, nc),
        in_specs=[pl.BlockSpec((2, LANES, LANES), lambda bi, gi, ci: (0, 0, 0)),
                  shared, shared, shared, perdir, perdir, perdir],
        out_specs=shared,
        out_shape=jax.ShapeDtypeStruct((b, s, w_b), BF16),
        scratch_shapes=[pltpu.VMEM((npairs, N_B, LANES), F32)],
        compiler_params=_params(("parallel", "parallel", "arbitrary"), VMEM_SMALL),
        name="wkv_scan_rev" if reverse else "wkv_scan_fwd",
    )(_wkv_masks(reverse), r, a, v, logw, kdir, bdir)


def _rwkv_out_kernel(yf_ref, yb_ref, bonus_ref, z_ref, g_ref, b_ref, o_ref):
    y = yf_ref[0].astype(F32) + yb_ref[0].astype(F32)
    gmat = _group_matrix(y.shape[1])
    mu = _group_sum(y, gmat) * (1.0 / N_B)
    yc = y - mu
    var = _group_sum(yc * yc, gmat) * (1.0 / N_B)
    yn = yc * lax.rsqrt(var + LNX_EPS) * g_ref[...] + b_ref[...]
    z = z_ref[0].astype(F32)
    o_ref[0] = ((yn + bonus_ref[0]) * (z * _sigmoid(z))).astype(o_ref.dtype)


def _rwkv_out(y_fwd, y_rev, bonus, p3, z_off, lnx_g, lnx_b):
    b, s, w_b = y_fwd.shape
    ts = _tile(s, TOKEN_ROWS, BF16_ROWS)
    cw = _tile(w_b, CHANNEL_COLS, LANES)
    assert z_off % cw == 0
    zc = z_off // cw
    vec = pl.BlockSpec((1, cw), lambda bi, si, ji: (0, ji))
    tok = pl.BlockSpec((1, ts, cw), lambda bi, si, ji: (bi, si, ji))
    return pl.pallas_call(
        _rwkv_out_kernel,
        grid=(b, s // ts, w_b // cw),
        in_specs=[tok, tok, tok,
                  pl.BlockSpec((1, ts, cw), lambda bi, si, ji: (bi, si, zc + ji)),
                  vec, vec],
        out_specs=pl.BlockSpec((1, ts, cw), lambda bi, si, ji: (bi, si, ji)),
        out_shape=jax.ShapeDtypeStruct((b, s, w_b), BF16),
        compiler_params=_params(("parallel", "parallel", "parallel"), VMEM_SMALL),
        name="rwkv_out",
    )(y_fwd, y_rev, bonus, p3, lnx_g.reshape(1, w_b), lnx_b.reshape(1, w_b))


def _merge_kernel(ua_ref, ub_ref, wa_ref, wb_ref, ga_ref, gb_ref, o_ref):
    oa = _dot(ua_ref[...], wa_ref[...])
    ob = _dot(ub_ref[...], wb_ref[...])
    m = _sigmoid(ga_ref[...].astype(F32)) * oa + _sigmoid(gb_ref[...].astype(F32)) * ob
    o_ref[...] = m.astype(o_ref.dtype)


def _merge(ua, ub, w_oa, w_ob, p2, g_off):
    t, w_a = ua.shape
    w_b = ub.shape[1]
    d = w_oa.shape[1]
    tm = _tile(t, PROJ_ROWS, BF16_ROWS)
    tn = _tile(d, PROJ_COLS, MXU_COLS)
    assert g_off % tn == 0 and d % tn == 0
    gc = g_off // tn
    nd = d // tn
    return pl.pallas_call(
        _merge_kernel,
        grid=(t // tm, nd),
        in_specs=[pl.BlockSpec((tm, w_a), lambda i, j: (i, 0)),
                  pl.BlockSpec((tm, w_b), lambda i, j: (i, 0)),
                  pl.BlockSpec((w_a, tn), lambda i, j: (0, j)),
                  pl.BlockSpec((w_b, tn), lambda i, j: (0, j)),
                  pl.BlockSpec((tm, tn), lambda i, j: (i, gc + j)),
                  pl.BlockSpec((tm, tn), lambda i, j: (i, gc + nd + j))],
        out_specs=pl.BlockSpec((tm, tn), lambda i, j: (i, j)),
        out_shape=jax.ShapeDtypeStruct((t, d), BF16),
        compiler_params=_params(("parallel", "arbitrary"), VMEM_MID),
        name="gate_merge",
    )(ua, ub, w_oa, w_ob, p2, p2)


def _out_kernel(m_ref, w_ref, x_ref, o_ref):
    o_ref[...] = x_ref[...] + _dot(m_ref[...], w_ref[...])


def _out_norm_kernel(m_ref, w_ref, x_ref, g_ref, o_ref, res_scr, ss_scr):
    j = pl.program_id(1)
    nj, _, tn = res_scr.shape
    res = x_ref[...] + _dot(m_ref[...], w_ref[...])
    res_scr[j] = res

    @pl.when(j == 0)
    def _():
        ss_scr[...] = jnp.zeros(ss_scr.shape, F32)

    ss_scr[...] += jnp.sum(res * res, axis=-1, keepdims=True)

    @pl.when(j == nj - 1)
    def _():
        scale = lax.rsqrt(ss_scr[...] * (1.0 / (nj * tn)) + NORM_EPS)
        for jj in range(nj):
            cols = slice(jj * tn, (jj + 1) * tn)
            o_ref[:, cols] = res_scr[jj] * scale * g_ref[:, cols]


def _out_proj(m, w_out, x2, final_g, final_norm):
    t, d = x2.shape
    tm = _tile(t, TOKEN_ROWS, BF16_ROWS)
    tn = _tile(d, PROJ_COLS, MXU_COLS)
    in_specs = [pl.BlockSpec((tm, d), lambda i, j: (i, 0)),
                pl.BlockSpec((d, tn), lambda i, j: (0, j)),
                pl.BlockSpec((tm, tn), lambda i, j: (i, j))]
    if not final_norm:
        return pl.pallas_call(
            _out_kernel,
            grid=(t // tm, d // tn),
            in_specs=in_specs,
            out_specs=pl.BlockSpec((tm, tn), lambda i, j: (i, j)),
            out_shape=jax.ShapeDtypeStruct((t, d), F32),
            compiler_params=_params(("parallel", "arbitrary"), VMEM_BIG),
            name="out_proj",
        )(m, w_out, x2)
    return pl.pallas_call(
        _out_norm_kernel,
        grid=(t // tm, d // tn),
        in_specs=in_specs + [pl.BlockSpec((1, d), lambda i, j: (0, 0))],
        out_specs=pl.BlockSpec((tm, d), lambda i, j: (i, 0)),
        out_shape=jax.ShapeDtypeStruct((t, d), F32),
        scratch_shapes=[pltpu.VMEM((d // tn, tm, tn), F32), pltpu.VMEM((tm, 1), F32)],
        compiler_params=_params(("parallel", "arbitrary"), VMEM_BIG),
        name="out_proj_norm",
    )(m, w_out, x2, final_g.reshape(1, d))


def _lambda_init(layer_idx):
    return 0.8 - 0.6 * math.exp(-0.3 * layer_idx)


def _mixer_layer(x, l, prm, final_g, final_norm):
    b, s, d = x.shape
    w_a = prm["w_oA"][l].shape[0]
    w_b = prm["w_oB"][l].shape[0]
    r_lora = prm["w_lora"].shape[2]
    c_shift = 3 * w_b + 4 * r_lora
    n_in = prm["w_in"].shape[2]
    assert n_in == 3 * w_a + c_shift + w_a + w_b + 2 * d
    assert w_a % LANES == 0 and w_b % LANES == 0 and s % CHUNK == 0
    s_off = 3 * w_a
    za_off = s_off + c_shift
    zb_off = za_off + w_a
    g_off = zb_off + w_b
    lam_init = _lambda_init(l)

    x2 = x.reshape(b * s, d)
    h = _rmsnorm_bf16(x2, prm["norm_g"][l])
    p2 = _matmul_bf16(h, prm["w_in"][l])
    p3 = p2.reshape(b, s, n_in)

    qr, kr, vt = _attn_prepass(p3, w_a)
    ua = _diff_attention(qr, kr, vt, p3, za_off, prm["lam_q1"][l], prm["lam_k1"][l], prm["lam_q2"][l],
                         prm["lam_k2"][l], prm["subln_g"][l], lam_init)

    r, v, a, logw, kdir, bdir, bonus = _rwkv_prep(
        p3, s_off, w_b, r_lora, prm["mu_prev"][l], prm["mu_next"][l], prm["w0"][l], prm["w_lora"][l],
        prm["a0"][l], prm["a_lora"][l], prm["k_k"][l], prm["k_a"][l], prm["r_k"][l])
    y_fwd = _wkv_scan(r, a, v, logw, kdir, bdir, reverse=False)
    y_rev = _wkv_scan(r, a, v, logw, kdir, bdir, reverse=True)
    ub = _rwkv_out(y_fwd, y_rev, bonus, p3, zb_off, prm["lnx_g"][l], prm["lnx_b"][l])

    m = _merge(ua.reshape(b * s, w_a), ub.reshape(b * s, w_b), prm["w_oA"][l].astype(BF16),
               prm["w_oB"][l].astype(BF16), p2, g_off)
    out = _out_proj(m, prm["w_out"][l].astype(BF16), x2, final_g, final_norm)
    return out.reshape(b, s, d)


def kernel(x_prompt, x_sample, norm_g, w_in, mu_prev, mu_next, lam_q1, lam_k1, lam_q2, lam_k2, subln_g, w0,
           w_lora, a0, a_lora, k_k, k_a, r_k, lnx_g, lnx_b, w_oA, w_oB, w_out, final_g):
    prm = dict(norm_g=norm_g, w_in=w_in, mu_prev=mu_prev, mu_next=mu_next, lam_q1=lam_q1, lam_k1=lam_k1,
               lam_q2=lam_q2, lam_k2=lam_k2, subln_g=subln_g, w0=w0, w_lora=w_lora, a0=a0, a_lora=a_lora,
               k_k=k_k, k_a=k_a, r_k=r_k, lnx_g=lnx_g, lnx_b=lnx_b, w_oA=w_oA, w_oB=w_oB, w_out=w_out)
    depth = norm_g.shape[0]

    def trunk(x):
        for l in range(depth):
            x = _mixer_layer(x, l, prm, final_g, final_norm=(l == depth - 1))
        return x

    return (trunk(x_prompt), trunk(x_sample))
```

```python
import functools
import math

import jax
import jax.numpy as jnp
import numpy as np
from jax import lax
from jax.experimental import pallas as pl
from jax.experimental.pallas import tpu as pltpu

F32 = jnp.float32
BF16 = jnp.bfloat16

LANES = 128
BF16_ROWS = 16
MXU_COLS = 256
PROJ_ROWS = 1024
PROJ_COLS = 512
TOKEN_ROWS = 512
RWKV_ROWS = 1024
NORM_ROWS = 256
CHANNEL_COLS = 256
VMEM_SMALL, VMEM_MID, VMEM_BIG = 32, 48, 56
DH_A = 64
N_B = 64
CHUNK = 64
ROPE_THETA = 10000.0
ATTN_SCALE = DH_A ** -0.5
LOG2_E = math.log2(math.e)
ATTN_KEY_CHUNK = 512
PREPASS_HEADS = 4
WKV_GROUP_LANES = 2048
VT_ROWS = LANES + BF16_ROWS
NORM_EPS = 1e-6
SUBLN_EPS = 1e-5
LNX_EPS = 64e-5
MIB = 2 ** 20


def _tile(n, target, align):
    if n <= target:
        return n
    t = (target // align) * align
    while t >= align:
        if n % t == 0:
            return t
        t -= align
    raise ValueError(f"no tile for {n} (target {target}, align {align})")


def _params(semantics, vmem_mib):
    return pltpu.CompilerParams(dimension_semantics=semantics, vmem_limit_bytes=vmem_mib * MIB)


def _sigmoid(x):
    return 1.0 / (1.0 + jnp.exp(-x))


def _dot(a, b):
    return jnp.dot(a, b, preferred_element_type=F32)


def _dot_nt(a, b):
    return lax.dot_general(a, b, (((1,), (1,)), ((), ())), preferred_element_type=F32)


def _dot_tn(a, b):
    return lax.dot_general(a, b, (((0,), (0,)), ((), ())), preferred_element_type=F32)


def _rmsnorm_kernel(x_ref, g_ref, o_ref):
    x = x_ref[...]
    ms = jnp.mean(x * x, axis=-1, keepdims=True)
    o_ref[...] = (x * lax.rsqrt(ms + NORM_EPS) * g_ref[...]).astype(o_ref.dtype)


def _rmsnorm_bf16(x2, g):
    t, d = x2.shape
    tm = _tile(t, NORM_ROWS, BF16_ROWS)
    return pl.pallas_call(
        _rmsnorm_kernel,
        grid=(t // tm,),
        in_specs=[pl.BlockSpec((tm, d), lambda i: (i, 0)), pl.BlockSpec((1, d), lambda i: (0, 0))],
        out_specs=pl.BlockSpec((tm, d), lambda i: (i, 0)),
        out_shape=jax.ShapeDtypeStruct((t, d), BF16),
        compiler_params=_params(("parallel",), VMEM_SMALL),
        name="rmsnorm_cast",
    )(x2, g.reshape(1, d))


def _matmul_kernel(a_ref, w_ref, o_ref):
    o_ref[...] = _dot(a_ref[...], w_ref[...].astype(BF16)).astype(o_ref.dtype)


def _matmul_bf16(a, w):
    m, k = a.shape
    n = w.shape[1]
    tm = _tile(m, PROJ_ROWS, BF16_ROWS)
    tn = _tile(n, PROJ_COLS, MXU_COLS)
    return pl.pallas_call(
        _matmul_kernel,
        grid=(m // tm, n // tn),
        in_specs=[pl.BlockSpec((tm, k), lambda i, j: (i, 0)), pl.BlockSpec((k, tn), lambda i, j: (0, j))],
        out_specs=pl.BlockSpec((tm, tn), lambda i, j: (i, j)),
        out_shape=jax.ShapeDtypeStruct((m, n), BF16),
        compiler_params=_params(("parallel", "arbitrary"), VMEM_BIG),
        name="in_proj",
    )(a, w)


def _rope_kernel(q_ref, k_ref, v_ref, cos_ref, sin_ref, qo_ref, ko_ref, vo_ref):
    cos = cos_ref[...]
    sin = sin_ref[...]
    lane = lax.broadcasted_iota(jnp.int32, cos.shape, 1)
    first_half = (lane % DH_A) < (DH_A // 2)

    def rope(x):
        partner = jnp.where(first_half,
                            pltpu.roll(x, LANES - DH_A // 2, axis=1),
                            pltpu.roll(x, DH_A // 2, axis=1))
        return x * cos + partner * sin

    for hh in range(qo_ref.shape[1]):
        cols = slice(hh * LANES, (hh + 1) * LANES)
        qo_ref[0, hh] = (rope(q_ref[0, :, cols].astype(F32)) * (ATTN_SCALE * LOG2_E)).astype(qo_ref.dtype)
        ko_ref[0, hh] = rope(k_ref[0, :, cols].astype(F32)).astype(ko_ref.dtype)
        vo_ref[0, hh, 0, 0:LANES, :] = v_ref[0, :, cols].astype(F32).T.astype(vo_ref.dtype)
        vo_ref[0, hh, 0, LANES:, :] = jnp.ones((vo_ref.shape[3] - LANES, vo_ref.shape[4]), vo_ref.dtype)


def _rope_tables(s):
    half = DH_A // 2
    inv = 1.0 / (ROPE_THETA ** (jnp.arange(0, DH_A, 2, dtype=F32) / DH_A))
    ang = jnp.arange(s, dtype=F32)[:, None] * inv[None, :]
    cos, sin = jnp.cos(ang), jnp.sin(ang)
    reps = LANES // half
    return jnp.tile(cos, (1, reps)), jnp.tile(jnp.concatenate([-sin, sin], axis=-1), (1, reps // 2))


def _attn_prepass(p3, w_a):
    b, s, _ = p3.shape
    h = w_a // LANES
    ts = _tile(s // 2, ATTN_KEY_CHUNK, LANES)
    cos, sin = _rope_tables(s)
    head_out = jax.ShapeDtypeStruct((b, h, s, LANES), BF16)
    vt_out = jax.ShapeDtypeStruct((b, h, s // ts, VT_ROWS, ts), BF16)
    hp = _tile(h, PREPASS_HEADS, 1)
    hg = h // hp
    col = lambda off: pl.BlockSpec((1, ts, hp * LANES), lambda bi, si, hi: (bi, si, off + hi))
    tab = pl.BlockSpec((ts, LANES), lambda bi, si, hi: (si, 0))
    out = pl.BlockSpec((1, hp, ts, LANES), lambda bi, si, hi: (bi, hi, si, 0))
    out_t = pl.BlockSpec((1, hp, 1, VT_ROWS, ts), lambda bi, si, hi: (bi, hi, si, 0, 0))
    return pl.pallas_call(
        _rope_kernel,
        grid=(b, s // ts, hg),
        in_specs=[col(0), col(hg), col(2 * hg), tab, tab],
        out_specs=[out, out, out_t],
        out_shape=[head_out, head_out, vt_out],
        compiler_params=_params(("parallel", "parallel", "arbitrary"), VMEM_SMALL),
        name="attn_prepass",
    )(p3, p3, p3, cos, sin)


def _attn_kernel(q_ref, k_ref, vt_ref, z_ref, lq1_ref, lk1_ref, lq2_ref, lk2_ref, g_ref, o_ref,
                 qq_scr, acc_scr, sa_scr, sb_scr, *, tq, qt, lam_init):
    nq = q_ref.shape[2] // tq
    nk = vt_ref.shape[2]
    tk = vt_ref.shape[4]
    tiles = [slice(c * qt, (c + 1) * qt) for c in range(2 * tq // qt)]
    lane = lax.broadcasted_iota(jnp.int32, (tq, LANES), 1)
    lam = (jnp.exp(jnp.sum(lq1_ref[...] * lk1_ref[...], keepdims=True))
           - jnp.exp(jnp.sum(lq2_ref[...] * lk2_ref[...], keepdims=True)) + lam_init)

    def load_queries(qi):
        rows = pl.ds(pl.multiple_of(qi * tq, tq), tq)
        q = q_ref[0, 0, rows, :].astype(F32)
        qq_scr[0:tq, :] = jnp.where(lane < DH_A, q, 0.0).astype(BF16)
        qq_scr[tq:2 * tq, :] = jnp.where(lane >= DH_A, q, 0.0).astype(BF16)

    def produce(i, s_scr):
        off = pl.multiple_of(i * tk, tk)
        ks = k_ref[0, 0, pl.ds(off, tk), :]
        s = [_dot_nt(ks, qq_scr[t, :]) for t in tiles]
        for t, x in zip(tiles, s):
            s_scr[:, t] = x
        return jnp.concatenate([jnp.max(x, axis=0, keepdims=True) for x in s], axis=1)

    def consume(i, s_scr, cmax, m_old):
        vt = vt_ref[0, 0, i]
        m_new = jnp.maximum(m_old, cmax)
        alpha = jnp.exp2(m_old - m_new)
        p = [jnp.exp2(s_scr[:, t] - m_new[:, t]).astype(BF16) for t in tiles]
        pv = [_dot(vt, x) for x in p]
        for t, x in zip(tiles, pv):
            acc_scr[:, t] = alpha[:, t] * acc_scr[:, t] + x
        return m_new

    def pair(j, carry):
        m, cmax_a = carry
        cmax_b = produce(2 * j + 1, sb_scr)
        m = consume(2 * j, sa_scr, cmax_a, m)
        cmax_a = produce(2 * j + 2, sa_scr)
        m = consume(2 * j + 1, sb_scr, cmax_b, m)
        return m, cmax_a

    m0 = jnp.full((1, 2 * tq), -jnp.inf, F32)
    trips = nk // 2 - 1
    unroll = next(u for u in (5, 3, 2, 1) if trips % u == 0 and (trips // u >= 2 or u == 1))

    def query_tile(qi, cmax_a):
        m, cmax_a = lax.fori_loop(0, trips, pair, (m0, cmax_a), unroll=unroll)
        cmax_b = produce(nk - 1, sb_scr)
        m = consume(nk - 2, sa_scr, cmax_a, m)
        load_queries(jnp.minimum(qi + 1, nq - 1))
        cmax_next = produce(0, sa_scr)
        consume(nk - 1, sb_scr, cmax_b, m)

        rows = pl.ds(pl.multiple_of(qi * tq, tq), tq)
        ot = acc_scr[0:LANES, :] * (1.0 / acc_scr[LANES:LANES + 1, :])
        acc_scr[...] = jnp.zeros(acc_scr.shape, F32)
        o = (ot[:, 0:tq] - lam * ot[:, tq:2 * tq]).T
        y = o * lax.rsqrt(jnp.mean(o * o, axis=-1, keepdims=True) + SUBLN_EPS) * g_ref[...] * (1.0 - lam_init)
        z = z_ref[0, rows, :].astype(F32)
        o_ref[0, rows, :] = (y * (z * _sigmoid(z))).astype(o_ref.dtype)
        return cmax_next

    load_queries(0)
    acc_scr[...] = jnp.zeros(acc_scr.shape, F32)
    lax.fori_loop(0, nq, query_tile, produce(0, sa_scr))


def _diff_attention(qr, kr, vt, p3, z_off, lam_q1, lam_k1, lam_q2, lam_k2, subln_g, lam_init):
    b, h, s, _ = qr.shape
    nk, tk = vt.shape[2], vt.shape[4]
    tq = _tile(s, TOKEN_ROWS, LANES)
    qt = _tile(2 * tq, CHANNEL_COLS, LANES)
    assert z_off % LANES == 0 and nk % 2 == 0
    zc = z_off // LANES
    vec = lambda n: pl.BlockSpec((1, n), lambda bi, hi: (0, 0))
    head = pl.BlockSpec((1, 1, s, LANES), lambda bi, hi: (bi, hi, 0, 0))
    kern = functools.partial(_attn_kernel, tq=tq, qt=qt, lam_init=lam_init)
    return pl.pallas_call(
        kern,
        grid=(b, h),
        in_specs=[
            head, head,
            pl.BlockSpec((1, 1, nk, VT_ROWS, tk), lambda bi, hi: (bi, hi, 0, 0, 0)),
            pl.BlockSpec((1, s, LANES), lambda bi, hi: (bi, 0, zc + hi)),
            vec(DH_A), vec(DH_A), vec(DH_A), vec(DH_A), vec(LANES),
        ],
        out_specs=pl.BlockSpec((1, s, LANES), lambda bi, hi: (bi, 0, hi)),
        out_shape=jax.ShapeDtypeStruct((b, s, h * LANES), BF16),
        scratch_shapes=[
            pltpu.VMEM((2 * tq, LANES), BF16),
            pltpu.VMEM((VT_ROWS, 2 * tq), F32),
            pltpu.VMEM((tk, 2 * tq), F32),
            pltpu.VMEM((tk, 2 * tq), F32),
        ],
        compiler_params=_params(("parallel", "arbitrary"), VMEM_BIG),
        name="diff_attn",
    )(qr, kr, vt, p3, lam_q1.reshape(1, DH_A), lam_k1.reshape(1, DH_A), lam_q2.reshape(1, DH_A),
      lam_k2.reshape(1, DH_A), subln_g.reshape(1, LANES))


def _group_sum(x, gmat):
    hi = x.astype(BF16)
    lo = (x - hi.astype(F32)).astype(BF16)
    return _dot(hi, gmat) + _dot(lo, gmat)


def _group_matrix(cw):
    r = lax.broadcasted_iota(jnp.int32, (cw, cw), 0) // N_B
    c = lax.broadcasted_iota(jnp.int32, (cw, cw), 1) // N_B
    return jnp.where(r == c, 1.0, 0.0).astype(BF16)


def _rwkv_prep_kernel(r_ref, k_ref, v_ref, lo_ref, rp_ref, kp_ref, vp_ref, lop_ref, rn_ref, kn_ref, vn_ref,
                      lon_ref, mu_ref, mulo_ref, w0_ref, a0_ref, wl_ref, al_ref, kk_ref, ka_ref, rk_ref,
                      ro_ref, vo_ref, ao_ref, lw_ref, kd_ref, bd_ref, bonus_ref, pw_scr, pa_scr):
    ts = r_ref.shape[1]
    si = pl.program_id(1)
    first = si == 0
    last = si == pl.num_programs(1) - 1

    def shift(cur_ref, prev_ref, next_ref, mu_p, mu_n):
        x = cur_ref[0].astype(F32)
        hp = prev_ref[0].astype(F32)
        hn = next_ref[0].astype(F32)
        pr = jnp.where(first, 0.0, hp[hp.shape[0] - 1:hp.shape[0], :])
        nx = jnp.where(last, 0.0, hn[0:1, :])
        row = lax.broadcasted_iota(jnp.int32, x.shape, 0)
        prev = jnp.where(row == 0, pr, pltpu.roll(x, 1, axis=0))
        nxt = jnp.where(row == ts - 1, nx, pltpu.roll(x, ts - 1, axis=0))
        return x + mu_p * (prev - x) + mu_n * (nxt - x)

    r = shift(r_ref, rp_ref, rn_ref, mu_ref[0, 0:1, :], mu_ref[1, 0:1, :])
    k = shift(k_ref, kp_ref, kn_ref, mu_ref[0, 1:2, :], mu_ref[1, 1:2, :])
    v = shift(v_ref, vp_ref, vn_ref, mu_ref[0, 2:3, :], mu_ref[1, 2:3, :])

    @pl.when(pl.program_id(2) == 0)
    def _():
        lo = shift(lo_ref, lop_ref, lon_ref, mulo_ref[0:1, :], mulo_ref[1:2, :])
        rl = lo.shape[1] // 4
        for d in range(2):
            pw_scr[d] = jnp.tanh(lo[:, d * rl:(d + 1) * rl]).astype(BF16)
            pa_scr[d] = lo[:, (2 + d) * rl:(3 + d) * rl].astype(BF16)

    cw = r.shape[1]
    gmat = _group_matrix(cw)
    kkh = k * kk_ref[...]
    nrm = jnp.sqrt(_group_sum(kkh * kkh, gmat))
    kk = kkh / jnp.maximum(nrm, 1e-12)
    ka = ka_ref[...]
    ro_ref[0] = r.astype(ro_ref.dtype)
    vo_ref[0] = v.astype(vo_ref.dtype)
    ao_ref[0] = (-kk).astype(ao_ref.dtype)
    ksum = jnp.zeros_like(k)
    for d in range(2):
        pw = pw_scr[d]
        pa = pa_scr[d]
        wl = w0_ref[d:d + 1, :] + _dot(pw, wl_ref[d])
        lw_ref[d, 0] = -math.exp(-0.5) * _sigmoid(wl)
        a = _sigmoid(a0_ref[d:d + 1, :] + _dot(pa, al_ref[d]))
        kdir = k * (1.0 + (a - 1.0) * ka)
        kd_ref[d, 0] = kdir.astype(kd_ref.dtype)
        bd_ref[d, 0] = (kk * a).astype(bd_ref.dtype)
        ksum = ksum + kdir
    bonus_ref[0] = _group_sum(r * ksum * rk_ref[...], gmat) * v


def _rwkv_prep(p3, s_off, w_b, r_lora, mu_prev, mu_next, w0, w_lora, a0, a_lora, k_k, k_a, r_k):
    b, s, _ = p3.shape
    ts = _tile(s, RWKV_ROWS, BF16_ROWS)
    cw = _tile(w_b, CHANNEL_COLS, LANES)
    lw = 4 * r_lora
    halo = BF16_ROWS
    nblk = s // halo
    assert s_off % cw == 0 and w_b % cw == 0 and (s_off + 3 * w_b) % lw == 0 and ts % halo == 0
    rc, kc, vc, lc = s_off // cw, (s_off + w_b) // cw, (s_off + 2 * w_b) // cw, (s_off + 3 * w_b) // lw
    per = ts // halo

    cur = lambda off: pl.BlockSpec((1, ts, cw), lambda bi, si, ji: (bi, si, off + ji))
    prv = lambda off: pl.BlockSpec((1, halo, cw), lambda bi, si, ji: (bi, jnp.maximum(si * per - 1, 0), off + ji))
    nxt = lambda off: pl.BlockSpec((1, halo, cw),
                                   lambda bi, si, ji: (bi, jnp.minimum((si + 1) * per, nblk - 1), off + ji))
    lo_cur = pl.BlockSpec((1, ts, lw), lambda bi, si, ji: (bi, si, lc))
    lo_prv = pl.BlockSpec((1, halo, lw), lambda bi, si, ji: (bi, jnp.maximum(si * per - 1, 0), lc))
    lo_nxt = pl.BlockSpec((1, halo, lw), lambda bi, si, ji: (bi, jnp.minimum((si + 1) * per, nblk - 1), lc))
    colvec = lambda rows: pl.BlockSpec((rows, cw), lambda bi, si, ji: (0, ji))

    mu_rkv = jnp.stack([mu_prev[:3 * w_b].reshape(3, w_b), mu_next[:3 * w_b].reshape(3, w_b)])
    mu_lo = jnp.stack([mu_prev[3 * w_b:], mu_next[3 * w_b:]])
    tok = pl.BlockSpec((1, ts, cw), lambda bi, si, ji: (bi, si, ji))
    tok2 = pl.BlockSpec((2, 1, ts, cw), lambda bi, si, ji: (0, bi, si, ji))
    shp = lambda dt: jax.ShapeDtypeStruct((b, s, w_b), dt)
    shp2 = lambda dt: jax.ShapeDtypeStruct((2, b, s, w_b), dt)
    return pl.pallas_call(
        _rwkv_prep_kernel,
        grid=(b, s // ts, w_b // cw),
        in_specs=[cur(rc), cur(kc), cur(vc), lo_cur, prv(rc), prv(kc), prv(vc), lo_prv,
                  nxt(rc), nxt(kc), nxt(vc), lo_nxt,
                  pl.BlockSpec((2, 3, cw), lambda bi, si, ji: (0, 0, ji)),
                  pl.BlockSpec((2, lw), lambda bi, si, ji: (0, 0)),
                  colvec(2), colvec(2),
                  pl.BlockSpec((2, r_lora, cw), lambda bi, si, ji: (0, 0, ji)),
                  pl.BlockSpec((2, r_lora, cw), lambda bi, si, ji: (0, 0, ji)),
                  colvec(1), colvec(1), colvec(1)],
        out_specs=[tok, tok, tok, tok2, tok2, tok2, tok],
        out_shape=[shp(BF16), shp(BF16), shp(BF16), shp2(F32), shp2(BF16), shp2(BF16), shp(F32)],
        scratch_shapes=[pltpu.VMEM((2, ts, r_lora), BF16), pltpu.VMEM((2, ts, r_lora), BF16)],
        compiler_params=_params(("parallel", "parallel", "arbitrary"), VMEM_MID),
        name="rwkv_prep",
    )(p3, p3, p3, p3, p3, p3, p3, p3, p3, p3, p3, p3, mu_rkv, mu_lo, w0, a0,
      w_lora.astype(BF16), a_lora.astype(BF16), k_k.reshape(1, w_b), k_a.reshape(1, w_b), r_k.reshape(1, w_b))


def _wkv_kernel(mask_ref, r_ref, a_ref, v_ref, lw_ref, k_ref, b_ref, y_ref, z_scr, *, npairs, reverse):
    assert CHUNK == N_B and 2 * N_B == LANES
    c = CHUNK
    ci = pl.program_id(2)

    @pl.when(ci == 0)
    def _():
        z_scr[...] = jnp.zeros(z_scr.shape, F32)

    strict = mask_ref[0]
    incl = mask_ref[1]
    cum = incl[0:c, 0:c].astype(BF16)
    lane = lax.broadcasted_iota(jnp.int32, (c, LANES), 1)
    low = lane < N_B
    row2 = lax.broadcasted_iota(jnp.int32, (LANES, LANES), 0)
    col2 = lax.broadcasted_iota(jnp.int32, (LANES, LANES), 1)
    own = (row2 < c) == (col2 < N_B)

    def stack(x):
        return jnp.concatenate([jnp.where(low, x, 0.0), jnp.where(low, 0.0, x)], axis=0)

    def each(f, *cols):
        return [f(*xs) for xs in zip(*cols)]

    sls = [slice(j * LANES, (j + 1) * LANES) for j in range(npairs)]
    lw = [lw_ref[0, 0, :, sl] for sl in sls]
    r = [r_ref[0, :, sl].astype(F32) for sl in sls]
    a = [a_ref[0, :, sl].astype(F32) for sl in sls]
    v = [v_ref[0, :, sl].astype(F32) for sl in sls]
    k = [k_ref[0, 0, :, sl].astype(F32) for sl in sls]
    b = [b_ref[0, 0, :, sl].astype(F32) for sl in sls]
    zb = [stack(z_scr[j]).astype(BF16) for j in range(npairs)]
    strict_p = strict[0:c, :] + strict[c:, :]
    incl_p = incl[0:c, :] + incl[c:, :]
    eye_p = lax.broadcasted_iota(jnp.int32, (c, LANES), 0) == (lane % N_B)

    def fold(x):
        return x[0:c, :] + x[c:, :]

    hi = each(lambda x: x.astype(BF16), lw)
    lo = each(lambda x, h: (x - h.astype(F32)).astype(BF16), lw, hi)
    cum2 = jnp.concatenate([cum, cum], axis=1)
    lam = each(lambda h, l: _dot(cum2, jnp.concatenate([h, l], axis=0)), hi, lo)
    tot = each(lambda x: jnp.sum(x, axis=0, keepdims=True), lw)
    gdec = each(jnp.exp, tot)
    a_p = each(lambda x, lm, w: x * jnp.exp(lm - w), a, lam, lw)
    r_p = each(lambda x, lm: x * jnp.exp(lm), r, lam)
    a_s = each(stack, a_p)
    e_neg = each(lambda lm: jnp.exp(-lm), lam)
    e_rem = each(lambda t, lm: jnp.exp(t - lm), tot, lam)
    b_s = each(lambda x, e: stack(x * e).astype(BF16), b, e_neg)
    k_s = each(lambda x, e: stack(x * e).astype(BF16), k, e_neg)
    bh_s = each(lambda x, e: stack(x * e).astype(BF16), b, e_rem)
    kh_s = each(lambda x, e: stack(x * e).astype(BF16), k, e_rem)
    v_s = each(lambda x: stack(x).astype(BF16), v)

    m4 = each(lambda x, y, p, q: _dot_nt(jnp.concatenate([x, y], axis=0).astype(BF16),
                                         jnp.concatenate([p, q], axis=0)), a_p, r_p, b_s, k_s)
    lp = each(lambda m: stack(m[0:c, 0:LANES] * strict_p).astype(BF16), m4)
    lak = each(lambda m: stack(m[0:c, LANES:] * strict_p).astype(BF16), m4)
    mrbk = each(lambda m: jnp.concatenate([stack(m[c:, 0:LANES] * incl_p), stack(m[c:, LANES:] * incl_p)],
                                          axis=1).astype(BF16), m4)

    half = LANES // 2
    x = each(lambda p, q, w: p + pltpu.roll(_dot(q, w), half, axis=1), a_s, lak, v_s)

    def live(arr, p):
        lo0 = 0 if reverse else p
        return jnp.concatenate([arr[lo0:lo0 + c - p], arr[c + lo0:c + lo0 + c - p]], axis=0)

    def add_live(full, part, p):
        n, lo0 = c - p, (0 if reverse else p)
        pieces = []
        for blk in range(2):
            base = blk * c
            pieces += [full[base:base + lo0], full[base + lo0:base + lo0 + n] + part[blk * n:(blk + 1) * n],
                       full[base + lo0 + n:base + c]]
        return jnp.concatenate([q for q in pieces if q.shape[0]], axis=0)

    steps = int(math.log2(c))
    for it in range(steps - 1):
        p = 2 ** it
        if p < BF16_ROWS:
            res = each(lambda l, q: _dot(l, jnp.concatenate([q.astype(BF16), l], axis=1)), lp, x)
            x = each(lambda q, s: q + s[:, 0:LANES], x, res)
            lp = each(lambda s: s[:, LANES:].astype(BF16), res)
        else:
            res = each(lambda l, q: _dot(live(l, p), jnp.concatenate([q.astype(BF16), l], axis=1)), lp, x)
            x = each(lambda q, s: add_live(q, s[:, 0:LANES], p), x, res)
            lp = each(lambda s: add_live(jnp.zeros((2 * c, LANES), F32), s[:, LANES:], p).astype(BF16), res)
    p = 2 ** (steps - 1)
    x = each(lambda l, q: add_live(q, _dot(live(l, p), q.astype(BF16)), p), lp, x)
    w_s = each(lambda q: jnp.where(own, q, 0.0).astype(BF16), x)
    u_s = each(lambda q: jnp.where(own, pltpu.roll(q, half, axis=1), 0.0).astype(BF16), x)

    zero = jnp.zeros((LANES, LANES), BF16)
    wuv = each(lambda w, u, vv: jnp.concatenate([jnp.concatenate([w, u], axis=1),
                                                 jnp.concatenate([zero, vv], axis=1)], axis=0), w_s, u_s, v_s)
    ry = each(_dot, mrbk, wuv)
    pq = each(lambda p, q, m: _dot_tn(jnp.concatenate([p, q], axis=0), m), bh_s, kh_s, wuv)
    rw = each(lambda p, s: (p + fold(s[:, 0:LANES])).astype(BF16), r_p, ry)
    pt = each(lambda g, s: (jnp.where(eye_p, g, 0.0) + fold(s[:, 0:LANES])).astype(BF16), gdec, pq)

    ys = each(lambda p, z, s: _dot(p, z) + fold(s[:, LANES:]), rw, zb, ry)
    z_new = each(lambda p, z, s: _dot(p, z) + fold(s[:, LANES:]), pt, zb, pq)
    for j, sl in enumerate(sls):
        y_ref[0, :, sl] = ys[j].astype(y_ref.dtype)
        z_scr[j] = z_new[j]


def _wkv_masks(reverse):
    i = np.arange(LANES)
    same = (i[:, None] // CHUNK) == (i[None, :] // CHUNK)
    t, s = i[:, None] % CHUNK, i[None, :] % CHUNK
    before = (s > t) if reverse else (s < t)
    return jnp.asarray(np.stack([same & before, same & (before | (s == t))]).astype(np.float32))


def _wkv_scan(r, a, v, logw, kdir, bdir, reverse):
    b, s, w_b = r.shape
    c = CHUNK
    nc = s // c
    d = int(reverse)
    gw = _tile(w_b, WKV_GROUP_LANES, LANES)
    npairs = gw // LANES
    chunk_of = (lambda ci: nc - 1 - ci) if reverse else (lambda ci: ci)
    shared = pl.BlockSpec((1, c, gw), lambda bi, gi, ci: (bi, chunk_of(ci), gi))
    perdir = pl.BlockSpec((1, 1, c, gw), lambda bi, gi, ci: (d, bi, chunk_of(ci), gi))
    kern = functools.partial(_wkv_kernel, npairs=npairs, reverse=reverse)
    return pl.pallas_call(
        kern,
        grid=(b, w_b // gw, nc),
        in_specs=[pl.BlockSpec((2, LANES, LANES), lambda bi, gi, ci: (0, 0, 0)),
                  shared, shared, shared, perdir, perdir, perdir],
        out_specs=shared,
        out_shape=jax.ShapeDtypeStruct((b, s, w_b), BF16),
        scratch_shapes=[pltpu.VMEM((npairs, N_B, LANES), F32)],
        compiler_params=_params(("parallel", "parallel", "arbitrary"), VMEM_SMALL),
        name="wkv_scan_rev" if reverse else "wkv_scan_fwd",
    )(_wkv_masks(reverse), r, a, v, logw, kdir, bdir)


def _rwkv_out_kernel(yf_ref, yb_ref, bonus_ref, z_ref, g_ref, b_ref, o_ref):
    y = yf_ref[0].astype(F32) + yb_ref[0].astype(F32)
    gmat = _group_matrix(y.shape[1])
    mu = _group_sum(y, gmat) * (1.0 / N_B)
    yc = y - mu
    var = _group_sum(yc * yc, gmat) * (1.0 / N_B)
    yn = yc * lax.rsqrt(var + LNX_EPS) * g_ref[...] + b_ref[...]
    z = z_ref[0].astype(F32)
    o_ref[0] = ((yn + bonus_ref[0]) * (z * _sigmoid(z))).astype(o_ref.dtype)


def _rwkv_out(y_fwd, y_rev, bonus, p3, z_off, lnx_g, lnx_b):
    b, s, w_b = y_fwd.shape
    ts = _tile(s, RWKV_ROWS, BF16_ROWS)
    cw = _tile(w_b, CHANNEL_COLS, LANES)
    assert z_off % cw == 0
    zc = z_off // cw
    vec = pl.BlockSpec((1, cw), lambda bi, si, ji: (0, ji))
    tok = pl.BlockSpec((1, ts, cw), lambda bi, si, ji: (bi, si, ji))
    return pl.pallas_call(
        _rwkv_out_kernel,
        grid=(b, s // ts, w_b // cw),
        in_specs=[tok, tok, tok,
                  pl.BlockSpec((1, ts, cw), lambda bi, si, ji: (bi, si, zc + ji)),
                  vec, vec],
        out_specs=pl.BlockSpec((1, ts, cw), lambda bi, si, ji: (bi, si, ji)),
        out_shape=jax.ShapeDtypeStruct((b, s, w_b), BF16),
        compiler_params=_params(("parallel", "parallel", "parallel"), VMEM_SMALL),
        name="rwkv_out",
    )(y_fwd, y_rev, bonus, p3, lnx_g.reshape(1, w_b), lnx_b.reshape(1, w_b))


def _merge_kernel(ua_ref, ub_ref, wa_ref, wb_ref, ga_ref, gb_ref, o_ref):
    oa = _dot(ua_ref[...], wa_ref[...])
    ob = _dot(ub_ref[...], wb_ref[...])
    m = _sigmoid(ga_ref[...].astype(F32)) * oa + _sigmoid(gb_ref[...].astype(F32)) * ob
    o_ref[...] = m.astype(o_ref.dtype)


def _merge(ua, ub, w_oa, w_ob, p2, g_off):
    t, w_a = ua.shape
    w_b = ub.shape[1]
    d = w_oa.shape[1]
    tm = _tile(t, PROJ_ROWS, BF16_ROWS)
    tn = _tile(d, PROJ_COLS, MXU_COLS)
    assert g_off % tn == 0 and d % tn == 0
    gc = g_off // tn
    nd = d // tn
    return pl.pallas_call(
        _merge_kernel,
        grid=(t // tm, nd),
        in_specs=[pl.BlockSpec((tm, w_a), lambda i, j: (i, 0)),
                  pl.BlockSpec((tm, w_b), lambda i, j: (i, 0)),
                  pl.BlockSpec((w_a, tn), lambda i, j: (0, j)),
                  pl.BlockSpec((w_b, tn), lambda i, j: (0, j)),
                  pl.BlockSpec((tm, tn), lambda i, j: (i, gc + j)),
                  pl.BlockSpec((tm, tn), lambda i, j: (i, gc + nd + j))],
        out_specs=pl.BlockSpec((tm, tn), lambda i, j: (i, j)),
        out_shape=jax.ShapeDtypeStruct((t, d), BF16),
        compiler_params=_params(("parallel", "arbitrary"), VMEM_MID),
        name="gate_merge",
    )(ua, ub, w_oa, w_ob, p2, p2)


def _out_kernel(m_ref, w_ref, x_ref, o_ref):
    o_ref[...] = x_ref[...] + _dot(m_ref[...], w_ref[...])


def _out_norm_kernel(m_ref, w_ref, x_ref, g_ref, o_ref, res_scr, ss_scr):
    j = pl.program_id(1)
    nj, _, tn = res_scr.shape
    res = x_ref[...] + _dot(m_ref[...], w_ref[...])
    res_scr[j] = res

    @pl.when(j == 0)
    def _():
        ss_scr[...] = jnp.zeros(ss_scr.shape, F32)

    ss_scr[...] += jnp.sum(res * res, axis=-1, keepdims=True)

    @pl.when(j == nj - 1)
    def _():
        scale = lax.rsqrt(ss_scr[...] * (1.0 / (nj * tn)) + NORM_EPS)
        for jj in range(nj):
            cols = slice(jj * tn, (jj + 1) * tn)
            o_ref[:, cols] = res_scr[jj] * scale * g_ref[:, cols]


def _out_proj(m, w_out, x2, final_g, final_norm):
    t, d = x2.shape
    tm = _tile(t, TOKEN_ROWS, BF16_ROWS)
    tn = _tile(d, PROJ_COLS, MXU_COLS)
    in_specs = [pl.BlockSpec((tm, d), lambda i, j: (i, 0)),
                pl.BlockSpec((d, tn), lambda i, j: (0, j)),
                pl.BlockSpec((tm, tn), lambda i, j: (i, j))]
    if not final_norm:
        return pl.pallas_call(
            _out_kernel,
            grid=(t // tm, d // tn),
            in_specs=in_specs,
            out_specs=pl.BlockSpec((tm, tn), lambda i, j: (i, j)),
            out_shape=jax.ShapeDtypeStruct((t, d), F32),
            compiler_params=_params(("parallel", "arbitrary"), VMEM_BIG),
            name="out_proj",
        )(m, w_out, x2)
    return pl.pallas_call(
        _out_norm_kernel,
        grid=(t // tm, d // tn),
        in_specs=in_specs + [pl.BlockSpec((1, d), lambda i, j: (0, 0))],
        out_specs=pl.BlockSpec((tm, d), lambda i, j: (i, 0)),
        out_shape=jax.ShapeDtypeStruct((t, d), F32),
        scratch_shapes=[pltpu.VMEM((d // tn, tm, tn), F32), pltpu.VMEM((tm, 1), F32)],
        compiler_params=_params(("parallel", "arbitrary"), VMEM_BIG),
        name="out_proj_norm",
    )(m, w_out, x2, final_g.reshape(1, d))


def _lambda_init(layer_idx):
    return 0.8 - 0.6 * math.exp(-0.3 * layer_idx)


def _mixer_layer(x, l, prm, final_g, final_norm):
    b, s, d = x.shape
    w_a = prm["w_oA"][l].shape[0]
    w_b = prm["w_oB"][l].shape[0]
    r_lora = prm["w_lora"].shape[2]
    c_shift = 3 * w_b + 4 * r_lora
    n_in = prm["w_in"].shape[2]
    assert n_in == 3 * w_a + c_shift + w_a + w_b + 2 * d
    assert w_a % LANES == 0 and w_b % LANES == 0 and s % CHUNK == 0
    s_off = 3 * w_a
    za_off = s_off + c_shift
    zb_off = za_off + w_a
    g_off = zb_off + w_b
    lam_init = _lambda_init(l)

    x2 = x.reshape(b * s, d)
    h = _rmsnorm_bf16(x2, prm["norm_g"][l])
    p2 = _matmul_bf16(h, prm["w_in"][l])
    p3 = p2.reshape(b, s, n_in)

    qr, kr, vt = _attn_prepass(p3, w_a)
    ua = _diff_attention(qr, kr, vt, p3, za_off, prm["lam_q1"][l], prm["lam_k1"][l], prm["lam_q2"][l],
                         prm["lam_k2"][l], prm["subln_g"][l], lam_init)

    r, v, a, logw, kdir, bdir, bonus = _rwkv_prep(
        p3, s_off, w_b, r_lora, prm["mu_prev"][l], prm["mu_next"][l], prm["w0"][l], prm["w_lora"][l],
        prm["a0"][l], prm["a_lora"][l], prm["k_k"][l], prm["k_a"][l], prm["r_k"][l])
    y_fwd = _wkv_scan(r, a, v, logw, kdir, bdir, reverse=False)
    y_rev = _wkv_scan(r, a, v, logw, kdir, bdir, reverse=True)
    ub = _rwkv_out(y_fwd, y_rev, bonus, p3, zb_off, prm["lnx_g"][l], prm["lnx_b"][l])

    m = _merge(ua.reshape(b * s, w_a), ub.reshape(b * s, w_b), prm["w_oA"][l].astype(BF16),
               prm["w_oB"][l].astype(BF16), p2, g_off)
    out = _out_proj(m, prm["w_out"][l].astype(BF16), x2, final_g, final_norm)
    return out.reshape(b, s, d)


def kernel(x_prompt, x_sample, norm_g, w_in, mu_prev, mu_next, lam_q1, lam_k1, lam_q2, lam_k2, subln_g, w0,
           w_lora, a0, a_lora, k_k, k_a, r_k, lnx_g, lnx_b, w_oA, w_oB, w_out, final_g):
    prm = dict(norm_g=norm_g, w_in=w_in, mu_prev=mu_prev, mu_next=mu_next, lam_q1=lam_q1, lam_k1=lam_k1,
               lam_q2=lam_q2, lam_k2=lam_k2, subln_g=subln_g, w0=w0, w_lora=w_lora, a0=a0, a_lora=a_lora,
               k_k=k_k, k_a=k_a, r_k=r_k, lnx_g=lnx_g, lnx_b=lnx_b, w_oA=w_oA, w_oB=w_oB, w_out=w_out)
    depth = norm_g.shape[0]

    def trunk(x):
        for l in range(depth):
            x = _mixer_layer(x, l, prm, final_g, final_norm=(l == depth - 1))
        return x

    return (trunk(x_prompt), trunk(x_sample))
```

```python
import functools
import math

import jax
import jax.numpy as jnp
import numpy as np
from jax import lax
from jax.experimental import pallas as pl
from jax.experimental.pallas import tpu as pltpu

F32 = jnp.float32
BF16 = jnp.bfloat16

LANES = 128
BF16_ROWS = 16
MXU_COLS = 256
PROJ_ROWS = 1024
PROJ_COLS = 512
TOKEN_ROWS = 512
RWKV_ROWS = 1024
RWKV_OUT_ROWS = 2048
NORM_ROWS = 256
CHANNEL_COLS = 256
VMEM_SMALL, VMEM_MID, VMEM_BIG = 32, 48, 56
DH_A = 64
N_B = 64
CHUNK = 64
ROPE_THETA = 10000.0
ATTN_SCALE = DH_A ** -0.5
LOG2_E = math.log2(math.e)
ATTN_KEY_CHUNK = 512
PREPASS_HEADS = 4
WKV_GROUP_LANES = 2048
WKV_STEP_CHUNKS = 2
VT_ROWS = LANES + BF16_ROWS
NORM_EPS = 1e-6
SUBLN_EPS = 1e-5
LNX_EPS = 64e-5
MIB = 2 ** 20


def _tile(n, target, align):
    if n <= target:
        return n
    t = (target // align) * align
    while t >= align:
        if n % t == 0:
            return t
        t -= align
    raise ValueError(f"no tile for {n} (target {target}, align {align})")


def _params(semantics, vmem_mib):
    return pltpu.CompilerParams(dimension_semantics=semantics, vmem_limit_bytes=vmem_mib * MIB)


def _sigmoid(x):
    return 1.0 / (1.0 + jnp.exp(-x))


def _dot(a, b):
    return jnp.dot(a, b, preferred_element_type=F32)


def _dot_nt(a, b):
    return lax.dot_general(a, b, (((1,), (1,)), ((), ())), preferred_element_type=F32)


def _dot_tn(a, b):
    return lax.dot_general(a, b, (((0,), (0,)), ((), ())), preferred_element_type=F32)


def _rmsnorm_kernel(x_ref, g_ref, o_ref):
    x = x_ref[...]
    ms = jnp.mean(x * x, axis=-1, keepdims=True)
    o_ref[...] = (x * lax.rsqrt(ms + NORM_EPS) * g_ref[...]).astype(o_ref.dtype)


def _rmsnorm_bf16(x2, g):
    t, d = x2.shape
    tm = _tile(t, NORM_ROWS, BF16_ROWS)
    return pl.pallas_call(
        _rmsnorm_kernel,
        grid=(t // tm,),
        in_specs=[pl.BlockSpec((tm, d), lambda i: (i, 0)), pl.BlockSpec((1, d), lambda i: (0, 0))],
        out_specs=pl.BlockSpec((tm, d), lambda i: (i, 0)),
        out_shape=jax.ShapeDtypeStruct((t, d), BF16),
        compiler_params=_params(("parallel",), VMEM_SMALL),
        name="rmsnorm_cast",
    )(x2, g.reshape(1, d))


def _matmul_kernel(a_ref, w_ref, o_ref):
    o_ref[...] = _dot(a_ref[...], w_ref[...].astype(BF16)).astype(o_ref.dtype)


def _matmul_bf16(a, w):
    m, k = a.shape
    n = w.shape[1]
    tm = _tile(m, PROJ_ROWS, BF16_ROWS)
    tn = _tile(n, PROJ_COLS, MXU_COLS)
    return pl.pallas_call(
        _matmul_kernel,
        grid=(m // tm, n // tn),
        in_specs=[pl.BlockSpec((tm, k), lambda i, j: (i, 0)), pl.BlockSpec((k, tn), lambda i, j: (0, j))],
        out_specs=pl.BlockSpec((tm, tn), lambda i, j: (i, j)),
        out_shape=jax.ShapeDtypeStruct((m, n), BF16),
        compiler_params=_params(("parallel", "arbitrary"), VMEM_BIG),
        name="in_proj",
    )(a, w)


def _rope_kernel(q_ref, k_ref, v_ref, cos_ref, sin_ref, qo_ref, ko_ref, vo_ref):
    cos = cos_ref[...]
    sin = sin_ref[...]
    lane = lax.broadcasted_iota(jnp.int32, cos.shape, 1)
    first_half = (lane % DH_A) < (DH_A // 2)

    def rope(x):
        partner = jnp.where(first_half,
                            pltpu.roll(x, LANES - DH_A // 2, axis=1),
                            pltpu.roll(x, DH_A // 2, axis=1))
        return x * cos + partner * sin

    for hh in range(qo_ref.shape[1]):
        cols = slice(hh * LANES, (hh + 1) * LANES)
        qo_ref[0, hh] = (rope(q_ref[0, :, cols].astype(F32)) * (ATTN_SCALE * LOG2_E)).astype(qo_ref.dtype)
        ko_ref[0, hh] = rope(k_ref[0, :, cols].astype(F32)).astype(ko_ref.dtype)
        vo_ref[0, hh, 0, 0:LANES, :] = v_ref[0, :, cols].astype(F32).T.astype(vo_ref.dtype)
        vo_ref[0, hh, 0, LANES:, :] = jnp.ones((vo_ref.shape[3] - LANES, vo_ref.shape[4]), vo_ref.dtype)


def _rope_tables(s):
    half = DH_A // 2
    inv = 1.0 / (ROPE_THETA ** (jnp.arange(0, DH_A, 2, dtype=F32) / DH_A))
    ang = jnp.arange(s, dtype=F32)[:, None] * inv[None, :]
    cos, sin = jnp.cos(ang), jnp.sin(ang)
    reps = LANES // half
    return jnp.tile(cos, (1, reps)), jnp.tile(jnp.concatenate([-sin, sin], axis=-1), (1, reps // 2))


def _attn_prepass(p3, w_a):
    b, s, _ = p3.shape
    h = w_a // LANES
    ts = _tile(s // 2, ATTN_KEY_CHUNK, LANES)
    cos, sin = _rope_tables(s)
    head_out = jax.ShapeDtypeStruct((b, h, s, LANES), BF16)
    vt_out = jax.ShapeDtypeStruct((b, h, s // ts, VT_ROWS, ts), BF16)
    hp = _tile(h, PREPASS_HEADS, 1)
    hg = h // hp
    col = lambda off: pl.BlockSpec((1, ts, hp * LANES), lambda bi, si, hi: (bi, si, off + hi))
    tab = pl.BlockSpec((ts, LANES), lambda bi, si, hi: (si, 0))
    out = pl.BlockSpec((1, hp, ts, LANES), lambda bi, si, hi: (bi, hi, si, 0))
    out_t = pl.BlockSpec((1, hp, 1, VT_ROWS, ts), lambda bi, si, hi: (bi, hi, si, 0, 0))
    return pl.pallas_call(
        _rope_kernel,
        grid=(b, s // ts, hg),
        in_specs=[col(0), col(hg), col(2 * hg), tab, tab],
        out_specs=[out, out, out_t],
        out_shape=[head_out, head_out, vt_out],
        compiler_params=_params(("parallel", "parallel", "arbitrary"), VMEM_SMALL),
        name="attn_prepass",
    )(p3, p3, p3, cos, sin)


def _attn_kernel(q_ref, k_ref, vt_ref, z_ref, lq1_ref, lk1_ref, lq2_ref, lk2_ref, g_ref, o_ref,
                 qq_scr, acc_scr, sa_scr, sb_scr, *, tq, qt, lam_init):
    nq = q_ref.shape[2] // tq
    nk = vt_ref.shape[2]
    tk = vt_ref.shape[4]
    tiles = [slice(c * qt, (c + 1) * qt) for c in range(2 * tq // qt)]
    lane = lax.broadcasted_iota(jnp.int32, (tq, LANES), 1)
    lam = (jnp.exp(jnp.sum(lq1_ref[...] * lk1_ref[...], keepdims=True))
           - jnp.exp(jnp.sum(lq2_ref[...] * lk2_ref[...], keepdims=True)) + lam_init)

    def load_queries(qi):
        rows = pl.ds(pl.multiple_of(qi * tq, tq), tq)
        q = q_ref[0, 0, rows, :].astype(F32)
        qq_scr[0:tq, :] = jnp.where(lane < DH_A, q, 0.0).astype(BF16)
        qq_scr[tq:2 * tq, :] = jnp.where(lane >= DH_A, q, 0.0).astype(BF16)

    def produce(i, s_scr):
        off = pl.multiple_of(i * tk, tk)
        ks = k_ref[0, 0, pl.ds(off, tk), :]
        s = [_dot_nt(ks, qq_scr[t, :]) for t in tiles]
        for t, x in zip(tiles, s):
            s_scr[:, t] = x
        return jnp.concatenate([jnp.max(x, axis=0, keepdims=True) for x in s], axis=1)

    def consume(i, s_scr, cmax, m_old):
        vt = vt_ref[0, 0, i]
        m_new = jnp.maximum(m_old, cmax)
        alpha = jnp.exp2(m_old - m_new)
        p = [jnp.exp2(s_scr[:, t] - m_new[:, t]).astype(BF16) for t in tiles]
        pv = [_dot(vt, x) for x in p]
        for t, x in zip(tiles, pv):
            acc_scr[:, t] = alpha[:, t] * acc_scr[:, t] + x
        return m_new

    def pair(j, carry):
        m, cmax_a = carry
        cmax_b = produce(2 * j + 1, sb_scr)
        m = consume(2 * j, sa_scr, cmax_a, m)
        cmax_a = produce(2 * j + 2, sa_scr)
        m = consume(2 * j + 1, sb_scr, cmax_b, m)
        return m, cmax_a

    m0 = jnp.full((1, 2 * tq), -jnp.inf, F32)
    trips = nk // 2 - 1
    unroll = next(u for u in (5, 3, 2, 1) if trips % u == 0 and (trips // u >= 2 or u == 1))

    def query_tile(qi, cmax_a):
        m, cmax_a = lax.fori_loop(0, trips, pair, (m0, cmax_a), unroll=unroll)
        cmax_b = produce(nk - 1, sb_scr)
        m = consume(nk - 2, sa_scr, cmax_a, m)
        load_queries(jnp.minimum(qi + 1, nq - 1))
        cmax_next = produce(0, sa_scr)
        consume(nk - 1, sb_scr, cmax_b, m)

        rows = pl.ds(pl.multiple_of(qi * tq, tq), tq)
        ot = acc_scr[0:LANES, :] * (1.0 / acc_scr[LANES:LANES + 1, :])
        acc_scr[...] = jnp.zeros(acc_scr.shape, F32)
        o = (ot[:, 0:tq] - lam * ot[:, tq:2 * tq]).T
        y = o * lax.rsqrt(jnp.mean(o * o, axis=-1, keepdims=True) + SUBLN_EPS) * g_ref[...] * (1.0 - lam_init)
        z = z_ref[0, rows, :].astype(F32)
        o_ref[0, rows, :] = (y * (z * _sigmoid(z))).astype(o_ref.dtype)
        return cmax_next

    load_queries(0)
    acc_scr[...] = jnp.zeros(acc_scr.shape, F32)
    lax.fori_loop(0, nq, query_tile, produce(0, sa_scr))


def _diff_attention(qr, kr, vt, p3, z_off, lam_q1, lam_k1, lam_q2, lam_k2, subln_g, lam_init):
    b, h, s, _ = qr.shape
    nk, tk = vt.shape[2], vt.shape[4]
    tq = _tile(s, TOKEN_ROWS, LANES)
    qt = _tile(2 * tq, CHANNEL_COLS, LANES)
    assert z_off % LANES == 0 and nk % 2 == 0
    zc = z_off // LANES
    vec = lambda n: pl.BlockSpec((1, n), lambda bi, hi: (0, 0))
    head = pl.BlockSpec((1, 1, s, LANES), lambda bi, hi: (bi, hi, 0, 0))
    kern = functools.partial(_attn_kernel, tq=tq, qt=qt, lam_init=lam_init)
    return pl.pallas_call(
        kern,
        grid=(b, h),
        in_specs=[
            head, head,
            pl.BlockSpec((1, 1, nk, VT_ROWS, tk), lambda bi, hi: (bi, hi, 0, 0, 0)),
            pl.BlockSpec((1, s, LANES), lambda bi, hi: (bi, 0, zc + hi)),
            vec(DH_A), vec(DH_A), vec(DH_A), vec(DH_A), vec(LANES),
        ],
        out_specs=pl.BlockSpec((1, s, LANES), lambda bi, hi: (bi, 0, hi)),
        out_shape=jax.ShapeDtypeStruct((b, s, h * LANES), BF16),
        scratch_shapes=[
            pltpu.VMEM((2 * tq, LANES), BF16),
            pltpu.VMEM((VT_ROWS, 2 * tq), F32),
            pltpu.VMEM((tk, 2 * tq), F32),
            pltpu.VMEM((tk, 2 * tq), F32),
        ],
        compiler_params=_params(("parallel", "arbitrary"), VMEM_BIG),
        name="diff_attn",
    )(qr, kr, vt, p3, lam_q1.reshape(1, DH_A), lam_k1.reshape(1, DH_A), lam_q2.reshape(1, DH_A),
      lam_k2.reshape(1, DH_A), subln_g.reshape(1, LANES))


def _group_sum(x, gmat):
    hi = x.astype(BF16)
    lo = (x - hi.astype(F32)).astype(BF16)
    return _dot(hi, gmat) + _dot(lo, gmat)


def _group_matrix(cw):
    r = lax.broadcasted_iota(jnp.int32, (cw, cw), 0) // N_B
    c = lax.broadcasted_iota(jnp.int32, (cw, cw), 1) // N_B
    return jnp.where(r == c, 1.0, 0.0).astype(BF16)


def _rwkv_prep_kernel(r_ref, k_ref, v_ref, lo_ref, rp_ref, kp_ref, vp_ref, lop_ref, rn_ref, kn_ref, vn_ref,
                      lon_ref, mu_ref, mulo_ref, w0_ref, a0_ref, wl_ref, al_ref, kk_ref, ka_ref, rk_ref,
                      ro_ref, vo_ref, ao_ref, lw_ref, kd_ref, bd_ref, bonus_ref, pw_scr, pa_scr):
    ts = r_ref.shape[1]
    si = pl.program_id(1)
    first = si == 0
    last = si == pl.num_programs(1) - 1

    def shift(cur_ref, prev_ref, next_ref, mu_p, mu_n):
        x = cur_ref[0].astype(F32)
        hp = prev_ref[0].astype(F32)
        hn = next_ref[0].astype(F32)
        pr = jnp.where(first, 0.0, hp[hp.shape[0] - 1:hp.shape[0], :])
        nx = jnp.where(last, 0.0, hn[0:1, :])
        row = lax.broadcasted_iota(jnp.int32, x.shape, 0)
        prev = jnp.where(row == 0, pr, pltpu.roll(x, 1, axis=0))
        nxt = jnp.where(row == ts - 1, nx, pltpu.roll(x, ts - 1, axis=0))
        return x + mu_p * (prev - x) + mu_n * (nxt - x)

    r = shift(r_ref, rp_ref, rn_ref, mu_ref[0, 0:1, :], mu_ref[1, 0:1, :])
    k = shift(k_ref, kp_ref, kn_ref, mu_ref[0, 1:2, :], mu_ref[1, 1:2, :])
    v = shift(v_ref, vp_ref, vn_ref, mu_ref[0, 2:3, :], mu_ref[1, 2:3, :])

    @pl.when(pl.program_id(2) == 0)
    def _():
        lo = shift(lo_ref, lop_ref, lon_ref, mulo_ref[0:1, :], mulo_ref[1:2, :])
        rl = lo.shape[1] // 4
        for d in range(2):
            pw_scr[d] = jnp.tanh(lo[:, d * rl:(d + 1) * rl]).astype(BF16)
            pa_scr[d] = lo[:, (2 + d) * rl:(3 + d) * rl].astype(BF16)

    cw = r.shape[1]
    gmat = _group_matrix(cw)
    kkh = k * kk_ref[...]
    nrm = jnp.sqrt(_group_sum(kkh * kkh, gmat))
    kk = kkh / jnp.maximum(nrm, 1e-12)
    ka = ka_ref[...]
    ro_ref[0] = r.astype(ro_ref.dtype)
    vo_ref[0] = v.astype(vo_ref.dtype)
    ao_ref[0] = (-kk).astype(ao_ref.dtype)
    ksum = jnp.zeros_like(k)
    for d in range(2):
        pw = pw_scr[d]
        pa = pa_scr[d]
        wl = w0_ref[d:d + 1, :] + _dot(pw, wl_ref[d])
        lw_ref[d, 0] = -math.exp(-0.5) * _sigmoid(wl)
        a = _sigmoid(a0_ref[d:d + 1, :] + _dot(pa, al_ref[d]))
        kdir = k * (1.0 + (a - 1.0) * ka)
        kd_ref[d, 0] = kdir.astype(kd_ref.dtype)
        bd_ref[d, 0] = (kk * a).astype(bd_ref.dtype)
        ksum = ksum + kdir
    bonus_ref[0] = _group_sum(r * ksum * rk_ref[...], gmat) * v


def _rwkv_prep(p3, s_off, w_b, r_lora, mu_prev, mu_next, w0, w_lora, a0, a_lora, k_k, k_a, r_k):
    b, s, _ = p3.shape
    ts = _tile(s, RWKV_ROWS, BF16_ROWS)
    cw = _tile(w_b, CHANNEL_COLS, LANES)
    lw = 4 * r_lora
    halo = BF16_ROWS
    nblk = s // halo
    assert s_off % cw == 0 and w_b % cw == 0 and (s_off + 3 * w_b) % lw == 0 and ts % halo == 0
    rc, kc, vc, lc = s_off // cw, (s_off + w_b) // cw, (s_off + 2 * w_b) // cw, (s_off + 3 * w_b) // lw
    per = ts // halo

    cur = lambda off: pl.BlockSpec((1, ts, cw), lambda bi, si, ji: (bi, si, off + ji))
    prv = lambda off: pl.BlockSpec((1, halo, cw), lambda bi, si, ji: (bi, jnp.maximum(si * per - 1, 0), off + ji))
    nxt = lambda off: pl.BlockSpec((1, halo, cw),
                                   lambda bi, si, ji: (bi, jnp.minimum((si + 1) * per, nblk - 1), off + ji))
    lo_cur = pl.BlockSpec((1, ts, lw), lambda bi, si, ji: (bi, si, lc))
    lo_prv = pl.BlockSpec((1, halo, lw), lambda bi, si, ji: (bi, jnp.maximum(si * per - 1, 0), lc))
    lo_nxt = pl.BlockSpec((1, halo, lw), lambda bi, si, ji: (bi, jnp.minimum((si + 1) * per, nblk - 1), lc))
    colvec = lambda rows: pl.BlockSpec((rows, cw), lambda bi, si, ji: (0, ji))

    mu_rkv = jnp.stack([mu_prev[:3 * w_b].reshape(3, w_b), mu_next[:3 * w_b].reshape(3, w_b)])
    mu_lo = jnp.stack([mu_prev[3 * w_b:], mu_next[3 * w_b:]])
    tok = pl.BlockSpec((1, ts, cw), lambda bi, si, ji: (bi, si, ji))
    tok2 = pl.BlockSpec((2, 1, ts, cw), lambda bi, si, ji: (0, bi, si, ji))
    shp = lambda dt: jax.ShapeDtypeStruct((b, s, w_b), dt)
    shp2 = lambda dt: jax.ShapeDtypeStruct((2, b, s, w_b), dt)
    return pl.pallas_call(
        _rwkv_prep_kernel,
        grid=(b, s // ts, w_b // cw),
        in_specs=[cur(rc), cur(kc), cur(vc), lo_cur, prv(rc), prv(kc), prv(vc), lo_prv,
                  nxt(rc), nxt(kc), nxt(vc), lo_nxt,
                  pl.BlockSpec((2, 3, cw), lambda bi, si, ji: (0, 0, ji)),
                  pl.BlockSpec((2, lw), lambda bi, si, ji: (0, 0)),
                  colvec(2), colvec(2),
                  pl.BlockSpec((2, r_lora, cw), lambda bi, si, ji: (0, 0, ji)),
                  pl.BlockSpec((2, r_lora, cw), lambda bi, si, ji: (0, 0, ji)),
                  colvec(1), colvec(1), colvec(1)],
        out_specs=[tok, tok, tok, tok2, tok2, tok2, tok],
        out_shape=[shp(BF16), shp(BF16), shp(BF16), shp2(F32), shp2(BF16), shp2(BF16), shp(F32)],
        scratch_shapes=[pltpu.VMEM((2, ts, r_lora), BF16), pltpu.VMEM((2, ts, r_lora), BF16)],
        compiler_params=_params(("parallel", "parallel", "arbitrary"), VMEM_MID),
        name="rwkv_prep",
    )(p3, p3, p3, p3, p3, p3, p3, p3, p3, p3, p3, p3, mu_rkv, mu_lo, w0, a0,
      w_lora.astype(BF16), a_lora.astype(BF16), k_k.reshape(1, w_b), k_a.reshape(1, w_b), r_k.reshape(1, w_b))


def _wkv_kernel(mask_ref, r_ref, a_ref, v_ref, lw_ref, k_ref, b_ref, y_ref, z_scr, *, npairs, reverse):
    @pl.when(pl.program_id(2) == 0)
    def _():
        z_scr[...] = jnp.zeros(z_scr.shape, F32)

    z = [z_scr[j] for j in range(npairs)]
    nsub = r_ref.shape[1] // CHUNK
    for sub in (reversed(range(nsub)) if reverse else range(nsub)):
        rows = slice(sub * CHUNK, (sub + 1) * CHUNK)
        z = _wkv_chunk(mask_ref, r_ref.at[:, rows, :], a_ref.at[:, rows, :], v_ref.at[:, rows, :],
                       lw_ref.at[:, :, rows, :], k_ref.at[:, :, rows, :], b_ref.at[:, :, rows, :],
                       y_ref.at[:, rows, :], z, npairs=npairs, reverse=reverse)
    for j in range(npairs):
        z_scr[j] = z[j]


def _wkv_chunk(mask_ref, r_ref, a_ref, v_ref, lw_ref, k_ref, b_ref, y_ref, z_in, *, npairs, reverse):
    assert CHUNK == N_B and 2 * N_B == LANES
    c = CHUNK
    strict = mask_ref[0]
    incl = mask_ref[1]
    cum = incl[0:c, 0:c].astype(BF16)
    lane = lax.broadcasted_iota(jnp.int32, (c, LANES), 1)
    low = lane < N_B
    row2 = lax.broadcasted_iota(jnp.int32, (LANES, LANES), 0)
    col2 = lax.broadcasted_iota(jnp.int32, (LANES, LANES), 1)
    own = (row2 < c) == (col2 < N_B)

    def stack(x):
        return jnp.concatenate([jnp.where(low, x, 0.0), jnp.where(low, 0.0, x)], axis=0)

    def each(f, *cols):
        return [f(*xs) for xs in zip(*cols)]

    sls = [slice(j * LANES, (j + 1) * LANES) for j in range(npairs)]
    lw = [lw_ref[0, 0, :, sl] for sl in sls]
    r = [r_ref[0, :, sl].astype(F32) for sl in sls]
    a = [a_ref[0, :, sl].astype(F32) for sl in sls]
    v = [v_ref[0, :, sl].astype(F32) for sl in sls]
    k = [k_ref[0, 0, :, sl].astype(F32) for sl in sls]
    b = [b_ref[0, 0, :, sl].astype(F32) for sl in sls]
    zb = [stack(z).astype(BF16) for z in z_in]
    strict_p = strict[0:c, :] + strict[c:, :]
    incl_p = incl[0:c, :] + incl[c:, :]
    eye_p = lax.broadcasted_iota(jnp.int32, (c, LANES), 0) == (lane % N_B)

    def fold(x):
        return x[0:c, :] + x[c:, :]

    hi = each(lambda x: x.astype(BF16), lw)
    lo = each(lambda x, h: (x - h.astype(F32)).astype(BF16), lw, hi)
    cum2 = jnp.concatenate([cum, cum], axis=1)
    lam = each(lambda h, l: _dot(cum2, jnp.concatenate([h, l], axis=0)), hi, lo)
    tot = each(lambda x: jnp.sum(x, axis=0, keepdims=True), lw)
    gdec = each(jnp.exp, tot)
    a_p = each(lambda x, lm, w: x * jnp.exp(lm - w), a, lam, lw)
    r_p = each(lambda x, lm: x * jnp.exp(lm), r, lam)
    a_s = each(stack, a_p)
    e_neg = each(lambda lm: jnp.exp(-lm), lam)
    e_rem = each(lambda t, lm: jnp.exp(t - lm), tot, lam)
    b_s = each(lambda x, e: stack(x * e).astype(BF16), b, e_neg)
    k_s = each(lambda x, e: stack(x * e).astype(BF16), k, e_neg)
    bh_s = each(lambda x, e: stack(x * e).astype(BF16), b, e_rem)
    kh_s = each(lambda x, e: stack(x * e).astype(BF16), k, e_rem)
    v_s = each(lambda x: stack(x).astype(BF16), v)

    m4 = each(lambda x, y, p, q: _dot_nt(jnp.concatenate([x, y], axis=0).astype(BF16),
                                         jnp.concatenate([p, q], axis=0)), a_p, r_p, b_s, k_s)
    lp = each(lambda m: stack(m[0:c, 0:LANES] * strict_p).astype(BF16), m4)
    lak = each(lambda m: stack(m[0:c, LANES:] * strict_p).astype(BF16), m4)
    mrbk = each(lambda m: jnp.concatenate([stack(m[c:, 0:LANES] * incl_p), stack(m[c:, LANES:] * incl_p)],
                                          axis=1).astype(BF16), m4)

    half = LANES // 2
    x = each(lambda p, q, w: p + pltpu.roll(_dot(q, w), half, axis=1), a_s, lak, v_s)

    def live(arr, p):
        lo0 = 0 if reverse else p
        return jnp.concatenate([arr[lo0:lo0 + c - p], arr[c + lo0:c + lo0 + c - p]], axis=0)

    def add_live(full, part, p):
        n, lo0 = c - p, (0 if reverse else p)
        pieces = []
        for blk in range(2):
            base = blk * c
            pieces += [full[base:base + lo0], full[base + lo0:base + lo0 + n] + part[blk * n:(blk + 1) * n],
                       full[base + lo0 + n:base + c]]
        return jnp.concatenate([q for q in pieces if q.shape[0]], axis=0)

    steps = int(math.log2(c))
    for it in range(steps - 1):
        p = 2 ** it
        if p < BF16_ROWS:
            res = each(lambda l, q: _dot(l, jnp.concatenate([q.astype(BF16), l], axis=1)), lp, x)
            x = each(lambda q, s: q + s[:, 0:LANES], x, res)
            lp = each(lambda s: s[:, LANES:].astype(BF16), res)
        else:
            res = each(lambda l, q: _dot(live(l, p), jnp.concatenate([q.astype(BF16), l], axis=1)), lp, x)
            x = each(lambda q, s: add_live(q, s[:, 0:LANES], p), x, res)
            lp = each(lambda s: add_live(jnp.zeros((2 * c, LANES), F32), s[:, LANES:], p).astype(BF16), res)
    p = 2 ** (steps - 1)
    x = each(lambda l, q: add_live(q, _dot(live(l, p), q.astype(BF16)), p), lp, x)
    w_s = each(lambda q: jnp.where(own, q, 0.0).astype(BF16), x)
    u_s = each(lambda q: jnp.where(own, pltpu.roll(q, half, axis=1), 0.0).astype(BF16), x)

    zero = jnp.zeros((LANES, LANES), BF16)
    wuv = each(lambda w, u, vv: jnp.concatenate([jnp.concatenate([w, u], axis=1),
                                                 jnp.concatenate([zero, vv], axis=1)], axis=0), w_s, u_s, v_s)
    ry = each(_dot, mrbk, wuv)
    pq = each(lambda p, q, m: _dot_tn(jnp.concatenate([p, q], axis=0), m), bh_s, kh_s, wuv)
    rw = each(lambda p, s: (p + fold(s[:, 0:LANES])).astype(BF16), r_p, ry)
    pt = each(lambda g, s: (jnp.where(eye_p, g, 0.0) + fold(s[:, 0:LANES])).astype(BF16), gdec, pq)

    ys = each(lambda p, z, s: _dot(p, z) + fold(s[:, LANES:]), rw, zb, ry)
    z_new = each(lambda p, z, s: _dot(p, z) + fold(s[:, LANES:]), pt, zb, pq)
    for j, sl in enumerate(sls):
        y_ref[0, :, sl] = ys[j].astype(y_ref.dtype)
    return z_new


def _wkv_masks(reverse):
    i = np.arange(LANES)
    same = (i[:, None] // CHUNK) == (i[None, :] // CHUNK)
    t, s = i[:, None] % CHUNK, i[None, :] % CHUNK
    before = (s > t) if reverse else (s < t)
    return jnp.asarray(np.stack([same & before, same & (before | (s == t))]).astype(np.float32))


def _wkv_scan(r, a, v, logw, kdir, bdir, reverse):
    b, s, w_b = r.shape
    c = _tile(s, WKV_STEP_CHUNKS * CHUNK, CHUNK)
    nc = s // c
    d = int(reverse)
    gw = _tile(w_b, WKV_GROUP_LANES, LANES)
    npairs = gw // LANES
    chunk_of = (lambda ci: nc - 1 - ci) if reverse else (lambda ci: ci)
    shared = pl.BlockSpec((1, c, gw), lambda bi, gi, ci: (bi, chunk_of(ci), gi))
    perdir = pl.BlockSpec((1, 1, c, gw), lambda bi, gi, ci: (d, bi, chunk_of(ci), gi))
    kern = functools.partial(_wkv_kernel, npairs=npairs, reverse=reverse)
    return pl.pallas_call(
        kern,
        grid=(b, w_b // gw, nc),
        in_specs=[pl.BlockSpec((2, LANES, LANES), lambda bi, gi, ci: (0, 0, 0)),
                  shared, shared, shared, perdir, perdir, perdir],
        out_specs=shared,
        out_shape=jax.ShapeDtypeStruct((b, s, w_b), BF16),
        scratch_shapes=[pltpu.VMEM((npairs, N_B, LANES), F32)],
        compiler_params=_params(("parallel", "parallel", "arbitrary"), VMEM_SMALL),
        name="wkv_scan_rev" if reverse else "wkv_scan_fwd",
    )(_wkv_masks(reverse), r, a, v, logw, kdir, bdir)


def _rwkv_out_kernel(yf_ref, yb_ref, bonus_ref, z_ref, g_ref, b_ref, o_ref):
    y = yf_ref[0].astype(F32) + yb_ref[0].astype(F32)
    gmat = _group_matrix(y.shape[1])
    mu = _group_sum(y, gmat) * (1.0 / N_B)
    yc = y - mu
    var = _group_sum(yc * yc, gmat) * (1.0 / N_B)
    yn = yc * lax.rsqrt(var + LNX_EPS) * g_ref[...] + b_ref[...]
    z = z_ref[0].astype(F32)
    o_ref[0] = ((yn + bonus_ref[0]) * (z * _sigmoid(z))).astype(o_ref.dtype)


def _rwkv_out(y_fwd, y_rev, bonus, p3, z_off, lnx_g, lnx_b):
    b, s, w_b = y_fwd.shape
    ts = _tile(s, RWKV_OUT_ROWS, BF16_ROWS)
    cw = _tile(w_b, CHANNEL_COLS, LANES)
    assert z_off % cw == 0
    zc = z_off // cw
    vec = pl.BlockSpec((1, cw), lambda bi, si, ji: (0, ji))
    tok = pl.BlockSpec((1, ts, cw), lambda bi, si, ji: (bi, si, ji))
    return pl.pallas_call(
        _rwkv_out_kernel,
        grid=(b, s // ts, w_b // cw),
        in_specs=[tok, tok, tok,
                  pl.BlockSpec((1, ts, cw), lambda bi, si, ji: (bi, si, zc + ji)),
                  vec, vec],
        out_specs=pl.BlockSpec((1, ts, cw), lambda bi, si, ji: (bi, si, ji)),
        out_shape=jax.ShapeDtypeStruct((b, s, w_b), BF16),
        compiler_params=_params(("parallel", "parallel", "parallel"), VMEM_SMALL),
        name="rwkv_out",
    )(y_fwd, y_rev, bonus, p3, lnx_g.reshape(1, w_b), lnx_b.reshape(1, w_b))


def _merge_kernel(ua_ref, ub_ref, wa_ref, wb_ref, ga_ref, gb_ref, o_ref):
    oa = _dot(ua_ref[...], wa_ref[...])
    ob = _dot(ub_ref[...], wb_ref[...])
    m = _sigmoid(ga_ref[...].astype(F32)) * oa + _sigmoid(gb_ref[...].astype(F32)) * ob
    o_ref[...] = m.astype(o_ref.dtype)


def _merge(ua, ub, w_oa, w_ob, p2, g_off):
    t, w_a = ua.shape
    w_b = ub.shape[1]
    d = w_oa.shape[1]
    tm = _tile(t, PROJ_ROWS, BF16_ROWS)
    tn = _tile(d, PROJ_COLS, MXU_COLS)
    assert g_off % tn == 0 and d % tn == 0
    gc = g_off // tn
    nd = d // tn
    return pl.pallas_call(
        _merge_kernel,
        grid=(t // tm, nd),
        in_specs=[pl.BlockSpec((tm, w_a), lambda i, j: (i, 0)),
                  pl.BlockSpec((tm, w_b), lambda i, j: (i, 0)),
                  pl.BlockSpec((w_a, tn), lambda i, j: (0, j)),
                  pl.BlockSpec((w_b, tn), lambda i, j: (0, j)),
                  pl.BlockSpec((tm, tn), lambda i, j: (i, gc + j)),
                  pl.BlockSpec((tm, tn), lambda i, j: (i, gc + nd + j))],
        out_specs=pl.BlockSpec((tm, tn), lambda i, j: (i, j)),
        out_shape=jax.ShapeDtypeStruct((t, d), BF16),
        compiler_params=_params(("parallel", "arbitrary"), VMEM_MID),
        name="gate_merge",
    )(ua, ub, w_oa, w_ob, p2, p2)


def _out_kernel(m_ref, w_ref, x_ref, o_ref):
    o_ref[...] = x_ref[...] + _dot(m_ref[...], w_ref[...])


def _out_norm_kernel(m_ref, w_ref, x_ref, g_ref, o_ref, res_scr, ss_scr):
    j = pl.program_id(1)
    nj, _, tn = res_scr.shape
    res = x_ref[...] + _dot(m_ref[...], w_ref[...])
    res_scr[j] = res

    @pl.when(j == 0)
    def _():
        ss_scr[...] = jnp.zeros(ss_scr.shape, F32)

    ss_scr[...] += jnp.sum(res * res, axis=-1, keepdims=True)

    @pl.when(j == nj - 1)
    def _():
        scale = lax.rsqrt(ss_scr[...] * (1.0 / (nj * tn)) + NORM_EPS)
        for jj in range(nj):
            cols = slice(jj * tn, (jj + 1) * tn)
            o_ref[:, cols] = res_scr[jj] * scale * g_ref[:, cols]


def _out_proj(m, w_out, x2, final_g, final_norm):
    t, d = x2.shape
    tm = _tile(t, TOKEN_ROWS, BF16_ROWS)
    tn = _tile(d, PROJ_COLS, MXU_COLS)
    in_specs = [pl.BlockSpec((tm, d), lambda i, j: (i, 0)),
                pl.BlockSpec((d, tn), lambda i, j: (0, j)),
                pl.BlockSpec((tm, tn), lambda i, j: (i, j))]
    if not final_norm:
        return pl.pallas_call(
            _out_kernel,
            grid=(t // tm, d // tn),
            in_specs=in_specs,
            out_specs=pl.BlockSpec((tm, tn), lambda i, j: (i, j)),
            out_shape=jax.ShapeDtypeStruct((t, d), F32),
            compiler_params=_params(("parallel", "arbitrary"), VMEM_BIG),
            name="out_proj",
        )(m, w_out, x2)
    return pl.pallas_call(
        _out_norm_kernel,
        grid=(t // tm, d // tn),
        in_specs=in_specs + [pl.BlockSpec((1, d), lambda i, j: (0, 0))],
        out_specs=pl.BlockSpec((tm, d), lambda i, j: (i, 0)),
        out_shape=jax.ShapeDtypeStruct((t, d), F32),
        scratch_shapes=[pltpu.VMEM((d // tn, tm, tn), F32), pltpu.VMEM((tm, 1), F32)],
        compiler_params=_params(("parallel", "arbitrary"), VMEM_BIG),
        name="out_proj_norm",
    )(m, w_out, x2, final_g.reshape(1, d))


def _lambda_init(layer_idx):
    return 0.8 - 0.6 * math.exp(-0.3 * layer_idx)


def _mixer_layer(x, l, prm, final_g, final_norm):
    b, s, d = x.shape
    w_a = prm["w_oA"][l].shape[0]
    w_b = prm["w_oB"][l].shape[0]
    r_lora = prm["w_lora"].shape[2]
    c_shift = 3 * w_b + 4 * r_lora
    n_in = prm["w_in"].shape[2]
    assert n_in == 3 * w_a + c_shift + w_a + w_b + 2 * d
    assert w_a % LANES == 0 and w_b % LANES == 0 and s % CHUNK == 0
    s_off = 3 * w_a
    za_off = s_off + c_shift
    zb_off = za_off + w_a
    g_off = zb_off + w_b
    lam_init = _lambda_init(l)

    x2 = x.reshape(b * s, d)
    h = _rmsnorm_bf16(x2, prm["norm_g"][l])
    p2 = _matmul_bf16(h, prm["w_in"][l])
    p3 = p2.reshape(b, s, n_in)

    qr, kr, vt = _attn_prepass(p3, w_a)
    ua = _diff_attention(qr, kr, vt, p3, za_off, prm["lam_q1"][l], prm["lam_k1"][l], prm["lam_q2"][l],
                         prm["lam_k2"][l], prm["subln_g"][l], lam_init)

    r, v, a, logw, kdir, bdir, bonus = _rwkv_prep(
        p3, s_off, w_b, r_lora, prm["mu_prev"][l], prm["mu_next"][l], prm["w0"][l], prm["w_lora"][l],
        prm["a0"][l], prm["a_lora"][l], prm["k_k"][l], prm["k_a"][l], prm["r_k"][l])
    y_fwd = _wkv_scan(r, a, v, logw, kdir, bdir, reverse=False)
    y_rev = _wkv_scan(r, a, v, logw, kdir, bdir, reverse=True)
    ub = _rwkv_out(y_fwd, y_rev, bonus, p3, zb_off, prm["lnx_g"][l], prm["lnx_b"][l])

    m = _merge(ua.reshape(b * s, w_a), ub.reshape(b * s, w_b), prm["w_oA"][l].astype(BF16),
               prm["w_oB"][l].astype(BF16), p2, g_off)
    out = _out_proj(m, prm["w_out"][l].astype(BF16), x2, final_g, final_norm)
    return out.reshape(b, s, d)


def kernel(x_prompt, x_sample, norm_g, w_in, mu_prev, mu_next, lam_q1, lam_k1, lam_q2, lam_k2, subln_g, w0,
           w_lora, a0, a_lora, k_k, k_a, r_k, lnx_g, lnx_b, w_oA, w_oB, w_out, final_g):
    prm = dict(norm_g=norm_g, w_in=w_in, mu_prev=mu_prev, mu_next=mu_next, lam_q1=lam_q1, lam_k1=lam_k1,
               lam_q2=lam_q2, lam_k2=lam_k2, subln_g=subln_g, w0=w0, w_lora=w_lora, a0=a0, a_lora=a_lora,
               k_k=k_k, k_a=k_a, r_k=r_k, lnx_g=lnx_g, lnx_b=lnx_b, w_oA=w_oA, w_oB=w_oB, w_out=w_out)
    depth = norm_g.shape[0]

    def trunk(x):
        for l in range(depth):
            x = _mixer_layer(x, l, prm, final_g, final_norm=(l == depth - 1))
        return x

    return (trunk(x_prompt), trunk(x_sample))
```

```python
import functools
import math

import jax
import jax.numpy as jnp
import numpy as np
from jax import lax
from jax.experimental import pallas as pl
from jax.experimental.pallas import tpu as pltpu

F32 = jnp.float32
BF16 = jnp.bfloat16

LANES = 128
BF16_ROWS = 16
MXU_COLS = 256
PROJ_ROWS = 1024
PROJ_COLS = 512
TOKEN_ROWS = 512
RWKV_ROWS = 2048
RWKV_OUT_ROWS = 2048
NORM_ROWS = 256
CHANNEL_COLS = 256
VMEM_SMALL, VMEM_MID, VMEM_BIG = 32, 48, 56
DH_A = 64
N_B = 64
CHUNK = 64
ROPE_THETA = 10000.0
ATTN_SCALE = DH_A ** -0.5
LOG2_E = math.log2(math.e)
ATTN_KEY_CHUNK = 512
PREPASS_HEADS = 8
WKV_GROUP_LANES = 2048
WKV_STEP_CHUNKS = 2
VT_ROWS = LANES + BF16_ROWS
NORM_EPS = 1e-6
SUBLN_EPS = 1e-5
LNX_EPS = 64e-5
MIB = 2 ** 20


def _tile(n, target, align):
    if n <= target:
        return n
    t = (target // align) * align
    while t >= align:
        if n % t == 0:
            return t
        t -= align
    raise ValueError(f"no tile for {n} (target {target}, align {align})")


def _params(semantics, vmem_mib):
    return pltpu.CompilerParams(dimension_semantics=semantics, vmem_limit_bytes=vmem_mib * MIB)


def _sigmoid(x):
    return 1.0 / (1.0 + jnp.exp(-x))


def _dot(a, b):
    return jnp.dot(a, b, preferred_element_type=F32)


def _dot_nt(a, b):
    return lax.dot_general(a, b, (((1,), (1,)), ((), ())), preferred_element_type=F32)


def _dot_tn(a, b):
    return lax.dot_general(a, b, (((0,), (0,)), ((), ())), preferred_element_type=F32)


def _rmsnorm_kernel(x_ref, g_ref, o_ref):
    x = x_ref[...]
    ms = jnp.mean(x * x, axis=-1, keepdims=True)
    o_ref[...] = (x * lax.rsqrt(ms + NORM_EPS) * g_ref[...]).astype(o_ref.dtype)


def _rmsnorm_bf16(x2, g):
    t, d = x2.shape
    tm = _tile(t, NORM_ROWS, BF16_ROWS)
    return pl.pallas_call(
        _rmsnorm_kernel,
        grid=(t // tm,),
        in_specs=[pl.BlockSpec((tm, d), lambda i: (i, 0)), pl.BlockSpec((1, d), lambda i: (0, 0))],
        out_specs=pl.BlockSpec((tm, d), lambda i: (i, 0)),
        out_shape=jax.ShapeDtypeStruct((t, d), BF16),
        compiler_params=_params(("parallel",), VMEM_SMALL),
        name="rmsnorm_cast",
    )(x2, g.reshape(1, d))


def _matmul_kernel(a_ref, w_ref, o_ref):
    o_ref[...] = _dot(a_ref[...], w_ref[...].astype(BF16)).astype(o_ref.dtype)


def _matmul_bf16(a, w):
    m, k = a.shape
    n = w.shape[1]
    tm = _tile(m, PROJ_ROWS, BF16_ROWS)
    tn = _tile(n, PROJ_COLS, MXU_COLS)
    return pl.pallas_call(
        _matmul_kernel,
        grid=(m // tm, n // tn),
        in_specs=[pl.BlockSpec((tm, k), lambda i, j: (i, 0)), pl.BlockSpec((k, tn), lambda i, j: (0, j))],
        out_specs=pl.BlockSpec((tm, tn), lambda i, j: (i, j)),
        out_shape=jax.ShapeDtypeStruct((m, n), BF16),
        compiler_params=_params(("parallel", "arbitrary"), VMEM_BIG),
        name="in_proj",
    )(a, w)


def _rope_kernel(q_ref, k_ref, v_ref, cos_ref, sin_ref, qo_ref, ko_ref, vo_ref):
    cos = cos_ref[...]
    sin = sin_ref[...]
    lane = lax.broadcasted_iota(jnp.int32, cos.shape, 1)
    first_half = (lane % DH_A) < (DH_A // 2)

    def rope(x):
        partner = jnp.where(first_half,
                            pltpu.roll(x, LANES - DH_A // 2, axis=1),
                            pltpu.roll(x, DH_A // 2, axis=1))
        return x * cos + partner * sin

    for hh in range(qo_ref.shape[1]):
        cols = slice(hh * LANES, (hh + 1) * LANES)
        qo_ref[0, hh] = (rope(q_ref[0, :, cols].astype(F32)) * (ATTN_SCALE * LOG2_E)).astype(qo_ref.dtype)
        ko_ref[0, hh] = rope(k_ref[0, :, cols].astype(F32)).astype(ko_ref.dtype)
        vo_ref[0, hh, 0, 0:LANES, :] = v_ref[0, :, cols].astype(F32).T.astype(vo_ref.dtype)
        vo_ref[0, hh, 0, LANES:, :] = jnp.ones((vo_ref.shape[3] - LANES, vo_ref.shape[4]), vo_ref.dtype)


def _rope_tables(s):
    half = DH_A // 2
    inv = 1.0 / (ROPE_THETA ** (jnp.arange(0, DH_A, 2, dtype=F32) / DH_A))
    ang = jnp.arange(s, dtype=F32)[:, None] * inv[None, :]
    cos, sin = jnp.cos(ang), jnp.sin(ang)
    reps = LANES // half
    return jnp.tile(cos, (1, reps)), jnp.tile(jnp.concatenate([-sin, sin], axis=-1), (1, reps // 2))


def _attn_prepass(p3, w_a):
    b, s, _ = p3.shape
    h = w_a // LANES
    ts = _tile(s // 2, ATTN_KEY_CHUNK, LANES)
    cos, sin = _rope_tables(s)
    head_out = jax.ShapeDtypeStruct((b, h, s, LANES), BF16)
    vt_out = jax.ShapeDtypeStruct((b, h, s // ts, VT_ROWS, ts), BF16)
    hp = _tile(h, PREPASS_HEADS, 1)
    hg = h // hp
    col = lambda off: pl.BlockSpec((1, ts, hp * LANES), lambda bi, si, hi: (bi, si, off + hi))
    tab = pl.BlockSpec((ts, LANES), lambda bi, si, hi: (si, 0))
    out = pl.BlockSpec((1, hp, ts, LANES), lambda bi, si, hi: (bi, hi, si, 0))
    out_t = pl.BlockSpec((1, hp, 1, VT_ROWS, ts), lambda bi, si, hi: (bi, hi, si, 0, 0))
    return pl.pallas_call(
        _rope_kernel,
        grid=(b, s // ts, hg),
        in_specs=[col(0), col(hg), col(2 * hg), tab, tab],
        out_specs=[out, out, out_t],
        out_shape=[head_out, head_out, vt_out],
        compiler_params=_params(("parallel", "parallel", "arbitrary"), VMEM_SMALL),
        name="attn_prepass",
    )(p3, p3, p3, cos, sin)


def _attn_kernel(q_ref, k_ref, vt_ref, z_ref, lq1_ref, lk1_ref, lq2_ref, lk2_ref, g_ref, o_ref,
                 qq_scr, acc_scr, sa_scr, sb_scr, *, tq, qt, lam_init):
    nq = q_ref.shape[2] // tq
    nk = vt_ref.shape[2]
    tk = vt_ref.shape[4]
    tiles = [slice(c * qt, (c + 1) * qt) for c in range(2 * tq // qt)]
    lane = lax.broadcasted_iota(jnp.int32, (tq, LANES), 1)
    lam = (jnp.exp(jnp.sum(lq1_ref[...] * lk1_ref[...], keepdims=True))
           - jnp.exp(jnp.sum(lq2_ref[...] * lk2_ref[...], keepdims=True)) + lam_init)

    def load_queries(qi):
        rows = pl.ds(pl.multiple_of(qi * tq, tq), tq)
        q = q_ref[0, 0, rows, :].astype(F32)
        qq_scr[0:tq, :] = jnp.where(lane < DH_A, q, 0.0).astype(BF16)
        qq_scr[tq:2 * tq, :] = jnp.where(lane >= DH_A, q, 0.0).astype(BF16)

    def produce(i, s_scr):
        off = pl.multiple_of(i * tk, tk)
        ks = k_ref[0, 0, pl.ds(off, tk), :]
        s = [_dot_nt(ks, qq_scr[t, :]) for t in tiles]
        for t, x in zip(tiles, s):
            s_scr[:, t] = x
        return jnp.concatenate([jnp.max(x, axis=0, keepdims=True) for x in s], axis=1)

    def consume(i, s_scr, cmax, m_old):
        vt = vt_ref[0, 0, i]
        m_new = jnp.maximum(m_old, cmax)
        alpha = jnp.exp2(m_old - m_new)
        p = [jnp.exp2(s_scr[:, t] - m_new[:, t]).astype(BF16) for t in tiles]
        pv = [_dot(vt, x) for x in p]
        for t, x in zip(tiles, pv):
            acc_scr[:, t] = alpha[:, t] * acc_scr[:, t] + x
        return m_new

    def pair(j, carry):
        m, cmax_a = carry
        cmax_b = produce(2 * j + 1, sb_scr)
        m = consume(2 * j, sa_scr, cmax_a, m)
        cmax_a = produce(2 * j + 2, sa_scr)
        m = consume(2 * j + 1, sb_scr, cmax_b, m)
        return m, cmax_a

    m0 = jnp.full((1, 2 * tq), -jnp.inf, F32)
    trips = nk // 2 - 1
    unroll = next(u for u in (5, 3, 2, 1) if trips % u == 0 and (trips // u >= 2 or u == 1))

    def query_tile(qi, cmax_a):
        m, cmax_a = lax.fori_loop(0, trips, pair, (m0, cmax_a), unroll=unroll)
        cmax_b = produce(nk - 1, sb_scr)
        m = consume(nk - 2, sa_scr, cmax_a, m)
        load_queries(jnp.minimum(qi + 1, nq - 1))
        cmax_next = produce(0, sa_scr)
        consume(nk - 1, sb_scr, cmax_b, m)

        rows = pl.ds(pl.multiple_of(qi * tq, tq), tq)
        ot = acc_scr[0:LANES, :] * (1.0 / acc_scr[LANES:LANES + 1, :])
        acc_scr[...] = jnp.zeros(acc_scr.shape, F32)
        o = (ot[:, 0:tq] - lam * ot[:, tq:2 * tq]).T
        y = o * lax.rsqrt(jnp.mean(o * o, axis=-1, keepdims=True) + SUBLN_EPS) * g_ref[...] * (1.0 - lam_init)
        z = z_ref[0, rows, :].astype(F32)
        o_ref[0, rows, :] = (y * (z * _sigmoid(z))).astype(o_ref.dtype)
        return cmax_next

    load_queries(0)
    acc_scr[...] = jnp.zeros(acc_scr.shape, F32)
    lax.fori_loop(0, nq, query_tile, produce(0, sa_scr))


def _diff_attention(qr, kr, vt, p3, z_off, lam_q1, lam_k1, lam_q2, lam_k2, subln_g, lam_init):
    b, h, s, _ = qr.shape
    nk, tk = vt.shape[2], vt.shape[4]
    tq = _tile(s, TOKEN_ROWS, LANES)
    qt = _tile(2 * tq, CHANNEL_COLS, LANES)
    assert z_off % LANES == 0 and nk % 2 == 0
    zc = z_off // LANES
    vec = lambda n: pl.BlockSpec((1, n), lambda bi, hi: (0, 0))
    head = pl.BlockSpec((1, 1, s, LANES), lambda bi, hi: (bi, hi, 0, 0))
    kern = functools.partial(_attn_kernel, tq=tq, qt=qt, lam_init=lam_init)
    return pl.pallas_call(
        kern,
        grid=(b, h),
        in_specs=[
            head, head,
            pl.BlockSpec((1, 1, nk, VT_ROWS, tk), lambda bi, hi: (bi, hi, 0, 0, 0)),
            pl.BlockSpec((1, s, LANES), lambda bi, hi: (bi, 0, zc + hi)),
            vec(DH_A), vec(DH_A), vec(DH_A), vec(DH_A), vec(LANES),
        ],
        out_specs=pl.BlockSpec((1, s, LANES), lambda bi, hi: (bi, 0, hi)),
        out_shape=jax.ShapeDtypeStruct((b, s, h * LANES), BF16),
        scratch_shapes=[
            pltpu.VMEM((2 * tq, LANES), BF16),
            pltpu.VMEM((VT_ROWS, 2 * tq), F32),
            pltpu.VMEM((tk, 2 * tq), F32),
            pltpu.VMEM((tk, 2 * tq), F32),
        ],
        compiler_params=_params(("parallel", "arbitrary"), VMEM_BIG),
        name="diff_attn",
    )(qr, kr, vt, p3, lam_q1.reshape(1, DH_A), lam_k1.reshape(1, DH_A), lam_q2.reshape(1, DH_A),
      lam_k2.reshape(1, DH_A), subln_g.reshape(1, LANES))


def _group_sum(x, gmat):
    hi = x.astype(BF16)
    lo = (x - hi.astype(F32)).astype(BF16)
    return _dot(hi, gmat) + _dot(lo, gmat)


def _group_matrix(cw):
    r = lax.broadcasted_iota(jnp.int32, (cw, cw), 0) // N_B
    c = lax.broadcasted_iota(jnp.int32, (cw, cw), 1) // N_B
    return jnp.where(r == c, 1.0, 0.0).astype(BF16)


def _rwkv_prep_kernel(r_ref, k_ref, v_ref, lo_ref, rp_ref, kp_ref, vp_ref, lop_ref, rn_ref, kn_ref, vn_ref,
                      lon_ref, mu_ref, mulo_ref, w0_ref, a0_ref, wl_ref, al_ref, kk_ref, ka_ref, rk_ref,
                      ro_ref, vo_ref, ao_ref, lw_ref, kd_ref, bd_ref, bonus_ref, pw_scr, pa_scr):
    ts = r_ref.shape[1]
    si = pl.program_id(1)
    first = si == 0
    last = si == pl.num_programs(1) - 1

    def shift(cur_ref, prev_ref, next_ref, mu_p, mu_n):
        x = cur_ref[0].astype(F32)
        hp = prev_ref[0].astype(F32)
        hn = next_ref[0].astype(F32)
        pr = jnp.where(first, 0.0, hp[hp.shape[0] - 1:hp.shape[0], :])
        nx = jnp.where(last, 0.0, hn[0:1, :])
        row = lax.broadcasted_iota(jnp.int32, x.shape, 0)
        prev = jnp.where(row == 0, pr, pltpu.roll(x, 1, axis=0))
        nxt = jnp.where(row == ts - 1, nx, pltpu.roll(x, ts - 1, axis=0))
        return x + mu_p * (prev - x) + mu_n * (nxt - x)

    r = shift(r_ref, rp_ref, rn_ref, mu_ref[0, 0:1, :], mu_ref[1, 0:1, :])
    k = shift(k_ref, kp_ref, kn_ref, mu_ref[0, 1:2, :], mu_ref[1, 1:2, :])
    v = shift(v_ref, vp_ref, vn_ref, mu_ref[0, 2:3, :], mu_ref[1, 2:3, :])

    @pl.when(pl.program_id(2) == 0)
    def _():
        lo = shift(lo_ref, lop_ref, lon_ref, mulo_ref[0:1, :], mulo_ref[1:2, :])
        rl = lo.shape[1] // 4
        for d in range(2):
            pw_scr[d] = jnp.tanh(lo[:, d * rl:(d + 1) * rl]).astype(BF16)
            pa_scr[d] = lo[:, (2 + d) * rl:(3 + d) * rl].astype(BF16)

    cw = r.shape[1]
    gmat = _group_matrix(cw)
    kkh = k * kk_ref[...]
    nrm = jnp.sqrt(_group_sum(kkh * kkh, gmat))
    kk = kkh / jnp.maximum(nrm, 1e-12)
    ka = ka_ref[...]
    ro_ref[0] = r.astype(ro_ref.dtype)
    vo_ref[0] = v.astype(vo_ref.dtype)
    ao_ref[0] = (-kk).astype(ao_ref.dtype)
    ksum = jnp.zeros_like(k)
    for d in range(2):
        pw = pw_scr[d]
        pa = pa_scr[d]
        wl = w0_ref[d:d + 1, :] + _dot(pw, wl_ref[d])
        lw_ref[d, 0] = -math.exp(-0.5) * _sigmoid(wl)
        a = _sigmoid(a0_ref[d:d + 1, :] + _dot(pa, al_ref[d]))
        kdir = k * (1.0 + (a - 1.0) * ka)
        kd_ref[d, 0] = kdir.astype(kd_ref.dtype)
        bd_ref[d, 0] = (kk * a).astype(bd_ref.dtype)
        ksum = ksum + kdir
    bonus_ref[0] = _group_sum(r * ksum * rk_ref[...], gmat) * v


def _rwkv_prep(p3, s_off, w_b, r_lora, mu_prev, mu_next, w0, w_lora, a0, a_lora, k_k, k_a, r_k):
    b, s, _ = p3.shape
    ts = _tile(s, RWKV_ROWS, BF16_ROWS)
    cw = _tile(w_b, CHANNEL_COLS, LANES)
    lw = 4 * r_lora
    halo = BF16_ROWS
    nblk = s // halo
    assert s_off % cw == 0 and w_b % cw == 0 and (s_off + 3 * w_b) % lw == 0 and ts % halo == 0
    rc, kc, vc, lc = s_off // cw, (s_off + w_b) // cw, (s_off + 2 * w_b) // cw, (s_off + 3 * w_b) // lw
    per = ts // halo

    cur = lambda off: pl.BlockSpec((1, ts, cw), lambda bi, si, ji: (bi, si, off + ji))
    prv = lambda off: pl.BlockSpec((1, halo, cw), lambda bi, si, ji: (bi, jnp.maximum(si * per - 1, 0), off + ji))
    nxt = lambda off: pl.BlockSpec((1, halo, cw),
                                   lambda bi, si, ji: (bi, jnp.minimum((si + 1) * per, nblk - 1), off + ji))
    lo_cur = pl.BlockSpec((1, ts, lw), lambda bi, si, ji: (bi, si, lc))
    lo_prv = pl.BlockSpec((1, halo, lw), lambda bi, si, ji: (bi, jnp.maximum(si * per - 1, 0), lc))
    lo_nxt = pl.BlockSpec((1, halo, lw), lambda bi, si, ji: (bi, jnp.minimum((si + 1) * per, nblk - 1), lc))
    colvec = lambda rows: pl.BlockSpec((rows, cw), lambda bi, si, ji: (0, ji))

    mu_rkv = jnp.stack([mu_prev[:3 * w_b].reshape(3, w_b), mu_next[:3 * w_b].reshape(3, w_b)])
    mu_lo = jnp.stack([mu_prev[3 * w_b:], mu_next[3 * w_b:]])
    tok = pl.BlockSpec((1, ts, cw), lambda bi, si, ji: (bi, si, ji))
    tok2 = pl.BlockSpec((2, 1, ts, cw), lambda bi, si, ji: (0, bi, si, ji))
    shp = lambda dt: jax.ShapeDtypeStruct((b, s, w_b), dt)
    shp2 = lambda dt: jax.ShapeDtypeStruct((2, b, s, w_b), dt)
    return pl.pallas_call(
        _rwkv_prep_kernel,
        grid=(b, s // ts, w_b // cw),
        in_specs=[cur(rc), cur(kc), cur(vc), lo_cur, prv(rc), prv(kc), prv(vc), lo_prv,
                  nxt(rc), nxt(kc), nxt(vc), lo_nxt,
                  pl.BlockSpec((2, 3, cw), lambda bi, si, ji: (0, 0, ji)),
                  pl.BlockSpec((2, lw), lambda bi, si, ji: (0, 0)),
                  colvec(2), colvec(2),
                  pl.BlockSpec((2, r_lora, cw), lambda bi, si, ji: (0, 0, ji)),
                  pl.BlockSpec((2, r_lora, cw), lambda bi, si, ji: (0, 0, ji)),
                  colvec(1), colvec(1), colvec(1)],
        out_specs=[tok, tok, tok, tok2, tok2, tok2, tok],
        out_shape=[shp(BF16), shp(BF16), shp(BF16), shp2(F32), shp2(BF16), shp2(BF16), shp(F32)],
        scratch_shapes=[pltpu.VMEM((2, ts, r_lora), BF16), pltpu.VMEM((2, ts, r_lora), BF16)],
        compiler_params=_params(("parallel", "parallel", "arbitrary"), VMEM_BIG),
        name="rwkv_prep",
    )(p3, p3, p3, p3, p3, p3, p3, p3, p3, p3, p3, p3, mu_rkv, mu_lo, w0, a0,
      w_lora.astype(BF16), a_lora.astype(BF16), k_k.reshape(1, w_b), k_a.reshape(1, w_b), r_k.reshape(1, w_b))


def _wkv_kernel(mask_ref, r_ref, a_ref, v_ref, lw_ref, k_ref, b_ref, y_ref, z_scr, *, npairs, reverse):
    @pl.when(pl.program_id(2) == 0)
    def _():
        z_scr[...] = jnp.zeros(z_scr.shape, F32)

    z = [z_scr[j] for j in range(npairs)]
    nsub = r_ref.shape[1] // CHUNK
    for sub in (reversed(range(nsub)) if reverse else range(nsub)):
        rows = slice(sub * CHUNK, (sub + 1) * CHUNK)
        z = _wkv_chunk(mask_ref, r_ref.at[:, rows, :], a_ref.at[:, rows, :], v_ref.at[:, rows, :],
                       lw_ref.at[:, :, rows, :], k_ref.at[:, :, rows, :], b_ref.at[:, :, rows, :],
                       y_ref.at[:, rows, :], z, npairs=npairs, reverse=reverse)
    for j in range(npairs):
        z_scr[j] = z[j]


def _wkv_chunk(mask_ref, r_ref, a_ref, v_ref, lw_ref, k_ref, b_ref, y_ref, z_in, *, npairs, reverse):
    assert CHUNK == N_B and 2 * N_B == LANES
    c = CHUNK
    strict = mask_ref[0]
    incl = mask_ref[1]
    cum = incl[0:c, 0:c].astype(BF16)
    lane = lax.broadcasted_iota(jnp.int32, (c, LANES), 1)
    low = lane < N_B
    row2 = lax.broadcasted_iota(jnp.int32, (LANES, LANES), 0)
    col2 = lax.broadcasted_iota(jnp.int32, (LANES, LANES), 1)
    own = (row2 < c) == (col2 < N_B)

    def stack(x):
        return jnp.concatenate([jnp.where(low, x, 0.0), jnp.where(low, 0.0, x)], axis=0)

    def each(f, *cols):
        return [f(*xs) for xs in zip(*cols)]

    sls = [slice(j * LANES, (j + 1) * LANES) for j in range(npairs)]
    lw = [lw_ref[0, 0, :, sl] for sl in sls]
    r = [r_ref[0, :, sl].astype(F32) for sl in sls]
    a = [a_ref[0, :, sl].astype(F32) for sl in sls]
    v = [v_ref[0, :, sl].astype(F32) for sl in sls]
    k = [k_ref[0, 0, :, sl].astype(F32) for sl in sls]
    b = [b_ref[0, 0, :, sl].astype(F32) for sl in sls]
    zb = [stack(z).astype(BF16) for z in z_in]
    strict_p = strict[0:c, :] + strict[c:, :]
    incl_p = incl[0:c, :] + incl[c:, :]
    eye_p = lax.broadcasted_iota(jnp.int32, (c, LANES), 0) == (lane % N_B)

    def fold(x):
        return x[0:c, :] + x[c:, :]

    hi = each(lambda x: x.astype(BF16), lw)
    lo = each(lambda x, h: (x - h.astype(F32)).astype(BF16), lw, hi)
    cum2 = jnp.concatenate([cum, cum], axis=1)
    lam = each(lambda h, l: _dot(cum2, jnp.concatenate([h, l], axis=0)), hi, lo)
    tot = each(lambda x: jnp.sum(x, axis=0, keepdims=True), lw)
    gdec = each(jnp.exp, tot)
    a_p = each(lambda x, lm, w: x * jnp.exp(lm - w), a, lam, lw)
    r_p = each(lambda x, lm: x * jnp.exp(lm), r, lam)
    a_s = each(stack, a_p)
    e_neg = each(lambda lm: jnp.exp(-lm), lam)
    e_rem = each(lambda t, lm: jnp.exp(t - lm), tot, lam)
    b_s = each(lambda x, e: stack(x * e).astype(BF16), b, e_neg)
    k_s = each(lambda x, e: stack(x * e).astype(BF16), k, e_neg)
    bh_s = each(lambda x, e: stack(x * e).astype(BF16), b, e_rem)
    kh_s = each(lambda x, e: stack(x * e).astype(BF16), k, e_rem)
    v_s = each(lambda x: stack(x).astype(BF16), v)

    m4 = each(lambda x, y, p, q: _dot_nt(jnp.concatenate([x, y], axis=0).astype(BF16),
                                         jnp.concatenate([p, q], axis=0)), a_p, r_p, b_s, k_s)
    lp = each(lambda m: stack(m[0:c, 0:LANES] * strict_p).astype(BF16), m4)
    lak = each(lambda m: stack(m[0:c, LANES:] * strict_p).astype(BF16), m4)
    mrbk = each(lambda m: jnp.concatenate([stack(m[c:, 0:LANES] * incl_p), stack(m[c:, LANES:] * incl_p)],
                                          axis=1).astype(BF16), m4)

    half = LANES // 2
    x = each(lambda p, q, w: p + pltpu.roll(_dot(q, w), half, axis=1), a_s, lak, v_s)

    def live(arr, p):
        lo0 = 0 if reverse else p
        return jnp.concatenate([arr[lo0:lo0 + c - p], arr[c + lo0:c + lo0 + c - p]], axis=0)

    def add_live(full, part, p):
        n, lo0 = c - p, (0 if reverse else p)
        pieces = []
        for blk in range(2):
            base = blk * c
            pieces += [full[base:base + lo0], full[base + lo0:base + lo0 + n] + part[blk * n:(blk + 1) * n],
                       full[base + lo0 + n:base + c]]
        return jnp.concatenate([q for q in pieces if q.shape[0]], axis=0)

    steps = int(math.log2(c))
    for it in range(steps - 1):
        p = 2 ** it
        if p < BF16_ROWS:
            res = each(lambda l, q: _dot(l, jnp.concatenate([q.astype(BF16), l], axis=1)), lp, x)
            x = each(lambda q, s: q + s[:, 0:LANES], x, res)
            lp = each(lambda s: s[:, LANES:].astype(BF16), res)
        else:
            res = each(lambda l, q: _dot(live(l, p), jnp.concatenate([q.astype(BF16), l], axis=1)), lp, x)
            x = each(lambda q, s: add_live(q, s[:, 0:LANES], p), x, res)
            lp = each(lambda s: add_live(jnp.zeros((2 * c, LANES), F32), s[:, LANES:], p).astype(BF16), res)
    p = 2 ** (steps - 1)
    x = each(lambda l, q: add_live(q, _dot(live(l, p), q.astype(BF16)), p), lp, x)
    w_s = each(lambda q: jnp.where(own, q, 0.0).astype(BF16), x)
    u_s = each(lambda q: jnp.where(own, pltpu.roll(q, half, axis=1), 0.0).astype(BF16), x)

    zero = jnp.zeros((LANES, LANES), BF16)
    wuv = each(lambda w, u, vv: jnp.concatenate([jnp.concatenate([w, u], axis=1),
                                                 jnp.concatenate([zero, vv], axis=1)], axis=0), w_s, u_s, v_s)
    ry = each(_dot, mrbk, wuv)
    pq = each(lambda p, q, m: _dot_tn(jnp.concatenate([p, q], axis=0), m), bh_s, kh_s, wuv)
    rw = each(lambda p, s: (p + fold(s[:, 0:LANES])).astype(BF16), r_p, ry)
    pt = each(lambda g, s: (jnp.where(eye_p, g, 0.0) + fold(s[:, 0:LANES])).astype(BF16), gdec, pq)

    ys = each(lambda p, z, s: _dot(p, z) + fold(s[:, LANES:]), rw, zb, ry)
    z_new = each(lambda p, z, s: _dot(p, z) + fold(s[:, LANES:]), pt, zb, pq)
    for j, sl in enumerate(sls):
        y_ref[0, :, sl] = ys[j].astype(y_ref.dtype)
    return z_new


def _wkv_masks(reverse):
    i = np.arange(LANES)
    same = (i[:, None] // CHUNK) == (i[None, :] // CHUNK)
    t, s = i[:, None] % CHUNK, i[None, :] % CHUNK
    before = (s > t) if reverse else (s < t)
    return jnp.asarray(np.stack([same & before, same & (before | (s == t))]).astype(np.float32))


def _wkv_scan(r, a, v, logw, kdir, bdir, reverse):
    b, s, w_b = r.shape
    c = _tile(s, WKV_STEP_CHUNKS * CHUNK, CHUNK)
    nc = s // c
    d = int(reverse)
    gw = _tile(w_b, WKV_GROUP_LANES, LANES)
    npairs = gw // LANES
    chunk_of = (lambda ci: nc - 1 - ci) if reverse else (lambda ci: ci)
    shared = pl.BlockSpec((1, c, gw), lambda bi, gi, ci: (bi, chunk_of(ci), gi))
    perdir = pl.BlockSpec((1, 1, c, gw), lambda bi, gi, ci: (d, bi, chunk_of(ci), gi))
    kern = functools.partial(_wkv_kernel, npairs=npairs, reverse=reverse)
    return pl.pallas_call(
        kern,
        grid=(b, w_b // gw, nc),
        in_specs=[pl.BlockSpec((2, LANES, LANES), lambda bi, gi, ci: (0, 0, 0)),
                  shared, shared, shared, perdir, perdir, perdir],
        out_specs=shared,
        out_shape=jax.ShapeDtypeStruct((b, s, w_b), BF16),
        scratch_shapes=[pltpu.VMEM((npairs, N_B, LANES), F32)],
        compiler_params=_params(("parallel", "parallel", "arbitrary"), VMEM_SMALL),
        name="wkv_scan_rev" if reverse else "wkv_scan_fwd",
    )(_wkv_masks(reverse), r, a, v, logw, kdir, bdir)


def _rwkv_out_kernel(yf_ref, yb_ref, bonus_ref, z_ref, g_ref, b_ref, o_ref):
    y = yf_ref[0].astype(F32) + yb_ref[0].astype(F32)
    gmat = _group_matrix(y.shape[1])
    mu = _group_sum(y, gmat) * (1.0 / N_B)
    yc = y - mu
    var = _group_sum(yc * yc, gmat) * (1.0 / N_B)
    yn = yc * lax.rsqrt(var + LNX_EPS) * g_ref[...] + b_ref[...]
    z = z_ref[0].astype(F32)
    o_ref[0] = ((yn + bonus_ref[0]) * (z * _sigmoid(z))).astype(o_ref.dtype)


def _rwkv_out(y_fwd, y_rev, bonus, p3, z_off, lnx_g, lnx_b):
    b, s, w_b = y_fwd.shape
    ts = _tile(s, RWKV_OUT_ROWS, BF16_ROWS)
    cw = _tile(w_b, CHANNEL_COLS, LANES)
    assert z_off % cw == 0
    zc = z_off // cw
    vec = pl.BlockSpec((1, cw), lambda bi, si, ji: (0, ji))
    tok = pl.BlockSpec((1, ts, cw), lambda bi, si, ji: (bi, si, ji))
    return pl.pallas_call(
        _rwkv_out_kernel,
        grid=(b, s // ts, w_b // cw),
        in_specs=[tok, tok, tok,
                  pl.BlockSpec((1, ts, cw), lambda bi, si, ji: (bi, si, zc + ji)),
                  vec, vec],
        out_specs=pl.BlockSpec((1, ts, cw), lambda bi, si, ji: (bi, si, ji)),
        out_shape=jax.ShapeDtypeStruct((b, s, w_b), BF16),
        compiler_params=_params(("parallel", "parallel", "parallel"), VMEM_SMALL),
        name="rwkv_out",
    )(y_fwd, y_rev, bonus, p3, lnx_g.reshape(1, w_b), lnx_b.reshape(1, w_b))


def _merge_kernel(ua_ref, ub_ref, wa_ref, wb_ref, ga_ref, gb_ref, o_ref):
    oa = _dot(ua_ref[...], wa_ref[...])
    ob = _dot(ub_ref[...], wb_ref[...])
    m = _sigmoid(ga_ref[...].astype(F32)) * oa + _sigmoid(gb_ref[...].astype(F32)) * ob
    o_ref[...] = m.astype(o_ref.dtype)


def _merge(ua, ub, w_oa, w_ob, p2, g_off):
    t, w_a = ua.shape
    w_b = ub.shape[1]
    d = w_oa.shape[1]
    tm = _tile(t, PROJ_ROWS, BF16_ROWS)
    tn = _tile(d, PROJ_COLS, MXU_COLS)
    assert g_off % tn == 0 and d % tn == 0
    gc = g_off // tn
    nd = d // tn
    return pl.pallas_call(
        _merge_kernel,
        grid=(t // tm, nd),
        in_specs=[pl.BlockSpec((tm, w_a), lambda i, j: (i, 0)),
                  pl.BlockSpec((tm, w_b), lambda i, j: (i, 0)),
                  pl.BlockSpec((w_a, tn), lambda i, j: (0, j)),
                  pl.BlockSpec((w_b, tn), lambda i, j: (0, j)),
                  pl.BlockSpec((tm, tn), lambda i, j: (i, gc + j)),
                  pl.BlockSpec((tm, tn), lambda i, j: (i, gc + nd + j))],
        out_specs=pl.BlockSpec((tm, tn), lambda i, j: (i, j)),
        out_shape=jax.ShapeDtypeStruct((t, d), BF16),
        compiler_params=_params(("parallel", "arbitrary"), VMEM_MID),
        name="gate_merge",
    )(ua, ub, w_oa, w_ob, p2, p2)


def _out_kernel(m_ref, w_ref, x_ref, o_ref):
    o_ref[...] = x_ref[...] + _dot(m_ref[...], w_ref[...])


def _out_norm_kernel(m_ref, w_ref, x_ref, g_ref, o_ref, res_scr, ss_scr):
    j = pl.program_id(1)
    nj, _, tn = res_scr.shape
    res = x_ref[...] + _dot(m_ref[...], w_ref[...])
    res_scr[j] = res

    @pl.when(j == 0)
    def _():
        ss_scr[...] = jnp.zeros(ss_scr.shape, F32)

    ss_scr[...] += jnp.sum(res * res, axis=-1, keepdims=True)

    @pl.when(j == nj - 1)
    def _():
        scale = lax.rsqrt(ss_scr[...] * (1.0 / (nj * tn)) + NORM_EPS)
        for jj in range(nj):
            cols = slice(jj * tn, (jj + 1) * tn)
            o_ref[:, cols] = res_scr[jj] * scale * g_ref[:, cols]


def _out_proj(m, w_out, x2, final_g, final_norm):
    t, d = x2.shape
    tm = _tile(t, TOKEN_ROWS, BF16_ROWS)
    tn = _tile(d, PROJ_COLS, MXU_COLS)
    in_specs = [pl.BlockSpec((tm, d), lambda i, j: (i, 0)),
                pl.BlockSpec((d, tn), lambda i, j: (0, j)),
                pl.BlockSpec((tm, tn), lambda i, j: (i, j))]
    if not final_norm:
        return pl.pallas_call(
            _out_kernel,
            grid=(t // tm, d // tn),
            in_specs=in_specs,
            out_specs=pl.BlockSpec((tm, tn), lambda i, j: (i, j)),
            out_shape=jax.ShapeDtypeStruct((t, d), F32),
            compiler_params=_params(("parallel", "arbitrary"), VMEM_BIG),
            name="out_proj",
        )(m, w_out, x2)
    return pl.pallas_call(
        _out_norm_kernel,
        grid=(t // tm, d // tn),
        in_specs=in_specs + [pl.BlockSpec((1, d), lambda i, j: (0, 0))],
        out_specs=pl.BlockSpec((tm, d), lambda i, j: (i, 0)),
        out_shape=jax.ShapeDtypeStruct((t, d), F32),
        scratch_shapes=[pltpu.VMEM((d // tn, tm, tn), F32), pltpu.VMEM((tm, 1), F32)],
        compiler_params=_params(("parallel", "arbitrary"), VMEM_BIG),
        name="out_proj_norm",
    )(m, w_out, x2, final_g.reshape(1, d))


def _lambda_init(layer_idx):
    return 0.8 - 0.6 * math.exp(-0.3 * layer_idx)


def _mixer_layer(x, l, prm, final_g, final_norm):
    b, s, d = x.shape
    w_a = prm["w_oA"][l].shape[0]
    w_b = prm["w_oB"][l].shape[0]
    r_lora = prm["w_lora"].shape[2]
    c_shift = 3 * w_b + 4 * r_lora
    n_in = prm["w_in"].shape[2]
    assert n_in == 3 * w_a + c_shift + w_a + w_b + 2 * d
    assert w_a % LANES == 0 and w_b % LANES == 0 and s % CHUNK == 0
    s_off = 3 * w_a
    za_off = s_off + c_shift
    zb_off = za_off + w_a
    g_off = zb_off + w_b
    lam_init = _lambda_init(l)

    x2 = x.reshape(b * s, d)
    h = _rmsnorm_bf16(x2, prm["norm_g"][l])
    p2 = _matmul_bf16(h, prm["w_in"][l])
    p3 = p2.reshape(b, s, n_in)

    qr, kr, vt = _attn_prepass(p3, w_a)
    ua = _diff_attention(qr, kr, vt, p3, za_off, prm["lam_q1"][l], prm["lam_k1"][l], prm["lam_q2"][l],
                         prm["lam_k2"][l], prm["subln_g"][l], lam_init)

    r, v, a, logw, kdir, bdir, bonus = _rwkv_prep(
        p3, s_off, w_b, r_lora, prm["mu_prev"][l], prm["mu_next"][l], prm["w0"][l], prm["w_lora"][l],
        prm["a0"][l], prm["a_lora"][l], prm["k_k"][l], prm["k_a"][l], prm["r_k"][l])
    y_fwd = _wkv_scan(r, a, v, logw, kdir, bdir, reverse=False)
    y_rev = _wkv_scan(r, a, v, logw, kdir, bdir, reverse=True)
    ub = _rwkv_out(y_fwd, y_rev, bonus, p3, zb_off, prm["lnx_g"][l], prm["lnx_b"][l])

    m = _merge(ua.reshape(b * s, w_a), ub.reshape(b * s, w_b), prm["w_oA"][l].astype(BF16),
               prm["w_oB"][l].astype(BF16), p2, g_off)
    out = _out_proj(m, prm["w_out"][l].astype(BF16), x2, final_g, final_norm)
    return out.reshape(b, s, d)


def kernel(x_prompt, x_sample, norm_g, w_in, mu_prev, mu_next, lam_q1, lam_k1, lam_q2, lam_k2, subln_g, w0,
           w_lora, a0, a_lora, k_k, k_a, r_k, lnx_g, lnx_b, w_oA, w_oB, w_out, final_g):
    prm = dict(norm_g=norm_g, w_in=w_in, mu_prev=mu_prev, mu_next=mu_next, lam_q1=lam_q1, lam_k1=lam_k1,
               lam_q2=lam_q2, lam_k2=lam_k2, subln_g=subln_g, w0=w0, w_lora=w_lora, a0=a0, a_lora=a_lora,
               k_k=k_k, k_a=k_a, r_k=r_k, lnx_g=lnx_g, lnx_b=lnx_b, w_oA=w_oA, w_oB=w_oB, w_out=w_out)
    depth = norm_g.shape[0]

    def trunk(x):
        for l in range(depth):
            x = _mixer_layer(x, l, prm, final_g, final_norm=(l == depth - 1))
        return x

    return (trunk(x_prompt), trunk(x_sample))
```

```python
import functools
import math

import jax
import jax.numpy as jnp
import numpy as np
from jax import lax
from jax.experimental import pallas as pl
from jax.experimental.pallas import tpu as pltpu

F32 = jnp.float32
BF16 = jnp.bfloat16

LANES = 128
BF16_ROWS = 16
MXU_COLS = 256
PROJ_ROWS = 1024
PROJ_COLS = 512
TOKEN_ROWS = 512
RWKV_ROWS = 2048
RWKV_OUT_ROWS = 2048
NORM_ROWS = 256
CHANNEL_COLS = 256
VMEM_SMALL, VMEM_MID, VMEM_BIG = 32, 48, 56
DH_A = 64
N_B = 64
CHUNK = 64
ROPE_THETA = 10000.0
ATTN_SCALE = DH_A ** -0.5
LOG2_E = math.log2(math.e)
ATTN_KEY_CHUNK = 512
PREPASS_HEADS = 8
WKV_GROUP_LANES = 2048
WKV_STEP_CHUNKS = 2
VT_ROWS = LANES + BF16_ROWS
NORM_EPS = 1e-6
SUBLN_EPS = 1e-5
LNX_EPS = 64e-5
MIB = 2 ** 20


def _tile(n, target, align):
    if n <= target:
        return n
    t = (target // align) * align
    while t >= align:
        if n % t == 0:
            return t
        t -= align
    raise ValueError(f"no tile for {n} (target {target}, align {align})")


def _params(semantics, vmem_mib):
    return pltpu.CompilerParams(dimension_semantics=semantics, vmem_limit_bytes=vmem_mib * MIB)


def _sigmoid(x):
    return 1.0 / (1.0 + jnp.exp(-x))


def _dot(a, b):
    return jnp.dot(a, b, preferred_element_type=F32)


def _dot_nt(a, b):
    return lax.dot_general(a, b, (((1,), (1,)), ((), ())), preferred_element_type=F32)


def _dot_tn(a, b):
    return lax.dot_general(a, b, (((0,), (0,)), ((), ())), preferred_element_type=F32)


def _rmsnorm_kernel(x_ref, g_ref, o_ref):
    x = x_ref[...]
    ms = jnp.mean(x * x, axis=-1, keepdims=True)
    o_ref[...] = (x * lax.rsqrt(ms + NORM_EPS) * g_ref[...]).astype(o_ref.dtype)


def _rmsnorm_bf16(x2, g):
    t, d = x2.shape
    tm = _tile(t, NORM_ROWS, BF16_ROWS)
    return pl.pallas_call(
        _rmsnorm_kernel,
        grid=(t // tm,),
        in_specs=[pl.BlockSpec((tm, d), lambda i: (i, 0)), pl.BlockSpec((1, d), lambda i: (0, 0))],
        out_specs=pl.BlockSpec((tm, d), lambda i: (i, 0)),
        out_shape=jax.ShapeDtypeStruct((t, d), BF16),
        compiler_params=_params(("parallel",), VMEM_SMALL),
        name="rmsnorm_cast",
    )(x2, g.reshape(1, d))


def _matmul_kernel(a_ref, w_ref, o_ref):
    o_ref[...] = _dot(a_ref[...], w_ref[...].astype(BF16)).astype(o_ref.dtype)


def _matmul_bf16(a, w):
    m, k = a.shape
    n = w.shape[1]
    tm = _tile(m, PROJ_ROWS, BF16_ROWS)
    tn = _tile(n, PROJ_COLS, MXU_COLS)
    return pl.pallas_call(
        _matmul_kernel,
        grid=(m // tm, n // tn),
        in_specs=[pl.BlockSpec((tm, k), lambda i, j: (i, 0)), pl.BlockSpec((k, tn), lambda i, j: (0, j))],
        out_specs=pl.BlockSpec((tm, tn), lambda i, j: (i, j)),
        out_shape=jax.ShapeDtypeStruct((m, n), BF16),
        compiler_params=_params(("parallel", "arbitrary"), VMEM_BIG),
        name="in_proj",
    )(a, w)


def _rope_kernel(q_ref, k_ref, v_ref, cos_ref, sin_ref, qo_ref, ko_ref, vo_ref):
    cos = cos_ref[...]
    sin = sin_ref[...]
    lane = lax.broadcasted_iota(jnp.int32, cos.shape, 1)
    first_half = (lane % DH_A) < (DH_A // 2)

    def rope(x):
        partner = jnp.where(first_half,
                            pltpu.roll(x, LANES - DH_A // 2, axis=1),
                            pltpu.roll(x, DH_A // 2, axis=1))
        return x * cos + partner * sin

    for hh in range(qo_ref.shape[1]):
        cols = slice(hh * LANES, (hh + 1) * LANES)
        qo_ref[0, hh] = (rope(q_ref[0, :, cols].astype(F32)) * (ATTN_SCALE * LOG2_E)).astype(qo_ref.dtype)
        ko_ref[0, hh] = rope(k_ref[0, :, cols].astype(F32)).astype(ko_ref.dtype)
        vo_ref[0, hh, 0, 0:LANES, :] = v_ref[0, :, cols].astype(F32).T.astype(vo_ref.dtype)
        vo_ref[0, hh, 0, LANES:, :] = jnp.ones((vo_ref.shape[3] - LANES, vo_ref.shape[4]), vo_ref.dtype)


def _rope_tables(s):
    half = DH_A // 2
    inv = 1.0 / (ROPE_THETA ** (jnp.arange(0, DH_A, 2, dtype=F32) / DH_A))
    ang = jnp.arange(s, dtype=F32)[:, None] * inv[None, :]
    cos, sin = jnp.cos(ang), jnp.sin(ang)
    reps = LANES // half
    return jnp.tile(cos, (1, reps)), jnp.tile(jnp.concatenate([-sin, sin], axis=-1), (1, reps // 2))


def _attn_prepass(p3, w_a):
    b, s, _ = p3.shape
    h = w_a // LANES
    ts = _tile(s // 2, ATTN_KEY_CHUNK, LANES)
    cos, sin = _rope_tables(s)
    head_out = jax.ShapeDtypeStruct((b, h, s, LANES), BF16)
    vt_out = jax.ShapeDtypeStruct((b, h, s // ts, VT_ROWS, ts), BF16)
    hp = _tile(h, PREPASS_HEADS, 1)
    hg = h // hp
    col = lambda off: pl.BlockSpec((1, ts, hp * LANES), lambda bi, si, hi: (bi, si, off + hi))
    tab = pl.BlockSpec((ts, LANES), lambda bi, si, hi: (si, 0))
    out = pl.BlockSpec((1, hp, ts, LANES), lambda bi, si, hi: (bi, hi, si, 0))
    out_t = pl.BlockSpec((1, hp, 1, VT_ROWS, ts), lambda bi, si, hi: (bi, hi, si, 0, 0))
    return pl.pallas_call(
        _rope_kernel,
        grid=(b, s // ts, hg),
        in_specs=[col(0), col(hg), col(2 * hg), tab, tab],
        out_specs=[out, out, out_t],
        out_shape=[head_out, head_out, vt_out],
        compiler_params=_params(("parallel", "parallel", "arbitrary"), VMEM_SMALL),
        name="attn_prepass",
    )(p3, p3, p3, cos, sin)


def _attn_kernel(q_ref, k_ref, vt_ref, z_ref, lq1_ref, lk1_ref, lq2_ref, lk2_ref, g_ref, o_ref,
                 qq_scr, acc_scr, sa_scr, sb_scr, *, tq, qt, lam_init):
    nq = q_ref.shape[2] // tq
    nk = vt_ref.shape[2]
    tk = vt_ref.shape[4]
    tiles = [slice(c * qt, (c + 1) * qt) for c in range(2 * tq // qt)]
    lane = lax.broadcasted_iota(jnp.int32, (tq, LANES), 1)
    lam = (jnp.exp(jnp.sum(lq1_ref[...] * lk1_ref[...], keepdims=True))
           - jnp.exp(jnp.sum(lq2_ref[...] * lk2_ref[...], keepdims=True)) + lam_init)

    def load_queries(qi):
        rows = pl.ds(pl.multiple_of(qi * tq, tq), tq)
        q = q_ref[0, 0, rows, :].astype(F32)
        qq_scr[0:tq, :] = jnp.where(lane < DH_A, q, 0.0).astype(BF16)
        qq_scr[tq:2 * tq, :] = jnp.where(lane >= DH_A, q, 0.0).astype(BF16)

    def produce(i, s_scr):
        off = pl.multiple_of(i * tk, tk)
        ks = k_ref[0, 0, pl.ds(off, tk), :]
        s = [_dot_nt(ks, qq_scr[t, :]) for t in tiles]
        for t, x in zip(tiles, s):
            s_scr[:, t] = x
        return jnp.concatenate([jnp.max(x, axis=0, keepdims=True) for x in s], axis=1)

    def consume(i, s_scr, cmax, m_old):
        vt = vt_ref[0, 0, i]
        m_new = jnp.maximum(m_old, cmax)
        alpha = jnp.exp2(m_old - m_new)
        p = [jnp.exp2(s_scr[:, t] - m_new[:, t]).astype(BF16) for t in tiles]
        pv = [_dot(vt, x) for x in p]
        for t, x in zip(tiles, pv):
            acc_scr[:, t] = alpha[:, t] * acc_scr[:, t] + x
        return m_new

    def pair(j, carry):
        m, cmax_a = carry
        cmax_b = produce(2 * j + 1, sb_scr)
        m = consume(2 * j, sa_scr, cmax_a, m)
        cmax_a = produce(2 * j + 2, sa_scr)
        m = consume(2 * j + 1, sb_scr, cmax_b, m)
        return m, cmax_a

    m0 = jnp.full((1, 2 * tq), -jnp.inf, F32)
    trips = nk // 2 - 1
    unroll = next(u for u in (5, 3, 2, 1) if trips % u == 0 and (trips // u >= 2 or u == 1))

    def query_tile(qi, cmax_a):
        m, cmax_a = lax.fori_loop(0, trips, pair, (m0, cmax_a), unroll=unroll)
        cmax_b = produce(nk - 1, sb_scr)
        m = consume(nk - 2, sa_scr, cmax_a, m)
        load_queries(jnp.minimum(qi + 1, nq - 1))
        cmax_next = produce(0, sa_scr)
        consume(nk - 1, sb_scr, cmax_b, m)

        rows = pl.ds(pl.multiple_of(qi * tq, tq), tq)
        ot = acc_scr[0:LANES, :] * (1.0 / acc_scr[LANES:LANES + 1, :])
        acc_scr[...] = jnp.zeros(acc_scr.shape, F32)
        o = (ot[:, 0:tq] - lam * ot[:, tq:2 * tq]).T
        y = o * lax.rsqrt(jnp.mean(o * o, axis=-1, keepdims=True) + SUBLN_EPS) * g_ref[...] * (1.0 - lam_init)
        z = z_ref[0, rows, :].astype(F32)
        o_ref[0, rows, :] = (y * (z * _sigmoid(z))).astype(o_ref.dtype)
        return cmax_next

    load_queries(0)
    acc_scr[...] = jnp.zeros(acc_scr.shape, F32)
    lax.fori_loop(0, nq, query_tile, produce(0, sa_scr))


def _diff_attention(qr, kr, vt, p3, z_off, lam_q1, lam_k1, lam_q2, lam_k2, subln_g, lam_init):
    b, h, s, _ = qr.shape
    nk, tk = vt.shape[2], vt.shape[4]
    tq = _tile(s, TOKEN_ROWS, LANES)
    qt = _tile(2 * tq, CHANNEL_COLS, LANES)
    assert z_off % LANES == 0 and nk % 2 == 0
    zc = z_off // LANES
    vec = lambda n: pl.BlockSpec((1, n), lambda bi, hi: (0, 0))
    head = pl.BlockSpec((1, 1, s, LANES), lambda bi, hi: (bi, hi, 0, 0))
    kern = functools.partial(_attn_kernel, tq=tq, qt=qt, lam_init=lam_init)
    return pl.pallas_call(
        kern,
        grid=(b, h),
        in_specs=[
            head, head,
            pl.BlockSpec((1, 1, nk, VT_ROWS, tk), lambda bi, hi: (bi, hi, 0, 0, 0)),
            pl.BlockSpec((1, s, LANES), lambda bi, hi: (bi, 0, zc + hi)),
            vec(DH_A), vec(DH_A), vec(DH_A), vec(DH_A), vec(LANES),
        ],
        out_specs=pl.BlockSpec((1, s, LANES), lambda bi, hi: (bi, 0, hi)),
        out_shape=jax.ShapeDtypeStruct((b, s, h * LANES), BF16),
        scratch_shapes=[
            pltpu.VMEM((2 * tq, LANES), BF16),
            pltpu.VMEM((VT_ROWS, 2 * tq), F32),
            pltpu.VMEM((tk, 2 * tq), F32),
            pltpu.VMEM((tk, 2 * tq), F32),
        ],
        compiler_params=_params(("parallel", "arbitrary"), VMEM_BIG),
        name="diff_attn",
    )(qr, kr, vt, p3, lam_q1.reshape(1, DH_A), lam_k1.reshape(1, DH_A), lam_q2.reshape(1, DH_A),
      lam_k2.reshape(1, DH_A), subln_g.reshape(1, LANES))


def _group_sum(x, gmat):
    hi = x.astype(BF16)
    lo = (x - hi.astype(F32)).astype(BF16)
    return _dot(hi, gmat) + _dot(lo, gmat)


def _group_matrix(cw):
    r = lax.broadcasted_iota(jnp.int32, (cw, cw), 0) // N_B
    c = lax.broadcasted_iota(jnp.int32, (cw, cw), 1) // N_B
    return jnp.where(r == c, 1.0, 0.0).astype(BF16)


def _rwkv_prep_kernel(r_ref, k_ref, v_ref, lo_ref, rp_ref, kp_ref, vp_ref, lop_ref, rn_ref, kn_ref, vn_ref,
                      lon_ref, mu_ref, mulo_ref, w0_ref, a0_ref, wl_ref, al_ref, kk_ref, ka_ref, rk_ref,
                      ro_ref, vo_ref, ao_ref, lw_ref, kd_ref, bd_ref, bonus_ref, pw_scr, pa_scr):
    ts = r_ref.shape[1]
    si = pl.program_id(1)
    first = si == 0
    last = si == pl.num_programs(1) - 1

    def shift(cur_ref, prev_ref, next_ref, mu_p, mu_n):
        x = cur_ref[0].astype(F32)
        hp = prev_ref[0].astype(F32)
        hn = next_ref[0].astype(F32)
        pr = jnp.where(first, 0.0, hp[hp.shape[0] - 1:hp.shape[0], :])
        nx = jnp.where(last, 0.0, hn[0:1, :])
        row = lax.broadcasted_iota(jnp.int32, x.shape, 0)
        prev = jnp.where(row == 0, pr, pltpu.roll(x, 1, axis=0))
        nxt = jnp.where(row == ts - 1, nx, pltpu.roll(x, ts - 1, axis=0))
        return x + mu_p * (prev - x) + mu_n * (nxt - x)

    r = shift(r_ref, rp_ref, rn_ref, mu_ref[0, 0:1, :], mu_ref[1, 0:1, :])
    k = shift(k_ref, kp_ref, kn_ref, mu_ref[0, 1:2, :], mu_ref[1, 1:2, :])
    v = shift(v_ref, vp_ref, vn_ref, mu_ref[0, 2:3, :], mu_ref[1, 2:3, :])

    @pl.when(pl.program_id(2) == 0)
    def _():
        lo = shift(lo_ref, lop_ref, lon_ref, mulo_ref[0:1, :], mulo_ref[1:2, :])
        rl = lo.shape[1] // 4
        for d in range(2):
            pw_scr[d] = jnp.tanh(lo[:, d * rl:(d + 1) * rl]).astype(BF16)
            pa_scr[d] = lo[:, (2 + d) * rl:(3 + d) * rl].astype(BF16)

    cw = r.shape[1]
    gmat = _group_matrix(cw)
    kkh = k * kk_ref[...]
    nrm = jnp.sqrt(_group_sum(kkh * kkh, gmat))
    kk = kkh / jnp.maximum(nrm, 1e-12)
    ka = ka_ref[...]
    ro_ref[0] = r.astype(ro_ref.dtype)
    vo_ref[0] = v.astype(vo_ref.dtype)
    ao_ref[0] = (-kk).astype(ao_ref.dtype)
    ksum = jnp.zeros_like(k)
    for d in range(2):
        pw = pw_scr[d]
        pa = pa_scr[d]
        wl = w0_ref[d:d + 1, :] + _dot(pw, wl_ref[d])
        lw_ref[d, 0] = -math.exp(-0.5) * _sigmoid(wl)
        a = _sigmoid(a0_ref[d:d + 1, :] + _dot(pa, al_ref[d]))
        kdir = k * (1.0 + (a - 1.0) * ka)
        kd_ref[d, 0] = kdir.astype(kd_ref.dtype)
        bd_ref[d, 0] = (kk * a).astype(bd_ref.dtype)
        ksum = ksum + kdir
    bonus_ref[0] = _group_sum(r * ksum * rk_ref[...], gmat) * v


def _rwkv_prep(p3, s_off, w_b, r_lora, mu_prev, mu_next, w0, w_lora, a0, a_lora, k_k, k_a, r_k):
    b, s, _ = p3.shape
    ts = _tile(s, RWKV_ROWS, BF16_ROWS)
    cw = _tile(w_b, CHANNEL_COLS, LANES)
    lw = 4 * r_lora
    halo = BF16_ROWS
    nblk = s // halo
    assert s_off % cw == 0 and w_b % cw == 0 and (s_off + 3 * w_b) % lw == 0 and ts % halo == 0
    rc, kc, vc, lc = s_off // cw, (s_off + w_b) // cw, (s_off + 2 * w_b) // cw, (s_off + 3 * w_b) // lw
    per = ts // halo

    cur = lambda off: pl.BlockSpec((1, ts, cw), lambda bi, si, ji: (bi, si, off + ji))
    prv = lambda off: pl.BlockSpec((1, halo, cw), lambda bi, si, ji: (bi, jnp.maximum(si * per - 1, 0), off + ji))
    nxt = lambda off: pl.BlockSpec((1, halo, cw),
                                   lambda bi, si, ji: (bi, jnp.minimum((si + 1) * per, nblk - 1), off + ji))
    lo_cur = pl.BlockSpec((1, ts, lw), lambda bi, si, ji: (bi, si, lc))
    lo_prv = pl.BlockSpec((1, halo, lw), lambda bi, si, ji: (bi, jnp.maximum(si * per - 1, 0), lc))
    lo_nxt = pl.BlockSpec((1, halo, lw), lambda bi, si, ji: (bi, jnp.minimum((si + 1) * per, nblk - 1), lc))
    colvec = lambda rows: pl.BlockSpec((rows, cw), lambda bi, si, ji: (0, ji))

    mu_rkv = jnp.stack([mu_prev[:3 * w_b].reshape(3, w_b), mu_next[:3 * w_b].reshape(3, w_b)])
    mu_lo = jnp.stack([mu_prev[3 * w_b:], mu_next[3 * w_b:]])
    tok = pl.BlockSpec((1, ts, cw), lambda bi, si, ji: (bi, si, ji))
    tok2 = pl.BlockSpec((2, 1, ts, cw), lambda bi, si, ji: (0, bi, si, ji))
    shp = lambda dt: jax.ShapeDtypeStruct((b, s, w_b), dt)
    shp2 = lambda dt: jax.ShapeDtypeStruct((2, b, s, w_b), dt)
    return pl.pallas_call(
        _rwkv_prep_kernel,
        grid=(b, s // ts, w_b // cw),
        in_specs=[cur(rc), cur(kc), cur(vc), lo_cur, prv(rc), prv(kc), prv(vc), lo_prv,
                  nxt(rc), nxt(kc), nxt(vc), lo_nxt,
                  pl.BlockSpec((2, 3, cw), lambda bi, si, ji: (0, 0, ji)),
                  pl.BlockSpec((2, lw), lambda bi, si, ji: (0, 0)),
                  colvec(2), colvec(2),
                  pl.BlockSpec((2, r_lora, cw), lambda bi, si, ji: (0, 0, ji)),
                  pl.BlockSpec((2, r_lora, cw), lambda bi, si, ji: (0, 0, ji)),
                  colvec(1), colvec(1), colvec(1)],
        out_specs=[tok, tok, tok, tok2, tok2, tok2, tok],
        out_shape=[shp(BF16), shp(BF16), shp(BF16), shp2(F32), shp2(BF16), shp2(BF16), shp(F32)],
        scratch_shapes=[pltpu.VMEM((2, ts, r_lora), BF16), pltpu.VMEM((2, ts, r_lora), BF16)],
        compiler_params=_params(("parallel", "parallel", "arbitrary"), VMEM_BIG),
        name="rwkv_prep",
    )(p3, p3, p3, p3, p3, p3, p3, p3, p3, p3, p3, p3, mu_rkv, mu_lo, w0, a0,
      w_lora.astype(BF16), a_lora.astype(BF16), k_k.reshape(1, w_b), k_a.reshape(1, w_b), r_k.reshape(1, w_b))


def _wkv_kernel(mask_ref, r_ref, a_ref, v_ref, lw_ref, k_ref, b_ref, y_ref, z_scr, *, npairs, reverse):
    @pl.when(pl.program_id(2) == 0)
    def _():
        z_scr[...] = jnp.zeros(z_scr.shape, F32)

    z = [z_scr[j] for j in range(npairs)]
    nsub = r_ref.shape[1] // CHUNK
    for sub in (reversed(range(nsub)) if reverse else range(nsub)):
        rows = slice(sub * CHUNK, (sub + 1) * CHUNK)
        z = _wkv_chunk(mask_ref, r_ref.at[:, rows, :], a_ref.at[:, rows, :], v_ref.at[:, rows, :],
                       lw_ref.at[:, :, rows, :], k_ref.at[:, :, rows, :], b_ref.at[:, :, rows, :],
                       y_ref.at[:, rows, :], z, npairs=npairs, reverse=reverse)
    for j in range(npairs):
        z_scr[j] = z[j]


def _wkv_chunk(mask_ref, r_ref, a_ref, v_ref, lw_ref, k_ref, b_ref, y_ref, z_in, *, npairs, reverse):
    assert CHUNK == N_B and 2 * N_B == LANES
    c = CHUNK
    strict = mask_ref[0]
    incl = mask_ref[1]
    cum = incl[0:c, 0:c].astype(BF16)
    lane = lax.broadcasted_iota(jnp.int32, (c, LANES), 1)
    low = lane < N_B
    row2 = lax.broadcasted_iota(jnp.int32, (LANES, LANES), 0)
    col2 = lax.broadcasted_iota(jnp.int32, (LANES, LANES), 1)
    own = (row2 < c) == (col2 < N_B)

    def stack(x):
        return jnp.concatenate([jnp.where(low, x, 0.0), jnp.where(low, 0.0, x)], axis=0)

    def each(f, *cols):
        return [f(*xs) for xs in zip(*cols)]

    sls = [slice(j * LANES, (j + 1) * LANES) for j in range(npairs)]
    lw = [lw_ref[0, 0, :, sl] for sl in sls]
    r = [r_ref[0, :, sl].astype(F32) for sl in sls]
    a = [a_ref[0, :, sl].astype(F32) for sl in sls]
    v = [v_ref[0, :, sl].astype(F32) for sl in sls]
    k = [k_ref[0, 0, :, sl].astype(F32) for sl in sls]
    b = [b_ref[0, 0, :, sl].astype(F32) for sl in sls]
    zb = [stack(z).astype(BF16) for z in z_in]
    strict_p = strict[0:c, :] + strict[c:, :]
    incl_p = incl[0:c, :] + incl[c:, :]
    eye_p = lax.broadcasted_iota(jnp.int32, (c, LANES), 0) == (lane % N_B)

    def fold(x):
        return x[0:c, :] + x[c:, :]

    hi = each(lambda x: x.astype(BF16), lw)
    lo = each(lambda x, h: (x - h.astype(F32)).astype(BF16), lw, hi)
    cum2 = jnp.concatenate([cum, cum], axis=1)
    lam = each(lambda h, l: _dot(cum2, jnp.concatenate([h, l], axis=0)), hi, lo)
    tot = each(lambda x: jnp.sum(x, axis=0, keepdims=True), lw)
    gdec = each(jnp.exp, tot)
    a_p = each(lambda x, lm, w: x * jnp.exp(lm - w), a, lam, lw)
    r_p = each(lambda x, lm: x * jnp.exp(lm), r, lam)
    a_s = each(stack, a_p)
    e_neg = each(lambda lm: jnp.exp(-lm), lam)
    e_rem = each(lambda t, lm: jnp.exp(t - lm), tot, lam)
    b_s = each(lambda x, e: stack(x * e).astype(BF16), b, e_neg)
    k_s = each(lambda x, e: stack(x * e).astype(BF16), k, e_neg)
    bh_s = each(lambda x, e: stack(x * e).astype(BF16), b, e_rem)
    kh_s = each(lambda x, e: stack(x * e).astype(BF16), k, e_rem)
    v_s = each(lambda x: stack(x).astype(BF16), v)

    m4 = each(lambda x, y, p, q: _dot_nt(jnp.concatenate([x, y], axis=0).astype(BF16),
                                         jnp.concatenate([p, q], axis=0)), a_p, r_p, b_s, k_s)
    lp = each(lambda m: stack(m[0:c, 0:LANES] * strict_p).astype(BF16), m4)
    lak = each(lambda m: stack(m[0:c, LANES:] * strict_p).astype(BF16), m4)
    mrbk = each(lambda m: jnp.concatenate([stack(m[c:, 0:LANES] * incl_p), stack(m[c:, LANES:] * incl_p)],
                                          axis=1).astype(BF16), m4)

    half = LANES // 2
    x = each(lambda p, q, w: p + pltpu.roll(_dot(q, w), half, axis=1), a_s, lak, v_s)

    def live(arr, p):
        lo0 = 0 if reverse else p
        return jnp.concatenate([arr[lo0:lo0 + c - p], arr[c + lo0:c + lo0 + c - p]], axis=0)

    def add_live(full, part, p):
        n, lo0 = c - p, (0 if reverse else p)
        pieces = []
        for blk in range(2):
            base = blk * c
            pieces += [full[base:base + lo0], full[base + lo0:base + lo0 + n] + part[blk * n:(blk + 1) * n],
                       full[base + lo0 + n:base + c]]
        return jnp.concatenate([q for q in pieces if q.shape[0]], axis=0)

    steps = int(math.log2(c))
    for it in range(steps - 1):
        p = 2 ** it
        if p < BF16_ROWS:
            res = each(lambda l, q: _dot(l, jnp.concatenate([q.astype(BF16), l], axis=1)), lp, x)
            x = each(lambda q, s: q + s[:, 0:LANES], x, res)
            lp = each(lambda s: s[:, LANES:].astype(BF16), res)
        else:
            res = each(lambda l, q: _dot(live(l, p), jnp.concatenate([q.astype(BF16), l], axis=1)), lp, x)
            x = each(lambda q, s: add_live(q, s[:, 0:LANES], p), x, res)
            lp = each(lambda s: add_live(jnp.zeros((2 * c, LANES), F32), s[:, LANES:], p).astype(BF16), res)
    p = 2 ** (steps - 1)
    x = each(lambda l, q: add_live(q, _dot(live(l, p), q.astype(BF16)), p), lp, x)
    w_s = each(lambda q: jnp.where(own, q, 0.0).astype(BF16), x)
    u_s = each(lambda q: jnp.where(own, pltpu.roll(q, half, axis=1), 0.0).astype(BF16), x)

    zero = jnp.zeros((LANES, LANES), BF16)
    wuv = each(lambda w, u, vv: jnp.concatenate([jnp.concatenate([w, u], axis=1),
                                                 jnp.concatenate([zero, vv], axis=1)], axis=0), w_s, u_s, v_s)
    ry = each(_dot, mrbk, wuv)
    pq = each(lambda p, q, m: _dot_tn(jnp.concatenate([p, q], axis=0), m), bh_s, kh_s, wuv)
    rw = each(lambda p, s: (p + fold(s[:, 0:LANES])).astype(BF16), r_p, ry)
    pt = each(lambda g, s: (jnp.where(eye_p, g, 0.0) + fold(s[:, 0:LANES])).astype(BF16), gdec, pq)

    ys = each(lambda p, z, s: _dot(p, z) + fold(s[:, LANES:]), rw, zb, ry)
    z_new = each(lambda p, z, s: _dot(p, z) + fold(s[:, LANES:]), pt, zb, pq)
    for j, sl in enumerate(sls):
        y_ref[0, :, sl] = ys[j].astype(y_ref.dtype)
    return z_new


def _wkv_masks(reverse):
    i = np.arange(LANES)
    same = (i[:, None] // CHUNK) == (i[None, :] // CHUNK)
    t, s = i[:, None] % CHUNK, i[None, :] % CHUNK
    before = (s > t) if reverse else (s < t)
    return jnp.asarray(np.stack([same & before, same & (before | (s == t))]).astype(np.float32))


def _wkv_scan(r, a, v, logw, kdir, bdir, reverse):
    b, s, w_b = r.shape
    c = _tile(s, WKV_STEP_CHUNKS * CHUNK, CHUNK)
    nc = s // c
    d = int(reverse)
    gw = _tile(w_b, WKV_GROUP_LANES, LANES)
    npairs = gw // LANES
    chunk_of = (lambda ci: nc - 1 - ci) if reverse else (lambda ci: ci)
    shared = pl.BlockSpec((1, c, gw), lambda bi, gi, ci: (bi, chunk_of(ci), gi))
    perdir = pl.BlockSpec((1, 1, c, gw), lambda bi, gi, ci: (d, bi, chunk_of(ci), gi))
    kern = functools.partial(_wkv_kernel, npairs=npairs, reverse=reverse)
    return pl.pallas_call(
        kern,
        grid=(b, w_b // gw, nc),
        in_specs=[pl.BlockSpec((2, LANES, LANES), lambda bi, gi, ci: (0, 0, 0)),
                  shared, shared, shared, perdir, perdir, perdir],
        out_specs=shared,
        out_shape=jax.ShapeDtypeStruct((b, s, w_b), BF16),
        scratch_shapes=[pltpu.VMEM((npairs, N_B, LANES), F32)],
        compiler_params=_params(("parallel", "parallel", "arbitrary"), VMEM_SMALL),
        name="wkv_scan_rev" if reverse else "wkv_scan_fwd",
    )(_wkv_masks(reverse), r, a, v, logw, kdir, bdir)


def _rwkv_out_kernel(yf_ref, yb_ref, bonus_ref, z_ref, g_ref, b_ref, o_ref):
    y = yf_ref[0].astype(F32) + yb_ref[0].astype(F32)
    gmat = _group_matrix(y.shape[1])
    mu = _group_sum(y, gmat) * (1.0 / N_B)
    yc = y - mu
    var = _group_sum(yc * yc, gmat) * (1.0 / N_B)
    yn = yc * lax.rsqrt(var + LNX_EPS) * g_ref[...] + b_ref[...]
    z = z_ref[0].astype(F32)
    o_ref[0] = ((yn + bonus_ref[0]) * (z * _sigmoid(z))).astype(o_ref.dtype)


def _rwkv_out(y_fwd, y_rev, bonus, p3, z_off, lnx_g, lnx_b):
    b, s, w_b = y_fwd.shape
    ts = _tile(s, RWKV_OUT_ROWS, BF16_ROWS)
    cw = _tile(w_b, CHANNEL_COLS, LANES)
    assert z_off % cw == 0
    zc = z_off // cw
    vec = pl.BlockSpec((1, cw), lambda bi, si, ji: (0, ji))
    tok = pl.BlockSpec((1, ts, cw), lambda bi, si, ji: (bi, si, ji))
    return pl.pallas_call(
        _rwkv_out_kernel,
        grid=(b, s // ts, w_b // cw),
        in_specs=[tok, tok, tok,
                  pl.BlockSpec((1, ts, cw), lambda bi, si, ji: (bi, si, zc + ji)),
                  vec, vec],
        out_specs=pl.BlockSpec((1, ts, cw), lambda bi, si, ji: (bi, si, ji)),
        out_shape=jax.ShapeDtypeStruct((b, s, w_b), BF16),
        compiler_params=_params(("parallel", "parallel", "parallel"), VMEM_SMALL),
        name="rwkv_out",
    )(y_fwd, y_rev, bonus, p3, lnx_g.reshape(1, w_b), lnx_b.reshape(1, w_b))


def _merge_kernel(ua_ref, ub_ref, wa_ref, wb_ref, ga_ref, gb_ref, o_ref):
    oa = _dot(ua_ref[...], wa_ref[...])
    ob = _dot(ub_ref[...], wb_ref[...])
    m = _sigmoid(ga_ref[...].astype(F32)) * oa + _sigmoid(gb_ref[...].astype(F32)) * ob
    o_ref[...] = m.astype(o_ref.dtype)


def _merge(ua, ub, w_oa, w_ob, p2, g_off):
    t, w_a = ua.shape
    w_b = ub.shape[1]
    d = w_oa.shape[1]
    tm = _tile(t, PROJ_ROWS, BF16_ROWS)
    tn = _tile(d, PROJ_COLS, MXU_COLS)
    assert g_off % tn == 0 and d % tn == 0
    gc = g_off // tn
    nd = d // tn
    return pl.pallas_call(
        _merge_kernel,
        grid=(t // tm, nd),
        in_specs=[pl.BlockSpec((tm, w_a), lambda i, j: (i, 0)),
                  pl.BlockSpec((tm, w_b), lambda i, j: (i, 0)),
                  pl.BlockSpec((w_a, tn), lambda i, j: (0, j)),
                  pl.BlockSpec((w_b, tn), lambda i, j: (0, j)),
                  pl.BlockSpec((tm, tn), lambda i, j: (i, gc + j)),
                  pl.BlockSpec((tm, tn), lambda i, j: (i, gc + nd + j))],
        out_specs=pl.BlockSpec((tm, tn), lambda i, j: (i, j)),
        out_shape=jax.ShapeDtypeStruct((t, d), BF16),
        compiler_params=_params(("parallel", "arbitrary"), VMEM_MID),
        name="gate_merge",
    )(ua, ub, w_oa, w_ob, p2, p2)


def _out_kernel(m_ref, w_ref, x_ref, o_ref):
    o_ref[...] = x_ref[...] + _dot(m_ref[...], w_ref[...])


def _out_norm_kernel(m_ref, w_ref, x_ref, g_ref, o_ref, res_scr, ss_scr):
    j = pl.program_id(1)
    nj, _, tn = res_scr.shape
    res = x_ref[...] + _dot(m_ref[...], w_ref[...])
    res_scr[j] = res

    @pl.when(j == 0)
    def _():
        ss_scr[...] = jnp.zeros(ss_scr.shape, F32)

    ss_scr[...] += jnp.sum(res * res, axis=-1, keepdims=True)

    @pl.when(j == nj - 1)
    def _():
        scale = lax.rsqrt(ss_scr[...] * (1.0 / (nj * tn)) + NORM_EPS)
        for jj in range(nj):
            cols = slice(jj * tn, (jj + 1) * tn)
            o_ref[:, cols] = res_scr[jj] * scale * g_ref[:, cols]


def _out_proj(m, w_out, x2, final_g, final_norm):
    t, d = x2.shape
    tm = _tile(t, TOKEN_ROWS, BF16_ROWS)
    tn = _tile(d, 2 * PROJ_COLS, MXU_COLS)
    in_specs = [pl.BlockSpec((tm, d), lambda i, j: (i, 0), pipeline_mode=pl.Buffered(1)),
                pl.BlockSpec((d, tn), lambda i, j: (0, j)),
                pl.BlockSpec((tm, tn), lambda i, j: (i, j))]
    if not final_norm:
        return pl.pallas_call(
            _out_kernel,
            grid=(t // tm, d // tn),
            in_specs=in_specs,
            out_specs=pl.BlockSpec((tm, tn), lambda i, j: (i, j)),
            out_shape=jax.ShapeDtypeStruct((t, d), F32),
            compiler_params=_params(("parallel", "arbitrary"), VMEM_BIG),
            name="out_proj",
        )(m, w_out, x2)
    return pl.pallas_call(
        _out_norm_kernel,
        grid=(t // tm, d // tn),
        in_specs=in_specs + [pl.BlockSpec((1, d), lambda i, j: (0, 0))],
        out_specs=pl.BlockSpec((tm, d), lambda i, j: (i, 0)),
        out_shape=jax.ShapeDtypeStruct((t, d), F32),
        scratch_shapes=[pltpu.VMEM((d // tn, tm, tn), F32), pltpu.VMEM((tm, 1), F32)],
        compiler_params=_params(("parallel", "arbitrary"), VMEM_BIG),
        name="out_proj_norm",
    )(m, w_out, x2, final_g.reshape(1, d))


def _lambda_init(layer_idx):
    return 0.8 - 0.6 * math.exp(-0.3 * layer_idx)


def _mixer_layer(x, l, prm, final_g, final_norm):
    b, s, d = x.shape
    w_a = prm["w_oA"][l].shape[0]
    w_b = prm["w_oB"][l].shape[0]
    r_lora = prm["w_lora"].shape[2]
    c_shift = 3 * w_b + 4 * r_lora
    n_in = prm["w_in"].shape[2]
    assert n_in == 3 * w_a + c_shift + w_a + w_b + 2 * d
    assert w_a % LANES == 0 and w_b % LANES == 0 and s % CHUNK == 0
    s_off = 3 * w_a
    za_off = s_off + c_shift
    zb_off = za_off + w_a
    g_off = zb_off + w_b
    lam_init = _lambda_init(l)

    x2 = x.reshape(b * s, d)
    h = _rmsnorm_bf16(x2, prm["norm_g"][l])
    p2 = _matmul_bf16(h, prm["w_in"][l])
    p3 = p2.reshape(b, s, n_in)

    qr, kr, vt = _attn_prepass(p3, w_a)
    ua = _diff_attention(qr, kr, vt, p3, za_off, prm["lam_q1"][l], prm["lam_k1"][l], prm["lam_q2"][l],
                         prm["lam_k2"][l], prm["subln_g"][l], lam_init)

    r, v, a, logw, kdir, bdir, bonus = _rwkv_prep(
        p3, s_off, w_b, r_lora, prm["mu_prev"][l], prm["mu_next"][l], prm["w0"][l], prm["w_lora"][l],
        prm["a0"][l], prm["a_lora"][l], prm["k_k"][l], prm["k_a"][l], prm["r_k"][l])
    y_fwd = _wkv_scan(r, a, v, logw, kdir, bdir, reverse=False)
    y_rev = _wkv_scan(r, a, v, logw, kdir, bdir, reverse=True)
    ub = _rwkv_out(y_fwd, y_rev, bonus, p3, zb_off, prm["lnx_g"][l], prm["lnx_b"][l])

    m = _merge(ua.reshape(b * s, w_a), ub.reshape(b * s, w_b), prm["w_oA"][l].astype(BF16),
               prm["w_oB"][l].astype(BF16), p2, g_off)
    out = _out_proj(m, prm["w_out"][l].astype(BF16), x2, final_g, final_norm)
    return out.reshape(b, s, d)


def kernel(x_prompt, x_sample, norm_g, w_in, mu_prev, mu_next, lam_q1, lam_k1, lam_q2, lam_k2, subln_g, w0,
           w_lora, a0, a_lora, k_k, k_a, r_k, lnx_g, lnx_b, w_oA, w_oB, w_out, final_g):
    prm = dict(norm_g=norm_g, w_in=w_in, mu_prev=mu_prev, mu_next=mu_next, lam_q1=lam_q1, lam_k1=lam_k1,
               lam_q2=lam_q2, lam_k2=lam_k2, subln_g=subln_g, w0=w0, w_lora=w_lora, a0=a0, a_lora=a_lora,
               k_k=k_k, k_a=k_a, r_k=r_k, lnx_g=lnx_g, lnx_b=lnx_b, w_oA=w_oA, w_oB=w_oB, w_out=w_out)
    depth = norm_g.shape[0]

    def trunk(x):
        for l in range(depth):
            x = _mixer_layer(x, l, prm, final_g, final_norm=(l == depth - 1))
        return x

    return (trunk(x_prompt), trunk(x_sample))
```
